```python
import jax
import jax.numpy as jnp
from jax import lax
import numpy as np

D_MODEL = 1024
BATCH = 16
SEQ = 2048
DEPTH = 4

GRID_W = 64
CTX_LEN = 256
HEAD_DIM = 64
ROT_FREQS = HEAD_DIM // 4
ROPE_THETA = 10000.0
NORM_EPS = 1e-5

POOL_WINDOWS = (2, 4, 8, 16)
POOL_GROUP = 64
POOL_WIDTH = POOL_GROUP * len(POOL_WINDOWS)
ATT_HEADS = 12
ATT_KV_HEADS = 3
ATT_GROUP = ATT_HEADS // ATT_KV_HEADS
ATT_WINDOW = 128
ATT_BLOCK = 128
ATT_Q = ATT_HEADS * HEAD_DIM
ATT_KV = ATT_KV_HEADS * HEAD_DIM
OFF_K = POOL_WIDTH + ATT_Q
OFF_V = OFF_K + ATT_KV
EVEN_IN = OFF_V + ATT_KV
EVEN_MIX = POOL_WIDTH + ATT_Q

RWKV_HEADS = 12
RWKV_WIDTH = RWKV_HEADS * HEAD_DIM
DECAY_LORA = 64
ICLR_LORA = 64
GATE_LORA = 160
GN_EPS = 64e-5
STATE_LO = RWKV_WIDTH
STATE_HI = 3 * RWKV_WIDTH + 2 * DECAY_LORA + 2 * ICLR_LORA
RWKV_IN = STATE_HI + GATE_LORA
FNET_GROUPS = 4
FNET_GROUP = 64
FNET_WIDTH = FNET_GROUPS * FNET_GROUP
ODD_IN = RWKV_IN + FNET_WIDTH
ODD_MIX = RWKV_WIDTH + FNET_WIDTH

N_EXPERTS = 32
TOP_K = 4
D_EXPERT = D_MODEL
SWIGLU_LIMIT = 7.0
SWIGLU_ALPHA = 1.702
MOE_BLOCK = 256

kernel_name = 'hybrid_pool_swa_rwkv7_fnet_moe_dit'


def rms_norm(x, g):
    xf = x.astype(jnp.float32)
    xf = xf * lax.rsqrt(jnp.mean(xf * xf, axis=-1, keepdims=True) + NORM_EPS)
    return (xf * g.astype(jnp.float32)).astype(x.dtype)


def modulate(h, shift, scale):
    return h * (1 + scale[:, None, :]) + shift[:, None, :]


def axial_rope_tables(n_tokens):
    rows = n_tokens // GRID_W
    row = jnp.repeat(jnp.arange(rows, dtype=jnp.float32), GRID_W)
    col = jnp.tile(jnp.arange(GRID_W, dtype=jnp.float32), rows)
    inv_freq = ROPE_THETA ** (-jnp.arange(ROT_FREQS, dtype=jnp.float32) / ROT_FREQS)
    ang = jnp.stack([row[:, None] * inv_freq, col[:, None] * inv_freq], axis=1)
    return jnp.cos(ang), jnp.sin(ang)


def apply_axial_rope(x, cos, sin):
    xs = x.astype(jnp.float32).reshape(x.shape[:-1] + (2, 2, ROT_FREQS))
    x1, x2 = xs[..., 0, :], xs[..., 1, :]
    c, s = cos[:, None], sin[:, None]
    out = jnp.stack([x1 * c - x2 * s, x2 * c + x1 * s], axis=-2)
    return out.reshape(x.shape).astype(x.dtype)


def multiscale_pool(u, pool_w, pool_scale):
    B, T, _ = u.shape
    uf = u.astype(jnp.float32).reshape(B, T, len(POOL_WINDOWS), POOL_GROUP)
    cs = jnp.pad(lax.cumsum(uf, axis=1), ((0, 0), (1, 0), (0, 0), (0, 0)))
    t = jnp.arange(T)
    means = []
    for gi, w in enumerate(POOL_WINDOWS):
        lo = jnp.clip(t - w // 2, 0, T)
        hi = jnp.clip(t - w // 2 + w, 0, T)
        win = cs[:, hi, gi] - cs[:, lo, gi]
        means.append(win / (hi - lo).astype(jnp.float32)[None, :, None])
    pooled = (jnp.stack(means, axis=2) - uf).astype(u.dtype)
    mixed = jnp.einsum('btgc,gcd->btgd', pooled, pool_w)
    return mixed.reshape(B, T, POOL_WIDTH) * pool_scale


def sink_softmax(scores, sink):
    full = jnp.concatenate([scores, jnp.broadcast_to(sink, scores.shape[:-1] + (1,))], axis=-1)
    return jax.nn.softmax(full, axis=-1)[..., :-1]


def latent_window_attention(q, k, v, kc, vc, sink):
    B, S = q.shape[:2]
    nb = S // ATT_BLOCK
    span = ATT_BLOCK + 2 * ATT_WINDOW
    scale = HEAD_DIM ** -0.5
    qb = q.reshape(B, nb, ATT_BLOCK, ATT_KV_HEADS, ATT_GROUP, HEAD_DIM)
    pad = ((0, 0), (ATT_WINDOW, ATT_WINDOW), (0, 0), (0, 0))
    kp, vp = jnp.pad(k, pad), jnp.pad(v, pad)
    sink_b = sink.astype(jnp.float32).reshape(1, ATT_KV_HEADS, ATT_GROUP, 1, 1)
    rel = jnp.arange(span)[None, :] - jnp.arange(ATT_BLOCK)[:, None]
    band = (rel >= 0) & (rel <= 2 * ATT_WINDOW)

    def block(j):
        qj = lax.dynamic_index_in_dim(qb, j, axis=1, keepdims=False)
        kj = lax.dynamic_slice_in_dim(kp, j * ATT_BLOCK, span, axis=1)
        vj = lax.dynamic_slice_in_dim(vp, j * ATT_BLOCK, span, axis=1)
        key_pos = j * ATT_BLOCK - ATT_WINDOW + jnp.arange(span)
        mask = band & ((key_pos >= 0) & (key_pos < S))[None, :]
        s_loc = jnp.einsum('bqkgd,bskd->bkgqs', qj, kj).astype(jnp.float32) * scale
        s_loc = jnp.where(mask, s_loc, -jnp.inf)
        s_ctx = jnp.einsum('bqkgd,bckd->bkgqc', qj, kc).astype(jnp.float32) * scale
        p = sink_softmax(jnp.concatenate([s_loc, s_ctx], axis=-1), sink_b).astype(v.dtype)
        o = (jnp.einsum('bkgqs,bskd->bqkgd', p[..., :span], vj)
             + jnp.einsum('bkgqc,bckd->bqkgd', p[..., span:], vc))
        return o.reshape(B, ATT_BLOCK, ATT_Q)

    out = lax.map(block, jnp.arange(nb))
    return jnp.moveaxis(out, 0, 1).reshape(B, S, ATT_Q)


def context_attention(qc, kc, vc, sink):
    B, C = qc.shape[:2]
    qg = qc.reshape(B, C, ATT_KV_HEADS, ATT_GROUP, HEAD_DIM)
    s = jnp.einsum('bqkgd,bckd->bkgqc', qg, kc).astype(jnp.float32) * HEAD_DIM ** -0.5
    p = sink_softmax(s, sink.astype(jnp.float32).reshape(1, ATT_KV_HEADS, ATT_GROUP, 1, 1)).astype(vc.dtype)
    return jnp.einsum('bkgqc,bckd->bqkgd', p, vc).reshape(B, C, ATT_Q)


def even_mixer(hx, hc, w_in, w_out, pool_w, pool_scale, sink, cos, sin, ctx_out):
    B, S, _ = hx.shape
    C = hc.shape[1]
    px = hx @ w_in
    qx = apply_axial_rope(px[..., POOL_WIDTH:OFF_K].reshape(B, S, ATT_HEADS, HEAD_DIM), cos, sin)
    kx = apply_axial_rope(px[..., OFF_K:OFF_V].reshape(B, S, ATT_KV_HEADS, HEAD_DIM), cos, sin)
    vx = px[..., OFF_V:].reshape(B, S, ATT_KV_HEADS, HEAD_DIM)
    pc = hc @ w_in if ctx_out else hc @ w_in[:, OFF_K:]
    kc = pc[..., -2 * ATT_KV:-ATT_KV].reshape(B, C, ATT_KV_HEADS, HEAD_DIM)
    vc = pc[..., -ATT_KV:].reshape(B, C, ATT_KV_HEADS, HEAD_DIM)
    att_x = latent_window_attention(qx, kx, vx, kc, vc, sink)
    pool_x = multiscale_pool(px[..., :POOL_WIDTH], pool_w, pool_scale)
    yx = jnp.concatenate([pool_x, att_x], axis=-1) @ w_out
    if not ctx_out:
        return yx, None
    qc = pc[..., POOL_WIDTH:OFF_K].reshape(B, C, ATT_HEADS, HEAD_DIM)
    att_c = context_attention(qc, kc, vc, sink)
    pool_c = multiscale_pool(pc[..., :POOL_WIDTH], pool_w, pool_scale)
    yc = jnp.concatenate([pool_c, att_c], axis=-1) @ w_out
    return yx, yc


def token_shift(u, mu):
    prev = jnp.pad(u[:, :-1], ((0, 0), (1, 0), (0, 0)))
    nxt = jnp.pad(u[:, 1:], ((0, 0), (0, 1), (0, 0)))
    return u + mu[0] * (prev - u) + mu[1] * (nxt - u)


def to_heads_tm(z):
    B, T, _ = z.shape
    return jnp.moveaxis(z.astype(jnp.float32).reshape(B, T, RWKV_HEADS, HEAD_DIM), 1, 0)


def rwkv_scan_inputs(fs, w0, w2, a0, a2, k_k, k_a):
    B, T, _ = fs.shape
    W = RWKV_WIDTH
    k, v = fs[..., :W], fs[..., W:2 * W]
    o_w = 2 * W
    o_a = o_w + 2 * DECAY_LORA
    kk = (k * k_k).astype(jnp.float32).reshape(B, T, RWKV_HEADS, HEAD_DIM)
    kk = kk / jnp.maximum(jnp.sqrt(jnp.sum(kk * kk, axis=-1, keepdims=True)), 1e-12)
    kk = kk.reshape(B, T, W)
    kf = k.astype(jnp.float32)
    dirs = []
    for d in range(2):
        wd = fs[..., o_w + d * DECAY_LORA:o_w + (d + 1) * DECAY_LORA]
        ad = fs[..., o_a + d * ICLR_LORA:o_a + (d + 1) * ICLR_LORA]
        w_log = -jax.nn.softplus(-(w0[d] + jnp.tanh(wd) @ w2[d]).astype(jnp.float32)) - 0.5
        decay = jnp.exp(-jnp.exp(w_log))
        a = jax.nn.sigmoid((a0[d] + ad @ a2[d]).astype(jnp.float32))
        k_d = kf * (1.0 + (a - 1.0) * k_a.astype(jnp.float32))
        dirs.append((decay, k_d, kk * a))
    return v, kk, dirs


def wkv_scan(state0, scan_in, d, r, reverse):
    v, kk, dirs = scan_in
    decay, k_d, b = dirs[d]
    seq = tuple(to_heads_tm(z) for z in (decay, k_d, v, kk, b))
    if r is not None:
        seq = seq + (to_heads_tm(r),)

    def step(S, inp):
        w_t, k_t, v_t, kk_t, b_t = inp[:5]
        sa = jnp.einsum('bhij,bhj->bhi', S, kk_t)
        S = S * w_t[:, :, None, :] - sa[..., None] * b_t[:, :, None, :] + v_t[..., None] * k_t[:, :, None, :]
        y = jnp.einsum('bhij,bhj->bhi', S, inp[5]) if len(inp) == 6 else None
        return S, y

    return lax.scan(step, state0, seq, reverse=reverse)


def rwkv_readout(y_tm, r, scan_in, g_cols, g2, r_k, gn_g, gn_b):
    v, _, dirs = scan_in
    y = jnp.moveaxis(y_tm, 0, 1)
    B, T = y.shape[:2]
    mean = jnp.mean(y, axis=-1, keepdims=True)
    var = jnp.mean(jnp.square(y - mean), axis=-1, keepdims=True)
    yn = ((y - mean) * lax.rsqrt(var + GN_EPS)).reshape(B, T, RWKV_WIDTH) * gn_g + gn_b
    rh = r.astype(jnp.float32).reshape(B, T, RWKV_HEADS, HEAD_DIM)
    vh = v.astype(jnp.float32).reshape(B, T, RWKV_HEADS, HEAD_DIM)
    coef = sum(jnp.sum(rh * dd[1].reshape(B, T, RWKV_HEADS, HEAD_DIM) * r_k, axis=-1, keepdims=True)
               for dd in dirs)
    out = yn + (coef * vh).reshape(B, T, RWKV_WIDTH)
    g = jax.nn.sigmoid(g_cols) @ g2
    return out.astype(r.dtype) * g


def fourier_mix(u):
    B, T, _ = u.shape
    uf = u.astype(jnp.float32).reshape(B, T, FNET_GROUPS, FNET_GROUP)
    return jnp.fft.fft2(uf, axes=(1, 3), norm='ortho').real.reshape(B, T, FNET_WIDTH).astype(u.dtype)


def odd_mixer(hx, hc, w_in, w_out, mu, w0, w2, a0, a2, g2, k_k, k_a, r_k, gn_g, gn_b, ctx_out):
    B = hx.shape[0]
    dir_params = (w0, w2, a0, a2, k_k, k_a)
    px = hx @ w_in
    fx = token_shift(px[..., :RWKV_IN], mu)
    x_in = rwkv_scan_inputs(fx[..., STATE_LO:STATE_HI], *dir_params)
    if ctx_out:
        pc = hc @ w_in
        fc = token_shift(pc[..., :RWKV_IN], mu)
        c_in = rwkv_scan_inputs(fc[..., STATE_LO:STATE_HI], *dir_params)
        rc = fc[..., :RWKV_WIDTH]
    else:
        fcs = token_shift(hc @ w_in[:, STATE_LO:STATE_HI], mu[:, STATE_LO:STATE_HI])
        c_in = rwkv_scan_inputs(fcs, *dir_params)
        rc = None
    rx = fx[..., :RWKV_WIDTH]
    state0 = jnp.zeros((B, RWKV_HEADS, HEAD_DIM, HEAD_DIM), jnp.float32)
    ys_x, ys_c = [], []
    for d, rev in enumerate((False, True)):
        s_c, y_c = wkv_scan(state0, c_in, d, rc, rev)
        _, y_x = wkv_scan(s_c, x_in, d, rx, rev)
        ys_x.append(y_x)
        ys_c.append(y_c)
    o_x = rwkv_readout(ys_x[0] + ys_x[1], rx, x_in, fx[..., STATE_HI:], g2, r_k, gn_g, gn_b)
    yx = jnp.concatenate([o_x, fourier_mix(px[..., RWKV_IN:])], axis=-1) @ w_out
    if not ctx_out:
        return yx, None
    o_c = rwkv_readout(ys_c[0] + ys_c[1], rc, c_in, fc[..., STATE_HI:], g2, r_k, gn_g, gn_b)
    yc = jnp.concatenate([o_c, fourier_mix(pc[..., RWKV_IN:])], axis=-1) @ w_out
    return yx, yc


def clamped_swiglu(gu):
    glu, lin = gu[..., ::2], gu[..., 1::2]
    glu = jnp.minimum(glu, SWIGLU_LIMIT)
    lin = jnp.clip(lin, -SWIGLU_LIMIT, SWIGLU_LIMIT)
    return glu * jax.nn.sigmoid(SWIGLU_ALPHA * glu) * (lin + 1)


def moe_ffn(h, router_w, router_b, w_gu, b_gu, w_dn, b_dn):
    n_tok, D = h.shape
    logits = (h @ router_w).astype(jnp.float32) + router_b.astype(jnp.float32)
    top_val, top_idx = lax.top_k(logits, TOP_K)
    gates = jax.nn.softmax(top_val, axis=-1)
    n_assign = n_tok * TOP_K
    flat_e = top_idx.reshape(-1).astype(jnp.int32)
    flat_tok = jnp.repeat(jnp.arange(n_tok, dtype=jnp.int32), TOP_K)
    flat_gate = gates.reshape(-1)
    order = jnp.argsort(flat_e)
    sorted_e, sorted_tok, sorted_gate = flat_e[order], flat_tok[order], flat_gate[order]
    counts = jax.ops.segment_sum(jnp.ones_like(flat_e), flat_e, num_segments=N_EXPERTS)
    starts = jnp.cumsum(counts) - counts
    padded = (counts + MOE_BLOCK - 1) // MOE_BLOCK * MOE_BLOCK
    pends = jnp.cumsum(padded)
    pstarts = pends - padded
    dest = pstarts[sorted_e] + (jnp.arange(n_assign, dtype=jnp.int32) - starts[sorted_e])
    n_rows = (-(-n_assign // MOE_BLOCK) + N_EXPERTS) * MOE_BLOCK
    n_blocks = n_rows // MOE_BLOCK
    row_tok = jnp.full((n_rows,), n_tok, jnp.int32).at[dest].set(sorted_tok)
    row_gate = jnp.zeros((n_rows,), jnp.float32).at[dest].set(sorted_gate)
    block_e = jnp.minimum(jnp.searchsorted(pends, jnp.arange(n_blocks) * MOE_BLOCK, side='right'),
                          N_EXPERTS - 1)
    h_pad = jnp.concatenate([h, jnp.zeros((1, D), h.dtype)], axis=0)

    def run(args):
        e, toks, gw = args
        xb = h_pad[toks]
        gu = xb @ w_gu[e] + b_gu[e]
        yb = clamped_swiglu(gu) @ w_dn[e] + b_dn[e]
        return yb * gw[:, None].astype(yb.dtype)

    y = lax.map(run, (block_e, row_tok.reshape(n_blocks, MOE_BLOCK), row_gate.reshape(n_blocks, MOE_BLOCK)))
    return jax.ops.segment_sum(y.reshape(n_rows, D), row_tok, num_segments=n_tok + 1)[:n_tok]


def setup_inputs(seed: int = 0) -> dict:
    key = jax.random.key(seed)
    keys = iter(jax.random.split(key, 40))
    D = D_MODEL
    n_even = (DEPTH + 1) // 2
    n_odd = DEPTH // 2

    def nrm(shape, std):
        return std * jax.random.normal(next(keys), shape, jnp.float32)

    def uni(shape, lo, hi):
        return jax.random.uniform(next(keys), shape, jnp.float32, lo, hi)

    return {
        'x': nrm((BATCH, SEQ, D), 1.0),
        'c': nrm((BATCH, D), 1.0),
        'ctx': nrm((BATCH, CTX_LEN, D), 1.0),
        'c_ctx': nrm((D,), 1.0),
        'ada_w': nrm((DEPTH, D, 6 * D), 0.5 * D ** -0.5),
        'ada_b': nrm((DEPTH, 6 * D), 0.02),
        'norm_mix_g': 1.0 + nrm((DEPTH, D), 0.02),
        'norm_ffn_g': 1.0 + nrm((DEPTH, D), 0.02),
        'ev_w_in': nrm((n_even, D, EVEN_IN), D ** -0.5),
        'ev_w_out': nrm((n_even, EVEN_MIX, D), EVEN_MIX ** -0.5),
        'pool_w': nrm((n_even, len(POOL_WINDOWS), POOL_GROUP, POOL_GROUP), POOL_GROUP ** -0.5),
        'pool_scale': 1.0 + nrm((n_even, POOL_WIDTH), 0.1),
        'att_sink': nrm((n_even, ATT_HEADS), 0.5),
        'od_w_in': nrm((n_odd, D, ODD_IN), D ** -0.5),
        'od_w_out': nrm((n_odd, ODD_MIX, D), ODD_MIX ** -0.5),
        'rw_mu': uni((n_odd, 2, RWKV_IN), 0.0, 0.5),
        'rw_w0': uni((n_odd, 2, RWKV_WIDTH), -6.0, -1.0),
        'rw_w2': nrm((n_odd, 2, DECAY_LORA, RWKV_WIDTH), 0.1),
        'rw_a0': nrm((n_odd, 2, RWKV_WIDTH), 0.5),
        'rw_a2': nrm((n_odd, 2, ICLR_LORA, RWKV_WIDTH), 0.1),
        'rw_g2': nrm((n_odd, GATE_LORA, RWKV_WIDTH), GATE_LORA ** -0.5),
        'rw_k_k': 0.85 + nrm((n_odd, RWKV_WIDTH), 0.05),
        'rw_k_a': 1.0 + nrm((n_odd, RWKV_WIDTH), 0.05),
        'rw_r_k': nrm((n_odd, RWKV_HEADS, HEAD_DIM), 0.1),
        'rw_gn_g': 1.0 + nrm((n_odd, RWKV_WIDTH), 0.02),
        'rw_gn_b': nrm((n_odd, RWKV_WIDTH), 0.01),
        'router_w': nrm((DEPTH, D, N_EXPERTS), D ** -0.5),
        'router_b': nrm((DEPTH, N_EXPERTS), 0.01),
        'exp_w_gu': nrm((DEPTH, N_EXPERTS, D, 2 * D_EXPERT), D ** -0.5),
        'exp_b_gu': nrm((DEPTH, N_EXPERTS, 2 * D_EXPERT), 0.01),
        'exp_w_dn': nrm((DEPTH, N_EXPERTS, D_EXPERT, D), D_EXPERT ** -0.5),
        'exp_b_dn': nrm((DEPTH, N_EXPERTS, D), 0.01),
        'final_g': 1.0 + nrm((D,), 0.02),
    }


def reference(x, c, ctx, c_ctx, ada_w, ada_b, norm_mix_g, norm_ffn_g, ev_w_in, ev_w_out, pool_w,
              pool_scale, att_sink, od_w_in, od_w_out, rw_mu, rw_w0, rw_w2, rw_a0, rw_a2, rw_g2,
              rw_k_k, rw_k_a, rw_r_k, rw_gn_g, rw_gn_b, router_w, router_b, exp_w_gu, exp_b_gu,
              exp_w_dn, exp_b_dn, final_g):
    B, S, D = x.shape
    C = ctx.shape[1]
    cos, sin = axial_rope_tables(S)
    h, hc = x, ctx
    for i in range(DEPTH):
        last = i == DEPTH - 1
        j = i // 2
        mod_x = jax.nn.silu(c) @ ada_w[i] + ada_b[i]
        mod_c = (jax.nn.silu(c_ctx) @ ada_w[i] + ada_b[i])[None]
        sh_m, sc_m, g_m, sh_f, sc_f, g_f = jnp.split(mod_x, 6, axis=-1)
        csh_m, csc_m, cg_m, csh_f, csc_f, cg_f = jnp.split(mod_c, 6, axis=-1)
        ax = modulate(rms_norm(h, norm_mix_g[i]), sh_m, sc_m)
        ac = modulate(rms_norm(hc, norm_mix_g[i]), csh_m, csc_m)
        if i % 2 == 0:
            yx, yc = even_mixer(ax, ac, ev_w_in[j], ev_w_out[j], pool_w[j], pool_scale[j], att_sink[j],
                                cos, sin, not last)
        else:
            yx, yc = odd_mixer(ax, ac, od_w_in[j], od_w_out[j], rw_mu[j], rw_w0[j], rw_w2[j], rw_a0[j],
                               rw_a2[j], rw_g2[j], rw_k_k[j], rw_k_a[j], rw_r_k[j], rw_gn_g[j],
                               rw_gn_b[j], not last)
        h = h + g_m[:, None, :] * yx
        fx = modulate(rms_norm(h, norm_ffn_g[i]), sh_f, sc_f).reshape(B * S, D)
        if last:
            out = moe_ffn(fx, router_w[i], router_b[i], exp_w_gu[i], exp_b_gu[i], exp_w_dn[i], exp_b_dn[i])
            h = h + g_f[:, None, :] * out.reshape(B, S, D)
        else:
            hc = hc + cg_m[:, None, :] * yc
            fc = modulate(rms_norm(hc, norm_ffn_g[i]), csh_f, csc_f).reshape(B * C, D)
            out = moe_ffn(jnp.concatenate([fx, fc], axis=0), router_w[i], router_b[i], exp_w_gu[i],
                          exp_b_gu[i], exp_w_dn[i], exp_b_dn[i])
            h = h + g_f[:, None, :] * out[:B * S].reshape(B, S, D)
            hc = hc + cg_f[:, None, :] * out[B * S:].reshape(B, C, D)
    return rms_norm(h, final_g)
```

```python
import functools
import math

import jax
import jax.numpy as jnp
from jax import lax
from jax.experimental import pallas as pl
from jax.experimental.pallas import tpu as pltpu

F32 = jnp.float32
BF16 = jnp.bfloat16

GRID_W = 64
HEAD_DIM = 64
ROT_FREQS = HEAD_DIM // 4
ROPE_THETA = 10000.0
NORM_EPS = 1e-5
POOL_WINDOWS = (2, 4, 8, 16)
POOL_GROUP = 64
POOL_WIDTH = POOL_GROUP * len(POOL_WINDOWS)
POOL_HALO = max(POOL_WINDOWS) // 2
ATT_HEADS = 12
ATT_KV_HEADS = 3
ATT_GROUP = ATT_HEADS // ATT_KV_HEADS
ATT_WINDOW = 128
ATT_BLOCK = 128
ATT_Q = ATT_HEADS * HEAD_DIM
ATT_KV = ATT_KV_HEADS * HEAD_DIM
EVEN_IN = POOL_WIDTH + ATT_Q + 2 * ATT_KV
RWKV_HEADS = 12
RWKV_WIDTH = RWKV_HEADS * HEAD_DIM
DECAY_LORA = 64
ICLR_LORA = 64
GATE_LORA = 160
GN_EPS = 64e-5
LORA_LO = 3 * RWKV_WIDTH
STATE_HI = LORA_LO + 2 * DECAY_LORA + 2 * ICLR_LORA
RWKV_IN = STATE_HI + GATE_LORA
FNET_GROUPS = 4
FNET_GROUP = 64
FNET_WIDTH = FNET_GROUPS * FNET_GROUP
ODD_IN = RWKV_IN + FNET_WIDTH
N_EXPERTS = 32
TOP_K = 4
SWIGLU_LIMIT = 7.0
SWIGLU_ALPHA = 1.702

LANES = 128
SUBLANES = 8
SCAN_CHUNK = 64
MOE_ROWS = 512
MASK_NEG = -1e30


def _cparams(n_axes, vmem_mb):
    return pltpu.CompilerParams(
        dimension_semantics=("arbitrary",) * n_axes,
        vmem_limit_bytes=vmem_mb * 1024 * 1024,
    )


def _dot(a, b):
    return jnp.dot(a, b, preferred_element_type=F32)


def _dot_nt(a, b):
    return lax.dot_general(a, b, (((1,), (1,)), ((), ())), preferred_element_type=F32)


def _dot_tn(a, b):
    return lax.dot_general(a, b, (((0,), (0,)), ((), ())), preferred_element_type=F32)


def _split(x):
    hi = x.astype(BF16)
    lo = (x - hi.astype(F32)).astype(BF16)
    return hi, lo


def _dot3(a, b):
    ah, al = _split(a)
    bh, bl = _split(b)
    return _dot(ah, bh) + (_dot(al, bh) + _dot(ah, bl))


def _dot_exact_lhs(a_exact, b):
    a16 = a_exact.astype(BF16)
    b1 = b.astype(BF16)
    r1 = b - b1.astype(F32)
    b2 = r1.astype(BF16)
    b3 = (r1 - b2.astype(F32)).astype(BF16)
    return _dot(a16, b1) + (_dot(a16, b2) + _dot(a16, b3))


def _modnorm(x, g, sh, sc):
    ms = jnp.mean(x * x, axis=-1, keepdims=True)
    xn = x * lax.rsqrt(ms + NORM_EPS) * g
    return xn * (1.0 + sc) + sh


def _sigmoid(x):
    return 1.0 / (1.0 + jnp.exp(-x))


def _head_sum(x):
    n = x.shape[1] // LANES
    lane = lax.broadcasted_iota(jnp.int32, (x.shape[0], LANES), 1)
    lo_mask = lane < HEAD_DIM
    parts = []
    for c in range(n):
        xc = x[:, c * LANES:(c + 1) * LANES]
        s_lo = jnp.sum(jnp.where(lo_mask, xc, 0.0), axis=-1, keepdims=True)
        s_hi = jnp.sum(jnp.where(lo_mask, 0.0, xc), axis=-1, keepdims=True)
        parts.append(jnp.where(lo_mask, s_lo, s_hi))
    return jnp.concatenate(parts, axis=1)


def _ada_kernel(c_ref, w_ref, b_ref, o_ref):
    x = c_ref[...]
    x = x * _sigmoid(x)
    o_ref[...] = _dot3(x, w_ref[...]) + b_ref[...]


def _ada_all(cvec, ada_w, ada_b):
    depth, d, n6 = ada_w.shape
    r = cvec.shape[0]
    tn = 1536 if n6 % 1536 == 0 else n6
    return pl.pallas_call(
        _ada_kernel,
        out_shape=jax.ShapeDtypeStruct((depth, r, n6), F32),
        grid=(depth, n6 // tn),
        in_specs=[
            pl.BlockSpec((r, d), lambda i, j: (0, 0)),
            pl.BlockSpec((None, d, tn), lambda i, j: (i, 0, j)),
            pl.BlockSpec((None, 1, tn), lambda i, j: (i, 0, j)),
        ],
        out_specs=pl.BlockSpec((None, r, tn), lambda i, j: (i, 0, j)),
        compiler_params=_cparams(2, 48),
        name="ada_mod",
    )(cvec, ada_w, ada_b.reshape(depth, 1, n6))


def _bmap(arr):
    if arr.shape[0] == 1:
        return lambda b: 0
    return lambda b: b


def _even_in_kernel(h_ref, sh_ref, sc_ref, g_ref, w_ref, cos_ref, sin_ref,
                    pool_ref, q_ref, kv_ref, *, rope):
    a = _modnorm(h_ref[...], g_ref[...], sh_ref[...], sc_ref[...]).astype(BF16)
    px = _dot(a, w_ref[...])
    pool_ref[...] = px[:, :POOL_WIDTH]
    n_chunks = (ATT_Q + 2 * ATT_KV) // LANES
    n_full = (ATT_Q + ATT_KV) // LANES
    outs = []
    if rope:
        lane = lax.broadcasted_iota(jnp.int32, (px.shape[0], LANES), 1)
        first = (lane % (2 * ROT_FREQS)) < ROT_FREQS
    for c in range(n_chunks):
        x = px[:, POOL_WIDTH + c * LANES:POOL_WIDTH + (c + 1) * LANES]
        if rope and c <= n_full:
            t0 = 0 if c < n_full else LANES
            cs = cos_ref[:, t0:t0 + LANES]
            sn = sin_ref[:, t0:t0 + LANES]
            rot = jnp.where(first, pltpu.roll(x, LANES - ROT_FREQS, 1), pltpu.roll(x, ROT_FREQS, 1))
            x = x * cs + rot * sn
        if c < ATT_Q // LANES:
            x = x * (HEAD_DIM ** -0.5)
        outs.append(x.astype(BF16))
    nq = ATT_Q // LANES
    q_ref[...] = jnp.concatenate(outs[:nq], axis=1)
    kv_ref[...] = jnp.concatenate(outs[nq:], axis=1)


def _even_in(h, sh, sc, g, w_bf, cos_t, sin_t, *, rope, tm):
    B, T, D = h.shape
    nt = T // tm
    bm = _bmap(sh)
    kern = functools.partial(_even_in_kernel, rope=rope)
    return pl.pallas_call(
        kern,
        out_shape=(
            jax.ShapeDtypeStruct((B, T, POOL_WIDTH), F32),
            jax.ShapeDtypeStruct((B, T, ATT_Q), BF16),
            jax.ShapeDtypeStruct((B, T, 2 * ATT_KV), BF16),
        ),
        grid=(nt, B),
        in_specs=[
            pl.BlockSpec((None, tm, D), lambda s, b: (b, s, 0)),
            pl.BlockSpec((None, 1, D), lambda s, b: (bm(b), 0, 0)),
            pl.BlockSpec((None, 1, D), lambda s, b: (bm(b), 0, 0)),
            pl.BlockSpec((1, D), lambda s, b: (0, 0)),
            pl.BlockSpec((D, EVEN_IN), lambda s, b: (0, 0)),
            pl.BlockSpec((tm, 2 * LANES), lambda s, b: (s, 0)),
            pl.BlockSpec((tm, 2 * LANES), lambda s, b: (s, 0)),
        ],
        out_specs=(
            pl.BlockSpec((None, tm, POOL_WIDTH), lambda s, b: (b, s, 0)),
            pl.BlockSpec((None, tm, ATT_Q), lambda s, b: (b, s, 0)),
            pl.BlockSpec((None, tm, 2 * ATT_KV), lambda s, b: (b, s, 0)),
        ),
        compiler_params=_cparams(2, 48),
        name="even_in",
    )(h, sh, sc, g, w_bf, cos_t, sin_t)


def _rope_tables(T):
    t = jnp.arange(T, dtype=jnp.int32)
    row = (t // GRID_W).astype(F32)
    col = (t % GRID_W).astype(F32)
    inv_freq = ROPE_THETA ** (-jnp.arange(ROT_FREQS, dtype=F32) / ROT_FREQS)
    ang_r = row[:, None] * inv_freq
    ang_c = col[:, None] * inv_freq
    cos_h = jnp.concatenate([jnp.cos(ang_r)] * 2 + [jnp.cos(ang_c)] * 2, axis=1)
    sin_h = jnp.concatenate([-jnp.sin(ang_r), jnp.sin(ang_r), -jnp.sin(ang_c), jnp.sin(ang_c)], axis=1)
    one = jnp.ones_like(cos_h)
    zero = jnp.zeros_like(sin_h)
    cos_t = jnp.concatenate([cos_h, cos_h, cos_h, one], axis=1)
    sin_t = jnp.concatenate([sin_h, sin_h, sin_h, zero], axis=1)
    return cos_t, sin_t


def _residual_and_router(h, y, gm, gf_norm, shf, scf, rw, rb, hn_ref, fx_ref, lg_ref):
    hn = h + gm * y
    hn_ref[...] = hn
    fx = _modnorm(hn, gf_norm, shf, scf)
    fx_ref[...] = fx.astype(BF16)
    lg_ref[...] = _dot3(fx, rw) + rb


def _even_mix_kernel(*refs, local, tq, T, n_ctx):
    it = iter(refs)
    sink_ref = next(it)
    q_ref = next(it)
    if local:
        kvp_ref, kvc_ref, kvn_ref = next(it), next(it), next(it)
    ckv_ref = next(it)
    up_ref, uc_ref, un_ref = next(it), next(it), next(it)
    h_ref, gm_ref, shf_ref, scf_ref, gfn_ref = next(it), next(it), next(it), next(it), next(it)
    wout_ref, pw_ref, ps_ref, rw_ref, rb_ref = next(it), next(it), next(it), next(it), next(it)
    hn_ref, fx_ref, lg_ref = next(it), next(it), next(it)

    j = pl.program_id(1)
    nb = pl.num_programs(1)

    n_loc = 3 * ATT_BLOCK if local else 0
    n_keys = n_loc + n_ctx
    rows = ATT_GROUP * tq
    if local:
        r_i = lax.broadcasted_iota(jnp.int32, (rows, n_loc), 0)
        c_i = lax.broadcasted_iota(jnp.int32, (rows, n_loc), 1)
        q_pos = j * tq + r_i % tq
        k_pos = (j - 1) * ATT_BLOCK + c_i
        valid = (jnp.abs(q_pos - k_pos) <= ATT_WINDOW) & (k_pos >= 0) & (k_pos < T)
    row_head = lax.broadcasted_iota(jnp.int32, (rows, 1), 0) // tq
    q = q_ref[...]
    ckv = ckv_ref[...]
    if local:
        kvl = jnp.concatenate([kvp_ref[...], kvc_ref[...], kvn_ref[...]], axis=0)
    att = []
    for g in range(ATT_KV_HEADS):
        qs = jnp.concatenate(
            [q[:, (g * ATT_GROUP + i) * HEAD_DIM:(g * ATT_GROUP + i + 1) * HEAD_DIM] for i in range(ATT_GROUP)],
            axis=0)
        kc = ckv[:, g * HEAD_DIM:(g + 1) * HEAD_DIM]
        vc = ckv[:, ATT_KV + g * HEAD_DIM:ATT_KV + (g + 1) * HEAD_DIM]
        s_ctx = _dot_nt(qs, kc)
        sink = jnp.zeros((rows, 1), F32)
        for i in range(ATT_GROUP):
            sink = jnp.where(row_head == i, sink_ref[g * ATT_GROUP + i], sink)
        if local:
            kl = kvl[:, g * HEAD_DIM:(g + 1) * HEAD_DIM]
            vl = kvl[:, ATT_KV + g * HEAD_DIM:ATT_KV + (g + 1) * HEAD_DIM]
            s_loc = jnp.where(valid, _dot_nt(qs, kl), MASK_NEG)
            m = jnp.maximum(jnp.maximum(jnp.max(s_loc, axis=-1, keepdims=True),
                                        jnp.max(s_ctx, axis=-1, keepdims=True)), sink)
            p_loc = jnp.exp(s_loc - m)
            p_ctx = jnp.exp(s_ctx - m)
            den = (jnp.sum(p_loc, axis=-1, keepdims=True) + jnp.sum(p_ctx, axis=-1, keepdims=True)
                   + jnp.exp(sink - m))
            o = _dot(p_loc.astype(BF16), vl) + _dot(p_ctx.astype(BF16), vc)
        else:
            m = jnp.maximum(jnp.max(s_ctx, axis=-1, keepdims=True), sink)
            p_ctx = jnp.exp(s_ctx - m)
            den = jnp.sum(p_ctx, axis=-1, keepdims=True) + jnp.exp(sink - m)
            o = _dot(p_ctx.astype(BF16), vc)
        o = o / den
        for i in range(ATT_GROUP):
            att.append(o[i * tq:(i + 1) * tq])
    att_x = jnp.concatenate(att, axis=1)

    u = uc_ref[...]
    up = jnp.where(j > 0, up_ref[...], 0.0)
    un = jnp.where(j < nb - 1, un_ref[...], 0.0)
    e = jnp.concatenate([up, u, un], axis=0)
    n_e = tq + 2 * POOL_HALO

    def shifted(x, k):
        return pltpu.roll(x, k % n_e, 0)

    a1 = e + shifted(e, 1)
    a2 = shifted(a1, 1) + shifted(a1, -1)
    a3 = shifted(a2, 2) + shifted(a2, -2)
    a4 = shifted(a3, 4) + shifted(a3, -4)
    lane = lax.broadcasted_iota(jnp.int32, (n_e, POOL_WIDTH), 1)
    grp = lane // POOL_GROUP
    win = jnp.where(grp == 0, a1, jnp.where(grp == 1, a2, jnp.where(grp == 2, a3, a4)))
    win = win[POOL_HALO:POOL_HALO + tq]
    t_i = j * tq + lax.broadcasted_iota(jnp.int32, (tq, POOL_WIDTH), 0)
    half = jnp.left_shift(1, lax.broadcasted_iota(jnp.int32, (tq, POOL_WIDTH), 1) // POOL_GROUP)
    cnt = (jnp.minimum(t_i + half, T) - jnp.maximum(t_i - half, 0)).astype(F32)
    pooled = win / cnt - u
    pool_x = _dot(pooled.astype(BF16), pw_ref[...]) * ps_ref[...]

    mix = jnp.concatenate([pool_x, att_x], axis=1).astype(BF16)
    y = _dot(mix, wout_ref[...])
    _residual_and_router(h_ref[...], y, gm_ref[...], gfn_ref[...], shf_ref[...], scf_ref[...],
                         rw_ref[...], rb_ref[...], hn_ref, fx_ref, lg_ref)


def _even_mix(sink, q, kv, ckv, pool_u, h, gm, shf, scf, gfn, wout, pw_bd, pscale, rw, rb, *, local):
    B, T, D = h.shape
    n_ctx = ckv.shape[1]
    tq = ATT_BLOCK if local else T
    nb = T // tq
    r8 = tq // SUBLANES
    n8 = T // SUBLANES
    bm = _bmap(gm)
    kern = functools.partial(_even_mix_kernel, local=local, tq=tq, T=T, n_ctx=n_ctx)
    vec = lambda: pl.BlockSpec((None, 1, D), lambda b, j: (bm(b), 0, 0))
    full = lambda a: pl.BlockSpec(a.shape, lambda b, j: (0,) * a.ndim)
    in_specs = [pl.BlockSpec(memory_space=pltpu.SMEM),
                pl.BlockSpec((None, tq, ATT_Q), lambda b, j: (b, j, 0))]
    args = [sink, q]
    if local:
        in_specs += [
            pl.BlockSpec((None, ATT_BLOCK, 2 * ATT_KV), lambda b, j: (b, jnp.maximum(j - 1, 0), 0)),
            pl.BlockSpec((None, ATT_BLOCK, 2 * ATT_KV), lambda b, j: (b, j, 0)),
            pl.BlockSpec((None, ATT_BLOCK, 2 * ATT_KV), lambda b, j: (b, jnp.minimum(j + 1, nb - 1), 0)),
        ]
        args += [kv, kv, kv]
    in_specs += [
        pl.BlockSpec((None, n_ctx, 2 * ATT_KV), lambda b, j: (b, 0, 0)),
        pl.BlockSpec((None, SUBLANES, POOL_WIDTH), lambda b, j: (b, jnp.maximum(j * r8 - 1, 0), 0)),
        pl.BlockSpec((None, tq, POOL_WIDTH), lambda b, j: (b, j, 0)),
        pl.BlockSpec((None, SUBLANES, POOL_WIDTH), lambda b, j: (b, jnp.minimum((j + 1) * r8, n8 - 1), 0)),
        pl.BlockSpec((None, tq, D), lambda b, j: (b, j, 0)),
        vec(), vec(), vec(), full(gfn), full(wout), full(pw_bd), full(pscale), full(rw), full(rb),
    ]
    args += [ckv, pool_u, pool_u, pool_u, h, gm, shf, scf, gfn, wout, pw_bd, pscale, rw, rb]
    return pl.pallas_call(
        kern,
        out_shape=(
            jax.ShapeDtypeStruct((B, T, D), F32),
            jax.ShapeDtypeStruct((B, T, D), BF16),
            jax.ShapeDtypeStruct((B, T, N_EXPERTS), F32),
        ),
        grid=(B, nb),
        in_specs=in_specs,
        out_specs=(
            pl.BlockSpec((None, tq, D), lambda b, j: (b, j, 0)),
            pl.BlockSpec((None, tq, D), lambda b, j: (b, j, 0)),
            pl.BlockSpec((None, tq, N_EXPERTS), lambda b, j: (b, j, 0)),
        ),
        compiler_params=_cparams(2, 48),
        name="even_mix_local" if local else "even_mix_ctx",
    )(*args)


def _odd_in_kernel(h_ref, hp_ref, hn_ref, sh_ref, sc_ref, g_ref, w_ref, mu_ref, w0_ref, w2_ref,
                   a0_ref, a2_ref, kk_ref, ka_ref, rk_ref,
                   r_out, v_out, kkn_out, bonus_out, lw_out, kd_out, bd_out, gc_out, fn_out, *, tm):
    s = pl.program_id(1)
    ns = pl.num_programs(1)
    g, sh, sc = g_ref[...], sh_ref[...], sc_ref[...]
    w = w_ref[...]
    a = _modnorm(h_ref[...], g, sh, sc).astype(BF16)
    px = _dot(a, w)
    fn_out[...] = px[:, RWKV_IN:]
    main = px[:, :RWKV_IN]
    ap = _modnorm(hp_ref[...], g, sh, sc).astype(BF16)
    an = _modnorm(hn_ref[...], g, sh, sc).astype(BF16)
    halo = _dot(jnp.concatenate([ap, an], axis=0), w[:, :RWKV_IN])
    prev_row = jnp.where(s > 0, halo[SUBLANES - 1:SUBLANES], 0.0)
    next_row = jnp.where(s < ns - 1, halo[SUBLANES:SUBLANES + 1], 0.0)
    row = lax.broadcasted_iota(jnp.int32, (tm, 1), 0)
    prev = jnp.where(row == 0, prev_row, pltpu.roll(main, 1, 0))
    nxt = jnp.where(row == tm - 1, next_row, pltpu.roll(main, tm - 1, 0))
    mu = mu_ref[...]
    fs = main + mu[0:1] * (prev - main) + mu[1:2] * (nxt - main)

    W = RWKV_WIDTH
    r = fs[:, :W]
    k = fs[:, W:2 * W]
    v = fs[:, 2 * W:3 * W]
    lora = fs[:, LORA_LO:STATE_HI]
    gc_out[...] = fs[:, STATE_HI:RWKV_IN]
    r_out[...] = r
    v_out[...] = v

    kx = k * kk_ref[...]
    nrm = jnp.sqrt(_head_sum(kx * kx))
    kkn = kx / jnp.maximum(nrm, 1e-12)
    kkn_out[...] = kkn
    ka = ka_ref[...]
    ksum = None
    for d in range(2):
        wd = lora[:, d * DECAY_LORA:(d + 1) * DECAY_LORA]
        o_a = 2 * DECAY_LORA
        ad = lora[:, o_a + d * ICLR_LORA:o_a + (d + 1) * ICLR_LORA]
        xw = _dot(jnp.tanh(wd).astype(BF16), w2_ref[d]) + w0_ref[d:d + 1]
        z = -xw
        softplus = jnp.maximum(z, 0.0) + jnp.log(1.0 + jnp.exp(-jnp.abs(z)))
        w_log = -softplus - 0.5
        lw_out[d] = -jnp.exp(w_log)
        xa = _dot(ad.astype(BF16), a2_ref[d]) + a0_ref[d:d + 1]
        a_d = _sigmoid(xa)
        k_d = k * (1.0 + (a_d - 1.0) * ka)
        kd_out[d] = k_d
        bd_out[d] = kkn * a_d
        ksum = k_d if ksum is None else ksum + k_d
    coef = _head_sum(r * ksum * rk_ref[...])
    bonus_out[...] = coef * v


def _odd_in(h, sh, sc, g, w_bf, mu, w0, w2_bf, a0, a2_bf, k_k, k_a, r_k, *, tm):
    B, T, D = h.shape
    ns = T // tm
    r8 = tm // SUBLANES
    n8 = T // SUBLANES
    bm = _bmap(sh)
    W = RWKV_WIDTH
    full = lambda a: pl.BlockSpec(a.shape, lambda b, s: (0,) * a.ndim)
    tok = lambda n: pl.BlockSpec((None, tm, n), lambda b, s: (b, s, 0))
    tok2 = lambda n: pl.BlockSpec((2, None, tm, n), lambda b, s: (0, b, s, 0))
    kern = functools.partial(_odd_in_kernel, tm=tm)
    return pl.pallas_call(
        kern,
        out_shape=(
            jax.ShapeDtypeStruct((B, T, W), F32),
            jax.ShapeDtypeStruct((B, T, W), F32),
            jax.ShapeDtypeStruct((B, T, W), F32),
            jax.ShapeDtypeStruct((B, T, W), F32),
            jax.ShapeDtypeStruct((2, B, T, W), F32),
            jax.ShapeDtypeStruct((2, B, T, W), F32),
            jax.ShapeDtypeStruct((2, B, T, W), F32),
            jax.ShapeDtypeStruct((B, T, GATE_LORA), F32),
            jax.ShapeDtypeStruct((B, T, FNET_WIDTH), F32),
        ),
        grid=(B, ns),
        in_specs=[
            pl.BlockSpec((None, tm, D), lambda b, s: (b, s, 0)),
            pl.BlockSpec((None, SUBLANES, D), lambda b, s: (b, jnp.maximum(s * r8 - 1, 0), 0)),
            pl.BlockSpec((None, SUBLANES, D), lambda b, s: (b, jnp.minimum((s + 1) * r8, n8 - 1), 0)),
            pl.BlockSpec((None, 1, D), lambda b, s: (bm(b), 0, 0)),
            pl.BlockSpec((None, 1, D), lambda b, s: (bm(b), 0, 0)),
            full(g), full(w_bf), full(mu), full(w0), full(w2_bf), full(a0), full(a2_bf),
            full(k_k), full(k_a), full(r_k),
        ],
        out_specs=(tok(W), tok(W), tok(W), tok(W), tok2(W), tok2(W), tok2(W), tok(GATE_LORA), tok(FNET_WIDTH)),
        compiler_params=_cparams(2, 60),
        name="odd_in",
    )(h, h, h, sh, sc, g, w_bf, mu, w0, w2_bf, a0, a2_bf, k_k, k_a, r_k)


def _tri_inverse(m, n):
    r_i = lax.broadcasted_iota(jnp.int32, (n, n), 0)
    c_i = lax.broadcasted_iota(jnp.int32, (n, n), 1)
    eye = (r_i == c_i).astype(F32)
    x = eye - m
    p = m
    for _ in range(int(math.log2(n)) - 1):
        p = _dot3(p, p)
        x = x + _dot3(x, p)
    return x


def _scan_kernel(r_ref, v_ref, kk_ref, lw_ref, kd_ref, bd_ref, s0_ref, y_ref, sT_ref, s_scr, *, L):
    d = pl.program_id(0)
    c = pl.program_id(2)
    nc = pl.num_programs(2)

    @pl.when(c == 0)
    def _():
        s_scr[...] = s0_ref[...]

    t_i = lax.broadcasted_iota(jnp.int32, (L, L), 0)
    s_i = lax.broadcasted_iota(jnp.int32, (L, L), 1)
    order = (t_i - s_i) * (1 - 2 * d)
    before = order > 0
    upto = order >= 0

    lw = lw_ref[...]
    c_in = _dot_exact_lhs(upto.astype(F32), lw)
    c_ex = c_in - lw
    tot = jnp.sum(lw, axis=0, keepdims=True)
    r = r_ref[...]
    v = v_ref[...]
    kk = kk_ref[...]
    kd = kd_ref[...]
    bd = bd_ref[...]
    inv = jnp.exp(-c_in)
    kt = kk * jnp.exp(c_ex)
    rt = r * jnp.exp(c_in)
    khat = kd * inv
    bhat = bd * inv
    end = jnp.exp(tot - c_in)
    kend = kd * end
    bend = bd * end
    g_tot = jnp.exp(tot)

    ys = []
    for hd in range(RWKV_HEADS):
        sl = slice(hd * HEAD_DIM, (hd + 1) * HEAD_DIM)
        lhs = jnp.concatenate([kt[:, sl], rt[:, sl]], axis=0).astype(BF16)
        rhs = jnp.concatenate([khat[:, sl], bhat[:, sl]], axis=0).astype(BF16)
        s4 = _dot_nt(lhs, rhs)
        a_kk = jnp.where(before, s4[:L, :L], 0.0)
        a_kb = jnp.where(before, s4[:L, L:], 0.0)
        a_rk = jnp.where(upto, s4[L:, :L], 0.0)
        a_rb = jnp.where(upto, s4[L:, L:], 0.0)
        s_h = s_scr[hd]
        ks = _dot_nt(lhs, s_h.astype(BF16))
        v_h = v[:, sl].astype(BF16)
        rhs_u = ks[:L] + _dot(a_kk.astype(BF16), v_h)
        u = _dot3(_tri_inverse(a_kb, L), rhs_u)
        u16 = u.astype(BF16)
        ys.append(ks[L:] + _dot(a_rk.astype(BF16), v_h) - _dot(a_rb.astype(BF16), u16))
        vu = jnp.concatenate([v_h, -u16], axis=0)
        ke = jnp.concatenate([kend[:, sl], bend[:, sl]], axis=0).astype(BF16)
        s_scr[hd] = s_h * g_tot[:, sl] + _dot_tn(vu, ke)
    y_ref[...] = jnp.concatenate(ys, axis=1)

    @pl.when(c == nc - 1)
    def _():
        sT_ref[...] = s_scr[...]


def _scan(r, v, kk, lw, kd, bd, s0):
    B, T, W = r.shape
    L = SCAN_CHUNK
    nc = T // L
    ci = lambda d, c: jnp.where(d == 0, c, nc - 1 - c)
    shared = pl.BlockSpec((None, L, W), lambda d, b, c: (b, ci(d, c), 0))
    per_dir = pl.BlockSpec((None, None, L, W), lambda d, b, c: (d, b, ci(d, c), 0))
    state = pl.BlockSpec((None, None, RWKV_HEADS, HEAD_DIM, HEAD_DIM), lambda d, b, c: (d, b, 0, 0, 0))
    kern = functools.partial(_scan_kernel, L=L)
    return pl.pallas_call(
        kern,
        out_shape=(
            jax.ShapeDtypeStruct((2, B, T, W), F32),
            jax.ShapeDtypeStruct((2, B, RWKV_HEADS, HEAD_DIM, HEAD_DIM), F32),
        ),
        grid=(2, B, nc),
        in_specs=[shared, shared, shared, per_dir, per_dir, per_dir, state],
        out_specs=(per_dir, state),
        scratch_shapes=[pltpu.VMEM((RWKV_HEADS, HEAD_DIM, HEAD_DIM), F32)],
        compiler_params=_cparams(3, 48),
        name="rwkv_scan",
    )(r, v, kk, lw, kd, bd, s0)


def _odd_out_kernel(y_ref, bonus_ref, gc_ref, u_ref, dft_ref, cs_ref, h_ref, gm_ref, shf_ref, scf_ref,
                    gfn_ref, gng_ref, gnb_ref, g2_ref, wout_ref, rw_ref, rb_ref,
                    hn_ref, fx_ref, lg_ref, ucs_scr, *, T):
    s = pl.program_id(1)

    @pl.when(s == 0)
    def _():
        t = _dot(u_ref[...].astype(BF16), cs_ref[...])
        ucs_scr[0:T, :] = t[:, :FNET_WIDTH].astype(BF16)
        ucs_scr[T:2 * T, :] = t[:, FNET_WIDTH:].astype(BF16)

    f = _dot(dft_ref[...], ucs_scr[...])
    y = y_ref[0] + y_ref[1]
    mean = _head_sum(y) * (1.0 / HEAD_DIM)
    yc = y - mean
    var = _head_sum(yc * yc) * (1.0 / HEAD_DIM)
    yn = yc * lax.rsqrt(var + GN_EPS) * gng_ref[...] + gnb_ref[...]
    out = yn + bonus_ref[...]
    gate = _dot(_sigmoid(gc_ref[...]).astype(BF16), g2_ref[...])
    o = out * gate
    mix = jnp.concatenate([o, f], axis=1).astype(BF16)
    ymix = _dot(mix, wout_ref[...])
    _residual_and_router(h_ref[...], ymix, gm_ref[...], gfn_ref[...], shf_ref[...], scf_ref[...],
                         rw_ref[...], rb_ref[...], hn_ref, fx_ref, lg_ref)


def _odd_out(y, bonus, gc, u, dft, cs64, h, gm, shf, scf, gfn, gn_g, gn_b, g2_bf, wout, rw, rb, *, tm):
    B, T, D = h.shape
    ns = T // tm
    W = RWKV_WIDTH
    bm = _bmap(gm)
    vec = lambda: pl.BlockSpec((None, 1, D), lambda b, s: (bm(b), 0, 0))
    full = lambda a: pl.BlockSpec(a.shape, lambda b, s: (0,) * a.ndim)
    kern = functools.partial(_odd_out_kernel, T=T)
    return pl.pallas_call(
        kern,
        out_shape=(
            jax.ShapeDtypeStruct((B, T, D), F32),
            jax.ShapeDtypeStruct((B, T, D), BF16),
            jax.ShapeDtypeStruct((B, T, N_EXPERTS), F32),
        ),
        grid=(B, ns),
        in_specs=[
            pl.BlockSpec((2, None, tm, W), lambda b, s: (0, b, s, 0)),
            pl.BlockSpec((None, tm, W), lambda b, s: (b, s, 0)),
            pl.BlockSpec((None, tm, GATE_LORA), lambda b, s: (b, s, 0)),
            pl.BlockSpec((None, T, FNET_WIDTH), lambda b, s: (b, 0, 0)),
            pl.BlockSpec((tm, 2 * T), lambda b, s: (s, 0)),
            full(cs64),
            pl.BlockSpec((None, tm, D), lambda b, s: (b, s, 0)),
            vec(), vec(), vec(), full(gfn), full(gn_g), full(gn_b), full(g2_bf), full(wout), full(rw), full(rb),
        ],
        out_specs=(
            pl.BlockSpec((None, tm, D), lambda b, s: (b, s, 0)),
            pl.BlockSpec((None, tm, D), lambda b, s: (b, s, 0)),
            pl.BlockSpec((None, tm, N_EXPERTS), lambda b, s: (b, s, 0)),
        ),
        scratch_shapes=[pltpu.VMEM((2 * T, FNET_WIDTH), BF16)],
        compiler_params=_cparams(2, 48),
        name="odd_out",
    )(y, bonus, gc, u, dft, cs64, h, gm, shf, scf, gfn, gn_g, gn_b, g2_bf, wout, rw, rb)


def _dft_tables(T):
    t = jnp.arange(T, dtype=jnp.int32)
    ang = ((t[:, None] * t[None, :]) % T).astype(F32) * (2.0 * math.pi / T)
    scale = 1.0 / math.sqrt(T * FNET_GROUP)
    return (jnp.concatenate([jnp.cos(ang), -jnp.sin(ang)], axis=1) * scale).astype(BF16)


def _channel_dft():
    c = jnp.arange(FNET_GROUP, dtype=jnp.int32)
    ang = ((c[:, None] * c[None, :]) % FNET_GROUP).astype(F32) * (2.0 * math.pi / FNET_GROUP)
    eye = jnp.eye(FNET_GROUPS, dtype=F32)
    return jnp.concatenate([jnp.kron(eye, jnp.cos(ang)), jnp.kron(eye, jnp.sin(ang))], axis=1).astype(BF16)


def _moe_kernel(be_ref, nu_ref, x_ref, wg_ref, wl_ref, bg_ref, bl_ref, wd_ref, bd_ref, y_ref):
    i = pl.program_id(0)

    @pl.when(i < nu_ref[0])
    def _():
        x = x_ref[...]
        glu = jnp.minimum(_dot(x, wg_ref[...]) + bg_ref[...], SWIGLU_LIMIT)
        lin = jnp.clip(_dot(x, wl_ref[...]) + bl_ref[...], -SWIGLU_LIMIT, SWIGLU_LIMIT)
        act = glu * _sigmoid(SWIGLU_ALPHA * glu) * (lin + 1.0)
        y_ref[...] = _dot(act.astype(BF16), wd_ref[...]) + bd_ref[...]

    @pl.when(i >= nu_ref[0])
    def _():
        y_ref[...] = jnp.zeros_like(y_ref)


def _moe_experts(block_e, n_used, xg, wg, wl, bg, bl, wd, bd):
    n_rows, D = xg.shape
    E, _, DE = wg.shape
    nblk = n_rows // MOE_ROWS
    grid_spec = pltpu.PrefetchScalarGridSpec(
        num_scalar_prefetch=2,
        grid=(nblk,),
        in_specs=[
            pl.BlockSpec((MOE_ROWS, D), lambda i, be, nu: (i, 0)),
            pl.BlockSpec((None, D, DE), lambda i, be, nu: (be[i], 0, 0)),
            pl.BlockSpec((None, D, DE), lambda i, be, nu: (be[i], 0, 0)),
            pl.BlockSpec((None, 1, DE), lambda i, be, nu: (be[i], 0, 0)),
            pl.BlockSpec((None, 1, DE), lambda i, be, nu: (be[i], 0, 0)),
            pl.BlockSpec((None, DE, D), lambda i, be, nu: (be[i], 0, 0)),
            pl.BlockSpec((None, 1, D), lambda i, be, nu: (be[i], 0, 0)),
        ],
        out_specs=pl.BlockSpec((MOE_ROWS, D), lambda i, be, nu: (i, 0)),
    )
    return pl.pallas_call(
        _moe_kernel,
        out_shape=jax.ShapeDtypeStruct((n_rows, D), F32),
        grid_spec=grid_spec,
        compiler_params=_cparams(1, 56),
        name="moe_experts",
    )(block_e, n_used, xg, wg, wl, bg, bl, wd, bd)


def _route(logits):
    n_tok = logits.shape[0]
    top_val, top_idx = lax.top_k(logits, TOP_K)
    gates = jax.nn.softmax(top_val, axis=-1)
    n_assign = n_tok * TOP_K
    flat_e = top_idx.reshape(-1).astype(jnp.int32)
    flat_tok = jnp.repeat(jnp.arange(n_tok, dtype=jnp.int32), TOP_K)
    order = jnp.argsort(flat_e)
    sorted_e, sorted_tok = flat_e[order], flat_tok[order]
    counts = jnp.sum((flat_e[:, None] == jnp.arange(N_EXPERTS, dtype=jnp.int32)[None, :]).astype(jnp.int32), axis=0)
    starts = jnp.cumsum(counts) - counts
    padded = (counts + MOE_ROWS - 1) // MOE_ROWS * MOE_ROWS
    pends = jnp.cumsum(padded)
    pstarts = pends - padded
    dest_sorted = pstarts[sorted_e] + (jnp.arange(n_assign, dtype=jnp.int32) - starts[sorted_e])
    n_blocks = -(-n_assign // MOE_ROWS) + N_EXPERTS
    n_rows = n_blocks * MOE_ROWS
    block_e = jnp.minimum(jnp.searchsorted(pends, jnp.arange(n_blocks, dtype=jnp.int32) * MOE_ROWS, side='right'),
                          N_EXPERTS - 1).astype(jnp.int32)
    rows = jnp.arange(n_rows, dtype=jnp.int32)
    row_e = jnp.repeat(block_e, MOE_ROWS)
    off = rows - pstarts[row_e]
    src = jnp.clip(starts[row_e] + off, 0, n_assign - 1)
    row_tok = jnp.where(off < counts[row_e], sorted_tok[src], 0)
    dest = jnp.zeros((n_assign,), jnp.int32).at[order].set(dest_sorted)
    n_used = (pends[-1] // MOE_ROWS).astype(jnp.int32).reshape(1)
    return gates, dest.reshape(n_tok, TOP_K), row_tok, block_e, n_used


def _combine_kernel(yg_ref, gate_ref, h_ref, gf_ref, fg_ref, o_ref, *, final):
    gate = gate_ref[...]
    acc = yg_ref[0] * gate[:, 0:1]
    for k in range(1, TOP_K):
        acc = acc + yg_ref[k] * gate[:, k:k + 1]
    hn = h_ref[...] + gf_ref[...] * acc
    if final:
        ms = jnp.mean(hn * hn, axis=-1, keepdims=True)
        hn = hn * lax.rsqrt(ms + NORM_EPS) * fg_ref[...]
    o_ref[...] = hn


def _combine(yg, gates, h, gf, final_g, *, final, tm):
    B, T, D = h.shape
    bm = _bmap(gf)
    kern = functools.partial(_combine_kernel, final=final)
    return pl.pallas_call(
        kern,
        out_shape=jax.ShapeDtypeStruct((B, T, D), F32),
        grid=(B, T // tm),
        in_specs=[
            pl.BlockSpec((TOP_K, None, tm, D), lambda b, s: (0, b, s, 0)),
            pl.BlockSpec((None, tm, TOP_K), lambda b, s: (b, s, 0)),
            pl.BlockSpec((None, tm, D), lambda b, s: (b, s, 0)),
            pl.BlockSpec((None, 1, D), lambda b, s: (bm(b), 0, 0)),
            pl.BlockSpec((1, D), lambda b, s: (0, 0)),
        ],
        out_specs=pl.BlockSpec((None, tm, D), lambda b, s: (b, s, 0)),
        compiler_params=_cparams(2, 48),
        name="moe_combine",
    )(yg, gates, h, gf, final_g)


def _block_diag(w):
    G, c, d = w.shape
    eye = jnp.eye(G, dtype=w.dtype)
    return (eye[:, None, :, None] * w[:, :, None, :]).reshape(G * c, G * d)


def kernel(x, c, ctx, c_ctx, ada_w, ada_b, norm_mix_g, norm_ffn_g, ev_w_in, ev_w_out, pool_w, pool_scale,
           att_sink, od_w_in, od_w_out, rw_mu, rw_w0, rw_w2, rw_a0, rw_a2, rw_g2, rw_k_k, rw_k_a, rw_r_k,
           rw_gn_g, rw_gn_b, router_w, router_b, exp_w_gu, exp_b_gu, exp_w_dn, exp_b_dn, final_g):
    B, S, D = x.shape
    C = ctx.shape[1]
    depth = ada_w.shape[0]
    DE = exp_w_dn.shape[2]

    n_mod = -(-(B + 1) // SUBLANES) * SUBLANES
    cvec = jnp.concatenate([c, c_ctx[None, :], jnp.zeros((n_mod - B - 1, D), F32)], axis=0)
    mods = _ada_all(cvec, ada_w, ada_b)

    cos_t, sin_t = _rope_tables(S)
    dft_x = dft_c = cs64 = None
    if depth > 1:
        dft_x, dft_c, cs64 = _dft_tables(S), _dft_tables(C), _channel_dft()

    tm_x = 512 if S % 512 == 0 else S
    tm_o = 256 if S % 256 == 0 else S

    h, hc = x, ctx
    for i in range(depth):
        last = i == depth - 1
        j = i // 2
        m6 = mods[i].reshape(n_mod, 6, D)
        mx = [m6[:B, k][:, None, :] for k in range(6)]
        mc = [m6[B:B + 1, k][:, None, :] for k in range(6)]
        g_mix = norm_mix_g[i][None, :]
        g_ffn = norm_ffn_g[i][None, :]
        rw = router_w[i]
        rb = router_b[i][None, :]

        if i % 2 == 0:
            assert not last, "an even final layer is not part of this block"
            w_in = ev_w_in[j].astype(BF16)
            w_out = ev_w_out[j].astype(BF16)
            pw_bd = _block_diag(pool_w[j]).astype(BF16)
            ps = pool_scale[j][None, :]
            sink = att_sink[j]
            pool_ux, qx, kvx = _even_in(h, mx[0], mx[1], g_mix, w_in, cos_t, sin_t, rope=True, tm=tm_x)
            pool_uc, qc, kvc = _even_in(hc, mc[0], mc[1], g_mix, w_in, cos_t[:C], sin_t[:C], rope=False, tm=C)
            h, fx_x, lg_x = _even_mix(sink, qx, kvx, kvc, pool_ux, h, mx[2], mx[3], mx[4], g_ffn, w_out,
                                      pw_bd, ps, rw, rb, local=True)
            hc, fx_c, lg_c = _even_mix(sink, qc, None, kvc, pool_uc, hc, mc[2], mc[3], mc[4], g_ffn, w_out,
                                       pw_bd, ps, rw, rb, local=False)
        else:
            w_in = od_w_in[j].astype(BF16)
            w_out = od_w_out[j].astype(BF16)
            od_args = (w_in, rw_mu[j], rw_w0[j], rw_w2[j].astype(BF16), rw_a0[j], rw_a2[j].astype(BF16),
                       rw_k_k[j][None, :], rw_k_a[j][None, :], rw_r_k[j].reshape(1, RWKV_WIDTH))
            rx, vx, kkx, bonus_x, lwx, kdx, bdx, gcx, ux = _odd_in(h, mx[0], mx[1], g_mix, *od_args, tm=tm_o)
            rc, vc, kkc, bonus_c, lwc, kdc, bdc, gcc, uc = _odd_in(hc, mc[0], mc[1], g_mix, *od_args, tm=C)
            s0 = jnp.zeros((2, B, RWKV_HEADS, HEAD_DIM, HEAD_DIM), F32)
            y_c, s_c = _scan(rc, vc, kkc, lwc, kdc, bdc, s0)
            y_x, _ = _scan(rx, vx, kkx, lwx, kdx, bdx, s_c)
            ro_args = (g_ffn, rw_gn_g[j][None, :], rw_gn_b[j][None, :], rw_g2[j].astype(BF16), w_out, rw, rb)
            h, fx_x, lg_x = _odd_out(y_x, bonus_x, gcx, ux, dft_x, cs64, h, mx[2], mx[3], mx[4], *ro_args,
                                     tm=tm_o)
            if not last:
                hc, fx_c, lg_c = _odd_out(y_c, bonus_c, gcc, uc, dft_c, cs64, hc, mc[2], mc[3], mc[4],
                                          *ro_args, tm=C)

        n_x = B * S
        if last:
            fx_all = fx_x.reshape(n_x, D)
            lg_all = lg_x.reshape(n_x, N_EXPERTS)
        else:
            fx_all = jnp.concatenate([fx_x.reshape(n_x, D), fx_c.reshape(B * C, D)], axis=0)
            lg_all = jnp.concatenate([lg_x.reshape(n_x, N_EXPERTS), lg_c.reshape(B * C, N_EXPERTS)], axis=0)
        gates, dest, row_tok, block_e, n_used = _route(lg_all)
        xg = jnp.take(fx_all, row_tok, axis=0)
        wg = exp_w_gu[i][:, :, 0::2].astype(BF16)
        wl = exp_w_gu[i][:, :, 1::2].astype(BF16)
        bg = exp_b_gu[i][:, None, 0::2]
        bl = exp_b_gu[i][:, None, 1::2]
        wd = exp_w_dn[i].astype(BF16)
        bd = exp_b_dn[i][:, None, :]
        y = _moe_experts(block_e, n_used, xg, wg, wl, bg, bl, wd, bd)
        tm_cx = 256 if S % 256 == 0 else S
        yg_x = jnp.take(y, dest[:n_x].T.reshape(TOP_K, B, S), axis=0)
        h = _combine(yg_x, gates[:n_x].reshape(B, S, TOP_K), h, mx[5], final_g[None, :], final=last, tm=tm_cx)
        if not last:
            yg_c = jnp.take(y, dest[n_x:].T.reshape(TOP_K, B, C), axis=0)
            hc = _combine(yg_c, gates[n_x:].reshape(B, C, TOP_K), hc, mc[5], final_g[None, :], final=False, tm=C)
    return h
```

```python
import functools
import math

import jax
import jax.numpy as jnp
from jax import lax
from jax.experimental import pallas as pl
from jax.experimental.pallas import tpu as pltpu

F32 = jnp.float32
BF16 = jnp.bfloat16

GRID_W = 64
HEAD_DIM = 64
ROT_FREQS = HEAD_DIM // 4
ROPE_THETA = 10000.0
NORM_EPS = 1e-5
POOL_WINDOWS = (2, 4, 8, 16)
POOL_GROUP = 64
POOL_WIDTH = POOL_GROUP * len(POOL_WINDOWS)
POOL_HALO = max(POOL_WINDOWS) // 2
ATT_HEADS = 12
ATT_KV_HEADS = 3
ATT_GROUP = ATT_HEADS // ATT_KV_HEADS
ATT_WINDOW = 128
ATT_BLOCK = 128
ATT_Q = ATT_HEADS * HEAD_DIM
ATT_KV = ATT_KV_HEADS * HEAD_DIM
EVEN_IN = POOL_WIDTH + ATT_Q + 2 * ATT_KV
RWKV_HEADS = 12
RWKV_WIDTH = RWKV_HEADS * HEAD_DIM
DECAY_LORA = 64
ICLR_LORA = 64
GATE_LORA = 160
GN_EPS = 64e-5
LORA_LO = 3 * RWKV_WIDTH
STATE_HI = LORA_LO + 2 * DECAY_LORA + 2 * ICLR_LORA
RWKV_IN = STATE_HI + GATE_LORA
FNET_GROUPS = 4
FNET_GROUP = 64
FNET_WIDTH = FNET_GROUPS * FNET_GROUP
ODD_IN = RWKV_IN + FNET_WIDTH
N_EXPERTS = 32
TOP_K = 4
SWIGLU_LIMIT = 7.0
SWIGLU_ALPHA = 1.702

LANES = 128
SUBLANES = 8
SCAN_CHUNK = 64
SCAN_GROUP = 4
SPLIT_ROWS = 256
MOE_ROWS = 512
MASK_NEG = -1e30


def _cparams(n_axes, vmem_mb):
    return pltpu.CompilerParams(
        dimension_semantics=("arbitrary",) * n_axes,
        vmem_limit_bytes=vmem_mb * 1024 * 1024,
    )


def _dot(a, b):
    return jnp.dot(a, b, preferred_element_type=F32)


def _dot_nt(a, b):
    return lax.dot_general(a, b, (((1,), (1,)), ((), ())), preferred_element_type=F32)


def _dot_tn(a, b):
    return lax.dot_general(a, b, (((0,), (0,)), ((), ())), preferred_element_type=F32)


def _split(x):
    hi = x.astype(BF16)
    lo = (x - hi.astype(F32)).astype(BF16)
    return hi, lo


def _dot3(a, b):
    ah, al = _split(a)
    bh, bl = _split(b)
    return _dot(ah, bh) + (_dot(al, bh) + _dot(ah, bl))


def _dot_exact_lhs(a_exact, b):
    a16 = a_exact.astype(BF16)
    b1 = b.astype(BF16)
    r1 = b - b1.astype(F32)
    b2 = r1.astype(BF16)
    b3 = (r1 - b2.astype(F32)).astype(BF16)
    return _dot(a16, b1) + (_dot(a16, b2) + _dot(a16, b3))


def _modnorm(x, g, sh, sc):
    ms = jnp.mean(x * x, axis=-1, keepdims=True)
    xn = x * lax.rsqrt(ms + NORM_EPS) * g
    return xn * (1.0 + sc) + sh


def _sigmoid(x):
    return 1.0 / (1.0 + jnp.exp(-x))


def _head_sum(x):
    n = x.shape[1] // LANES
    lane = lax.broadcasted_iota(jnp.int32, (x.shape[0], LANES), 1)
    lo_mask = lane < HEAD_DIM
    parts = []
    for c in range(n):
        xc = x[:, c * LANES:(c + 1) * LANES]
        s_lo = jnp.sum(jnp.where(lo_mask, xc, 0.0), axis=-1, keepdims=True)
        s_hi = jnp.sum(jnp.where(lo_mask, 0.0, xc), axis=-1, keepdims=True)
        parts.append(jnp.where(lo_mask, s_lo, s_hi))
    return jnp.concatenate(parts, axis=1)


def _ada_kernel(c_ref, w_ref, b_ref, o_ref):
    x = c_ref[...]
    x = x * _sigmoid(x)
    o_ref[...] = _dot3(x, w_ref[...]) + b_ref[...]


def _ada_all(cvec, ada_w, ada_b):
    depth, d, n6 = ada_w.shape
    r = cvec.shape[0]
    tn = 1536 if n6 % 1536 == 0 else n6
    return pl.pallas_call(
        _ada_kernel,
        out_shape=jax.ShapeDtypeStruct((depth, r, n6), F32),
        grid=(depth, n6 // tn),
        in_specs=[
            pl.BlockSpec((r, d), lambda i, j: (0, 0)),
            pl.BlockSpec((None, d, tn), lambda i, j: (i, 0, j)),
            pl.BlockSpec((None, 1, tn), lambda i, j: (i, 0, j)),
        ],
        out_specs=pl.BlockSpec((None, r, tn), lambda i, j: (i, 0, j)),
        compiler_params=_cparams(2, 48),
        name="ada_mod",
    )(cvec, ada_w, ada_b.reshape(depth, 1, n6))


def _bmap(arr):
    if arr.shape[0] == 1:
        return lambda b: 0
    return lambda b: b


def _even_in_kernel(h_ref, sh_ref, sc_ref, g_ref, w_ref, cos_ref, sin_ref,
                    pool_ref, q_ref, kv_ref, *, rope):
    a = _modnorm(h_ref[...], g_ref[...], sh_ref[...], sc_ref[...]).astype(BF16)
    px = _dot(a, w_ref[...])
    pool_ref[...] = px[:, :POOL_WIDTH]
    n_chunks = (ATT_Q + 2 * ATT_KV) // LANES
    n_full = (ATT_Q + ATT_KV) // LANES
    outs = []
    if rope:
        lane = lax.broadcasted_iota(jnp.int32, (px.shape[0], LANES), 1)
        first = (lane % (2 * ROT_FREQS)) < ROT_FREQS
    for c in range(n_chunks):
        x = px[:, POOL_WIDTH + c * LANES:POOL_WIDTH + (c + 1) * LANES]
        if rope and c <= n_full:
            t0 = 0 if c < n_full else LANES
            cs = cos_ref[:, t0:t0 + LANES]
            sn = sin_ref[:, t0:t0 + LANES]
            rot = jnp.where(first, pltpu.roll(x, LANES - ROT_FREQS, 1), pltpu.roll(x, ROT_FREQS, 1))
            x = x * cs + rot * sn
        if c < ATT_Q // LANES:
            x = x * (HEAD_DIM ** -0.5)
        outs.append(x.astype(BF16))
    nq = ATT_Q // LANES
    q_ref[...] = jnp.concatenate(outs[:nq], axis=1)
    kv_ref[...] = jnp.concatenate(outs[nq:], axis=1)


def _even_in(h, sh, sc, g, w_bf, cos_t, sin_t, *, rope, tm):
    B, T, D = h.shape
    nt = T // tm
    bm = _bmap(sh)
    kern = functools.partial(_even_in_kernel, rope=rope)
    return pl.pallas_call(
        kern,
        out_shape=(
            jax.ShapeDtypeStruct((B, T, POOL_WIDTH), F32),
            jax.ShapeDtypeStruct((B, T, ATT_Q), BF16),
            jax.ShapeDtypeStruct((B, T, 2 * ATT_KV), BF16),
        ),
        grid=(nt, B),
        in_specs=[
            pl.BlockSpec((None, tm, D), lambda s, b: (b, s, 0)),
            pl.BlockSpec((None, 1, D), lambda s, b: (bm(b), 0, 0)),
            pl.BlockSpec((None, 1, D), lambda s, b: (bm(b), 0, 0)),
            pl.BlockSpec((1, D), lambda s, b: (0, 0)),
            pl.BlockSpec((D, EVEN_IN), lambda s, b: (0, 0)),
            pl.BlockSpec((tm, 2 * LANES), lambda s, b: (s, 0)),
            pl.BlockSpec((tm, 2 * LANES), lambda s, b: (s, 0)),
        ],
        out_specs=(
            pl.BlockSpec((None, tm, POOL_WIDTH), lambda s, b: (b, s, 0)),
            pl.BlockSpec((None, tm, ATT_Q), lambda s, b: (b, s, 0)),
            pl.BlockSpec((None, tm, 2 * ATT_KV), lambda s, b: (b, s, 0)),
        ),
        compiler_params=_cparams(2, 48),
        name="even_in",
    )(h, sh, sc, g, w_bf, cos_t, sin_t)


def _rope_tables(T):
    t = jnp.arange(T, dtype=jnp.int32)
    row = (t // GRID_W).astype(F32)
    col = (t % GRID_W).astype(F32)
    inv_freq = ROPE_THETA ** (-jnp.arange(ROT_FREQS, dtype=F32) / ROT_FREQS)
    ang_r = row[:, None] * inv_freq
    ang_c = col[:, None] * inv_freq
    cos_h = jnp.concatenate([jnp.cos(ang_r)] * 2 + [jnp.cos(ang_c)] * 2, axis=1)
    sin_h = jnp.concatenate([-jnp.sin(ang_r), jnp.sin(ang_r), -jnp.sin(ang_c), jnp.sin(ang_c)], axis=1)
    one = jnp.ones_like(cos_h)
    zero = jnp.zeros_like(sin_h)
    cos_t = jnp.concatenate([cos_h, cos_h, cos_h, one], axis=1)
    sin_t = jnp.concatenate([sin_h, sin_h, sin_h, zero], axis=1)
    return cos_t, sin_t


def _residual_and_router(h, y, gm, gf_norm, shf, scf, rw, rb, hn_ref, fx_ref, lg_ref):
    hn = h + gm * y
    hn_ref[...] = hn
    fx = _modnorm(hn, gf_norm, shf, scf)
    fx_ref[...] = fx.astype(BF16)
    lg_ref[...] = _dot3(fx, rw) + rb


def _even_mix_kernel(*refs, local, tq, T, n_ctx):
    it = iter(refs)
    sink_ref = next(it)
    q_ref = next(it)
    if local:
        kvp_ref, kvc_ref, kvn_ref = next(it), next(it), next(it)
    ckv_ref = next(it)
    up_ref, uc_ref, un_ref = next(it), next(it), next(it)
    h_ref, gm_ref, shf_ref, scf_ref, gfn_ref = next(it), next(it), next(it), next(it), next(it)
    wout_ref, pw_ref, ps_ref, rw_ref, rb_ref = next(it), next(it), next(it), next(it), next(it)
    hn_ref, fx_ref, lg_ref = next(it), next(it), next(it)

    j = pl.program_id(1)
    nb = pl.num_programs(1)

    n_loc = 3 * ATT_BLOCK if local else 0
    n_keys = n_loc + n_ctx
    rows = ATT_GROUP * tq
    if local:
        r_i = lax.broadcasted_iota(jnp.int32, (rows, n_loc), 0)
        c_i = lax.broadcasted_iota(jnp.int32, (rows, n_loc), 1)
        q_pos = j * tq + r_i % tq
        k_pos = (j - 1) * ATT_BLOCK + c_i
        valid = (jnp.abs(q_pos - k_pos) <= ATT_WINDOW) & (k_pos >= 0) & (k_pos < T)
    row_head = lax.broadcasted_iota(jnp.int32, (rows, 1), 0) // tq
    q = q_ref[...]
    ckv = ckv_ref[...]
    if local:
        kvl = jnp.concatenate([kvp_ref[...], kvc_ref[...], kvn_ref[...]], axis=0)
    att = []
    for g in range(ATT_KV_HEADS):
        qs = jnp.concatenate(
            [q[:, (g * ATT_GROUP + i) * HEAD_DIM:(g * ATT_GROUP + i + 1) * HEAD_DIM] for i in range(ATT_GROUP)],
            axis=0)
        kc = ckv[:, g * HEAD_DIM:(g + 1) * HEAD_DIM]
        vc = ckv[:, ATT_KV + g * HEAD_DIM:ATT_KV + (g + 1) * HEAD_DIM]
        s_ctx = _dot_nt(qs, kc)
        sink = jnp.zeros((rows, 1), F32)
        for i in range(ATT_GROUP):
            sink = jnp.where(row_head == i, sink_ref[g * ATT_GROUP + i], sink)
        if local:
            kl = kvl[:, g * HEAD_DIM:(g + 1) * HEAD_DIM]
            vl = kvl[:, ATT_KV + g * HEAD_DIM:ATT_KV + (g + 1) * HEAD_DIM]
            s_loc = jnp.where(valid, _dot_nt(qs, kl), MASK_NEG)
            m = jnp.maximum(jnp.maximum(jnp.max(s_loc, axis=-1, keepdims=True),
                                        jnp.max(s_ctx, axis=-1, keepdims=True)), sink)
            p_loc = jnp.exp(s_loc - m)
            p_ctx = jnp.exp(s_ctx - m)
            den = (jnp.sum(p_loc, axis=-1, keepdims=True) + jnp.sum(p_ctx, axis=-1, keepdims=True)
                   + jnp.exp(sink - m))
            o = _dot(p_loc.astype(BF16), vl) + _dot(p_ctx.astype(BF16), vc)
        else:
            m = jnp.maximum(jnp.max(s_ctx, axis=-1, keepdims=True), sink)
            p_ctx = jnp.exp(s_ctx - m)
            den = jnp.sum(p_ctx, axis=-1, keepdims=True) + jnp.exp(sink - m)
            o = _dot(p_ctx.astype(BF16), vc)
        o = o / den
        for i in range(ATT_GROUP):
            att.append(o[i * tq:(i + 1) * tq])
    att_x = jnp.concatenate(att, axis=1)

    u = uc_ref[...]
    up = jnp.where(j > 0, up_ref[...], 0.0)
    un = jnp.where(j < nb - 1, un_ref[...], 0.0)
    e = jnp.concatenate([up, u, un], axis=0)
    n_e = tq + 2 * POOL_HALO

    def shifted(x, k):
        return pltpu.roll(x, k % n_e, 0)

    a1 = e + shifted(e, 1)
    a2 = shifted(a1, 1) + shifted(a1, -1)
    a3 = shifted(a2, 2) + shifted(a2, -2)
    a4 = shifted(a3, 4) + shifted(a3, -4)
    lane = lax.broadcasted_iota(jnp.int32, (n_e, POOL_WIDTH), 1)
    grp = lane // POOL_GROUP
    win = jnp.where(grp == 0, a1, jnp.where(grp == 1, a2, jnp.where(grp == 2, a3, a4)))
    win = win[POOL_HALO:POOL_HALO + tq]
    t_i = j * tq + lax.broadcasted_iota(jnp.int32, (tq, POOL_WIDTH), 0)
    half = jnp.left_shift(1, lax.broadcasted_iota(jnp.int32, (tq, POOL_WIDTH), 1) // POOL_GROUP)
    cnt = (jnp.minimum(t_i + half, T) - jnp.maximum(t_i - half, 0)).astype(F32)
    pooled = win / cnt - u
    pool_x = _dot(pooled.astype(BF16), pw_ref[...]) * ps_ref[...]

    mix = jnp.concatenate([pool_x, att_x], axis=1).astype(BF16)
    y = _dot(mix, wout_ref[...])
    _residual_and_router(h_ref[...], y, gm_ref[...], gfn_ref[...], shf_ref[...], scf_ref[...],
                         rw_ref[...], rb_ref[...], hn_ref, fx_ref, lg_ref)


def _even_mix(sink, q, kv, ckv, pool_u, h, gm, shf, scf, gfn, wout, pw_bd, pscale, rw, rb, *, local):
    B, T, D = h.shape
    n_ctx = ckv.shape[1]
    tq = ATT_BLOCK if local else T
    nb = T // tq
    r8 = tq // SUBLANES
    n8 = T // SUBLANES
    bm = _bmap(gm)
    kern = functools.partial(_even_mix_kernel, local=local, tq=tq, T=T, n_ctx=n_ctx)
    vec = lambda: pl.BlockSpec((None, 1, D), lambda b, j: (bm(b), 0, 0))
    full = lambda a: pl.BlockSpec(a.shape, lambda b, j: (0,) * a.ndim)
    in_specs = [pl.BlockSpec(memory_space=pltpu.SMEM),
                pl.BlockSpec((None, tq, ATT_Q), lambda b, j: (b, j, 0))]
    args = [sink, q]
    if local:
        in_specs += [
            pl.BlockSpec((None, ATT_BLOCK, 2 * ATT_KV), lambda b, j: (b, jnp.maximum(j - 1, 0), 0)),
            pl.BlockSpec((None, ATT_BLOCK, 2 * ATT_KV), lambda b, j: (b, j, 0)),
            pl.BlockSpec((None, ATT_BLOCK, 2 * ATT_KV), lambda b, j: (b, jnp.minimum(j + 1, nb - 1), 0)),
        ]
        args += [kv, kv, kv]
    in_specs += [
        pl.BlockSpec((None, n_ctx, 2 * ATT_KV), lambda b, j: (b, 0, 0)),
        pl.BlockSpec((None, SUBLANES, POOL_WIDTH), lambda b, j: (b, jnp.maximum(j * r8 - 1, 0), 0)),
        pl.BlockSpec((None, tq, POOL_WIDTH), lambda b, j: (b, j, 0)),
        pl.BlockSpec((None, SUBLANES, POOL_WIDTH), lambda b, j: (b, jnp.minimum((j + 1) * r8, n8 - 1), 0)),
        pl.BlockSpec((None, tq, D), lambda b, j: (b, j, 0)),
        vec(), vec(), vec(), full(gfn), full(wout), full(pw_bd), full(pscale), full(rw), full(rb),
    ]
    args += [ckv, pool_u, pool_u, pool_u, h, gm, shf, scf, gfn, wout, pw_bd, pscale, rw, rb]
    return pl.pallas_call(
        kern,
        out_shape=(
            jax.ShapeDtypeStruct((B, T, D), F32),
            jax.ShapeDtypeStruct((B, T, D), BF16),
            jax.ShapeDtypeStruct((B, T, N_EXPERTS), F32),
        ),
        grid=(B, nb),
        in_specs=in_specs,
        out_specs=(
            pl.BlockSpec((None, tq, D), lambda b, j: (b, j, 0)),
            pl.BlockSpec((None, tq, D), lambda b, j: (b, j, 0)),
            pl.BlockSpec((None, tq, N_EXPERTS), lambda b, j: (b, j, 0)),
        ),
        compiler_params=_cparams(2, 48),
        name="even_mix_local" if local else "even_mix_ctx",
    )(*args)


def _odd_in_kernel(h_ref, hp_ref, hn_ref, sh_ref, sc_ref, g_ref, w_ref, mu_ref, w0_ref, w2_ref,
                   a0_ref, a2_ref, kk_ref, ka_ref, rk_ref,
                   r_out, v_out, kkn_out, bonus_out, lw_out, kd_out, bd_out, gc_out, fn_out, *, tm):
    s = pl.program_id(1)
    ns = pl.num_programs(1)
    g, sh, sc = g_ref[...], sh_ref[...], sc_ref[...]
    w = w_ref[...]
    a = _modnorm(h_ref[...], g, sh, sc).astype(BF16)
    px = _dot(a, w)
    fn_out[...] = px[:, RWKV_IN:]
    main = px[:, :RWKV_IN]
    ap = _modnorm(hp_ref[...], g, sh, sc).astype(BF16)
    an = _modnorm(hn_ref[...], g, sh, sc).astype(BF16)
    halo = _dot(jnp.concatenate([ap, an], axis=0), w[:, :RWKV_IN])
    prev_row = jnp.where(s > 0, halo[SUBLANES - 1:SUBLANES], 0.0)
    next_row = jnp.where(s < ns - 1, halo[SUBLANES:SUBLANES + 1], 0.0)
    row = lax.broadcasted_iota(jnp.int32, (tm, 1), 0)
    prev = jnp.where(row == 0, prev_row, pltpu.roll(main, 1, 0))
    nxt = jnp.where(row == tm - 1, next_row, pltpu.roll(main, tm - 1, 0))
    mu = mu_ref[...]
    fs = main + mu[0:1] * (prev - main) + mu[1:2] * (nxt - main)

    W = RWKV_WIDTH
    r = fs[:, :W]
    k = fs[:, W:2 * W]
    v = fs[:, 2 * W:3 * W]
    lora = fs[:, LORA_LO:STATE_HI]
    gc_out[...] = fs[:, STATE_HI:RWKV_IN]
    r_out[...] = r
    v_out[...] = v

    kx = k * kk_ref[...]
    nrm = jnp.sqrt(_head_sum(kx * kx))
    kkn = kx / jnp.maximum(nrm, 1e-12)
    kkn_out[...] = kkn
    ka = ka_ref[...]
    ksum = None
    for d in range(2):
        wd = lora[:, d * DECAY_LORA:(d + 1) * DECAY_LORA]
        o_a = 2 * DECAY_LORA
        ad = lora[:, o_a + d * ICLR_LORA:o_a + (d + 1) * ICLR_LORA]
        xw = _dot(jnp.tanh(wd).astype(BF16), w2_ref[d]) + w0_ref[d:d + 1]
        z = -xw
        softplus = jnp.maximum(z, 0.0) + jnp.log(1.0 + jnp.exp(-jnp.abs(z)))
        w_log = -softplus - 0.5
        lw_out[d] = -jnp.exp(w_log)
        xa = _dot(ad.astype(BF16), a2_ref[d]) + a0_ref[d:d + 1]
        a_d = _sigmoid(xa)
        k_d = k * (1.0 + (a_d - 1.0) * ka)
        kd_out[d] = k_d
        bd_out[d] = kkn * a_d
        ksum = k_d if ksum is None else ksum + k_d
    coef = _head_sum(r * ksum * rk_ref[...])
    bonus_out[...] = coef * v


def _odd_in(h, sh, sc, g, w_bf, mu, w0, w2_bf, a0, a2_bf, k_k, k_a, r_k, *, tm):
    B, T, D = h.shape
    ns = T // tm
    r8 = tm // SUBLANES
    n8 = T // SUBLANES
    bm = _bmap(sh)
    W = RWKV_WIDTH
    full = lambda a: pl.BlockSpec(a.shape, lambda b, s: (0,) * a.ndim)
    tok = lambda n: pl.BlockSpec((None, tm, n), lambda b, s: (b, s, 0))
    tok2 = lambda n: pl.BlockSpec((2, None, tm, n), lambda b, s: (0, b, s, 0))
    kern = functools.partial(_odd_in_kernel, tm=tm)
    return pl.pallas_call(
        kern,
        out_shape=(
            jax.ShapeDtypeStruct((B, T, W), F32),
            jax.ShapeDtypeStruct((B, T, W), F32),
            jax.ShapeDtypeStruct((B, T, W), F32),
            jax.ShapeDtypeStruct((B, T, W), F32),
            jax.ShapeDtypeStruct((2, B, T, W), F32),
            jax.ShapeDtypeStruct((2, B, T, W), F32),
            jax.ShapeDtypeStruct((2, B, T, W), F32),
            jax.ShapeDtypeStruct((B, T, GATE_LORA), F32),
            jax.ShapeDtypeStruct((B, T, FNET_WIDTH), F32),
        ),
        grid=(B, ns),
        in_specs=[
            pl.BlockSpec((None, tm, D), lambda b, s: (b, s, 0)),
            pl.BlockSpec((None, SUBLANES, D), lambda b, s: (b, jnp.maximum(s * r8 - 1, 0), 0)),
            pl.BlockSpec((None, SUBLANES, D), lambda b, s: (b, jnp.minimum((s + 1) * r8, n8 - 1), 0)),
            pl.BlockSpec((None, 1, D), lambda b, s: (bm(b), 0, 0)),
            pl.BlockSpec((None, 1, D), lambda b, s: (bm(b), 0, 0)),
            full(g), full(w_bf), full(mu), full(w0), full(w2_bf), full(a0), full(a2_bf),
            full(k_k), full(k_a), full(r_k),
        ],
        out_specs=(tok(W), tok(W), tok(W), tok(W), tok2(W), tok2(W), tok2(W), tok(GATE_LORA), tok(FNET_WIDTH)),
        compiler_params=_cparams(2, 60),
        name="odd_in",
    )(h, h, h, sh, sc, g, w_bf, mu, w0, w2_bf, a0, a2_bf, k_k, k_a, r_k)


def _tri_inverse(m, n, nil):
    r_i = lax.broadcasted_iota(jnp.int32, (n, n), 0)
    c_i = lax.broadcasted_iota(jnp.int32, (n, n), 1)
    x = jnp.where(r_i == c_i, 1.0, 0.0) - m
    ph, pl_ = _split(m)
    for _ in range(int(math.log2(nil)) - 1):
        p = _dot(ph, ph) + (_dot(pl_, ph) + _dot(ph, pl_))
        ph, pl_ = _split(p)
        xh, xl = _split(x)
        x = x + (_dot(xh, ph) + (_dot(xl, ph) + _dot(xh, pl_)))
    return x


def _scan_kernel(r_ref, v_ref, kk_ref, lw_ref, kd_ref, bd_ref, s0_ref, y_ref, sT_ref, s_scr, *, L):
    d = pl.program_id(0)
    c = pl.program_id(2)
    nc = pl.num_programs(2)
    R = SCAN_GROUP * L
    GW = SCAN_GROUP * HEAD_DIM

    @pl.when(c == 0)
    def _():
        s_scr[...] = s0_ref[...]

    sign = 1 - 2 * d
    t_i = lax.broadcasted_iota(jnp.int32, (L, L), 0)
    s_i = lax.broadcasted_iota(jnp.int32, (L, L), 1)
    upto = ((t_i - s_i) * sign) >= 0

    row = lax.broadcasted_iota(jnp.int32, (R, GW), 0)
    col = lax.broadcasted_iota(jnp.int32, (R, GW), 1)
    same = (row // L) == (col // HEAD_DIM)
    order = ((row % L) - (col % L)) * sign
    before_bd = same & (order > 0)
    upto_bd = same & (order >= 0)

    lw = lw_ref[...]
    c_in = _dot_exact_lhs(jnp.where(upto, 1.0, 0.0), lw)
    c_ex = c_in - lw
    tot = jnp.sum(lw, axis=0, keepdims=True)
    kd = kd_ref[...]
    bd = bd_ref[...]
    inv = jnp.exp(-c_in)
    end = jnp.exp(tot - c_in)
    g_tot = jnp.exp(tot)
    cols = dict(
        kt=kk_ref[...] * jnp.exp(c_ex), rt=r_ref[...] * jnp.exp(c_in), kh=kd * inv, bh=bd * inv,
        v=v_ref[...], ke=kd * end, be=bd * end)

    def wide(x):
        return jnp.where(same, jnp.concatenate([x] * SCAN_GROUP, axis=0), 0.0).astype(BF16)

    ys = []
    for g in range(RWKV_HEADS // SCAN_GROUP):
        sl = slice(g * GW, (g + 1) * GW)
        w = {k: wide(a[:, sl]) for k, a in cols.items()}
        lhs = jnp.concatenate([w["kt"], w["rt"]], axis=0)
        s4 = _dot_nt(lhs, jnp.concatenate([w["kh"], w["bh"]], axis=0))
        a_kk = jnp.where(before_bd, s4[:R, :R], 0.0).astype(BF16)
        a_kb = jnp.where(before_bd, s4[:R, R:], 0.0)
        a_rk = jnp.where(upto_bd, s4[R:, :R], 0.0).astype(BF16)
        a_rb = jnp.where(upto_bd, s4[R:, R:], 0.0).astype(BF16)
        s_g = s_scr[g]
        ks = _dot_nt(lhs, s_g.astype(BF16))
        u = _dot3(_tri_inverse(a_kb, R, L), ks[:R] + _dot(a_kk, w["v"]))
        u16 = u.astype(BF16)
        yw = ks[R:] + _dot(a_rk, w["v"]) - _dot(a_rb, u16)
        y_g = yw[0:L]
        for hh in range(1, SCAN_GROUP):
            y_g = y_g + yw[hh * L:(hh + 1) * L]
        ys.append(y_g)
        vu = jnp.concatenate([w["v"], -u16], axis=0)
        ke = jnp.concatenate([w["ke"], w["be"]], axis=0)
        s_scr[g] = s_g * g_tot[:, sl] + _dot_tn(vu, ke)
    y_ref[...] = jnp.concatenate(ys, axis=1)

    @pl.when(c == nc - 1)
    def _():
        sT_ref[...] = s_scr[...]


def _scan_state_shape(B):
    gw = SCAN_GROUP * HEAD_DIM
    return (2, B, RWKV_HEADS // SCAN_GROUP, gw, gw)


def _scan(r, v, kk, lw, kd, bd, s0):
    B, T, W = r.shape
    L = SCAN_CHUNK
    assert L == HEAD_DIM and T % L == 0
    nc = T // L
    st_shape = _scan_state_shape(B)
    ci = lambda d, c: jnp.where(d == 0, c, nc - 1 - c)
    shared = pl.BlockSpec((None, L, W), lambda d, b, c: (b, ci(d, c), 0))
    per_dir = pl.BlockSpec((None, None, L, W), lambda d, b, c: (d, b, ci(d, c), 0))
    state = pl.BlockSpec((None, None) + st_shape[2:], lambda d, b, c: (d, b, 0, 0, 0))
    kern = functools.partial(_scan_kernel, L=L)
    return pl.pallas_call(
        kern,
        out_shape=(
            jax.ShapeDtypeStruct((2, B, T, W), F32),
            jax.ShapeDtypeStruct(st_shape, F32),
        ),
        grid=(2, B, nc),
        in_specs=[shared, shared, shared, per_dir, per_dir, per_dir, state],
        out_specs=(per_dir, state),
        scratch_shapes=[pltpu.VMEM(st_shape[2:], F32)],
        compiler_params=_cparams(3, 48),
        name="rwkv_scan",
    )(r, v, kk, lw, kd, bd, s0)


def _odd_out_kernel(y_ref, bonus_ref, gc_ref, u_ref, dft_ref, cs_ref, h_ref, gm_ref, shf_ref, scf_ref,
                    gfn_ref, gng_ref, gnb_ref, g2_ref, wout_ref, rw_ref, rb_ref,
                    hn_ref, fx_ref, lg_ref, ucs_scr, *, T):
    s = pl.program_id(1)

    @pl.when(s == 0)
    def _():
        t = _dot(u_ref[...].astype(BF16), cs_ref[...])
        ucs_scr[0:T, :] = t[:, :FNET_WIDTH].astype(BF16)
        ucs_scr[T:2 * T, :] = t[:, FNET_WIDTH:].astype(BF16)

    f = _dot(dft_ref[...], ucs_scr[...])
    y = y_ref[0] + y_ref[1]
    mean = _head_sum(y) * (1.0 / HEAD_DIM)
    yc = y - mean
    var = _head_sum(yc * yc) * (1.0 / HEAD_DIM)
    yn = yc * lax.rsqrt(var + GN_EPS) * gng_ref[...] + gnb_ref[...]
    out = yn + bonus_ref[...]
    gate = _dot(_sigmoid(gc_ref[...]).astype(BF16), g2_ref[...])
    o = out * gate
    mix = jnp.concatenate([o, f], axis=1).astype(BF16)
    ymix = _dot(mix, wout_ref[...])
    _residual_and_router(h_ref[...], ymix, gm_ref[...], gfn_ref[...], shf_ref[...], scf_ref[...],
                         rw_ref[...], rb_ref[...], hn_ref, fx_ref, lg_ref)


def _odd_out(y, bonus, gc, u, dft, cs64, h, gm, shf, scf, gfn, gn_g, gn_b, g2_bf, wout, rw, rb, *, tm):
    B, T, D = h.shape
    ns = T // tm
    W = RWKV_WIDTH
    bm = _bmap(gm)
    vec = lambda: pl.BlockSpec((None, 1, D), lambda b, s: (bm(b), 0, 0))
    full = lambda a: pl.BlockSpec(a.shape, lambda b, s: (0,) * a.ndim)
    kern = functools.partial(_odd_out_kernel, T=T)
    return pl.pallas_call(
        kern,
        out_shape=(
            jax.ShapeDtypeStruct((B, T, D), F32),
            jax.ShapeDtypeStruct((B, T, D), BF16),
            jax.ShapeDtypeStruct((B, T, N_EXPERTS), F32),
        ),
        grid=(B, ns),
        in_specs=[
            pl.BlockSpec((2, None, tm, W), lambda b, s: (0, b, s, 0)),
            pl.BlockSpec((None, tm, W), lambda b, s: (b, s, 0)),
            pl.BlockSpec((None, tm, GATE_LORA), lambda b, s: (b, s, 0)),
            pl.BlockSpec((None, T, FNET_WIDTH), lambda b, s: (b, 0, 0)),
            pl.BlockSpec((tm, 2 * T), lambda b, s: (s, 0)),
            full(cs64),
            pl.BlockSpec((None, tm, D), lambda b, s: (b, s, 0)),
            vec(), vec(), vec(), full(gfn), full(gn_g), full(gn_b), full(g2_bf), full(wout), full(rw), full(rb),
        ],
        out_specs=(
            pl.BlockSpec((None, tm, D), lambda b, s: (b, s, 0)),
            pl.BlockSpec((None, tm, D), lambda b, s: (b, s, 0)),
            pl.BlockSpec((None, tm, N_EXPERTS), lambda b, s: (b, s, 0)),
        ),
        scratch_shapes=[pltpu.VMEM((2 * T, FNET_WIDTH), BF16)],
        compiler_params=_cparams(2, 48),
        name="odd_out",
    )(y, bonus, gc, u, dft, cs64, h, gm, shf, scf, gfn, gn_g, gn_b, g2_bf, wout, rw, rb)


def _dft_tables(T):
    t = jnp.arange(T, dtype=jnp.int32)
    ang = ((t[:, None] * t[None, :]) % T).astype(F32) * (2.0 * math.pi / T)
    scale = 1.0 / math.sqrt(T * FNET_GROUP)
    return (jnp.concatenate([jnp.cos(ang), -jnp.sin(ang)], axis=1) * scale).astype(BF16)


def _channel_dft():
    c = jnp.arange(FNET_GROUP, dtype=jnp.int32)
    ang = ((c[:, None] * c[None, :]) % FNET_GROUP).astype(F32) * (2.0 * math.pi / FNET_GROUP)
    eye = jnp.eye(FNET_GROUPS, dtype=F32)
    return jnp.concatenate([jnp.kron(eye, jnp.cos(ang)), jnp.kron(eye, jnp.sin(ang))], axis=1).astype(BF16)


def _split_gu_kernel(w_ref, wg_ref, wl_ref, t_scr):
    n_slab = w_ref.shape[0] // LANES
    de = w_ref.shape[1] // 2
    for s in range(n_slab):
        t_scr[s] = w_ref[s * LANES:(s + 1) * LANES, :].T
    ev = [t_scr[s, pl.ds(0, de, stride=2), :].T for s in range(n_slab)]
    od = [t_scr[s, pl.ds(1, de, stride=2), :].T for s in range(n_slab)]
    wg_ref[...] = jnp.concatenate(ev, axis=0).astype(BF16)
    wl_ref[...] = jnp.concatenate(od, axis=0).astype(BF16)


def _split_gu(w_gu):
    E, D, de2 = w_gu.shape
    de = de2 // 2
    rows = E * D
    tr = SPLIT_ROWS if rows % SPLIT_ROWS == 0 else LANES
    wg, wl = pl.pallas_call(
        _split_gu_kernel,
        out_shape=(jax.ShapeDtypeStruct((rows, de), BF16), jax.ShapeDtypeStruct((rows, de), BF16)),
        grid=(rows // tr,),
        in_specs=[pl.BlockSpec((tr, de2), lambda i: (i, 0))],
        out_specs=(pl.BlockSpec((tr, de), lambda i: (i, 0)), pl.BlockSpec((tr, de), lambda i: (i, 0))),
        scratch_shapes=[pltpu.VMEM((tr // LANES, de2, LANES), F32)],
        compiler_params=_cparams(1, 48),
        name="split_gu",
    )(w_gu.reshape(rows, de2))
    return wg.reshape(E, D, de), wl.reshape(E, D, de)


def _moe_kernel(be_ref, nu_ref, x_ref, wg_ref, wl_ref, bg_ref, bl_ref, wd_ref, bd_ref, y_ref):
    i = pl.program_id(0)

    @pl.when(i < nu_ref[0])
    def _():
        x = x_ref[...]
        glu = jnp.minimum(_dot(x, wg_ref[...]) + bg_ref[...], SWIGLU_LIMIT)
        lin = jnp.clip(_dot(x, wl_ref[...]) + bl_ref[...], -SWIGLU_LIMIT, SWIGLU_LIMIT)
        act = glu * _sigmoid(SWIGLU_ALPHA * glu) * (lin + 1.0)
        y_ref[...] = _dot(act.astype(BF16), wd_ref[...]) + bd_ref[...]

    @pl.when(i >= nu_ref[0])
    def _():
        y_ref[...] = jnp.zeros_like(y_ref)


def _moe_experts(block_e, n_used, xg, wg, wl, bg, bl, wd, bd):
    n_rows, D = xg.shape
    E, _, DE = wg.shape
    nblk = n_rows // MOE_ROWS
    grid_spec = pltpu.PrefetchScalarGridSpec(
        num_scalar_prefetch=2,
        grid=(nblk,),
        in_specs=[
            pl.BlockSpec((MOE_ROWS, D), lambda i, be, nu: (i, 0)),
            pl.BlockSpec((None, D, DE), lambda i, be, nu: (be[i], 0, 0)),
            pl.BlockSpec((None, D, DE), lambda i, be, nu: (be[i], 0, 0)),
            pl.BlockSpec((None, 1, DE), lambda i, be, nu: (be[i], 0, 0)),
            pl.BlockSpec((None, 1, DE), lambda i, be, nu: (be[i], 0, 0)),
            pl.BlockSpec((None, DE, D), lambda i, be, nu: (be[i], 0, 0)),
            pl.BlockSpec((None, 1, D), lambda i, be, nu: (be[i], 0, 0)),
        ],
        out_specs=pl.BlockSpec((MOE_ROWS, D), lambda i, be, nu: (i, 0)),
    )
    return pl.pallas_call(
        _moe_kernel,
        out_shape=jax.ShapeDtypeStruct((n_rows, D), F32),
        grid_spec=grid_spec,
        compiler_params=_cparams(1, 56),
        name="moe_experts",
    )(block_e, n_used, xg, wg, wl, bg, bl, wd, bd)


def _route(logits):
    n_tok = logits.shape[0]
    top_val, top_idx = lax.top_k(logits, TOP_K)
    gates = jax.nn.softmax(top_val, axis=-1)
    n_assign = n_tok * TOP_K
    flat_e = top_idx.reshape(-1).astype(jnp.int32)
    flat_tok = jnp.repeat(jnp.arange(n_tok, dtype=jnp.int32), TOP_K)
    order = jnp.argsort(flat_e)
    sorted_e, sorted_tok = flat_e[order], flat_tok[order]
    counts = jnp.sum((flat_e[:, None] == jnp.arange(N_EXPERTS, dtype=jnp.int32)[None, :]).astype(jnp.int32), axis=0)
    starts = jnp.cumsum(counts) - counts
    padded = (counts + MOE_ROWS - 1) // MOE_ROWS * MOE_ROWS
    pends = jnp.cumsum(padded)
    pstarts = pends - padded
    dest_sorted = pstarts[sorted_e] + (jnp.arange(n_assign, dtype=jnp.int32) - starts[sorted_e])
    n_blocks = -(-n_assign // MOE_ROWS) + N_EXPERTS
    n_rows = n_blocks * MOE_ROWS
    block_e = jnp.minimum(jnp.searchsorted(pends, jnp.arange(n_blocks, dtype=jnp.int32) * MOE_ROWS, side='right'),
                          N_EXPERTS - 1).astype(jnp.int32)
    rows = jnp.arange(n_rows, dtype=jnp.int32)
    row_e = jnp.repeat(block_e, MOE_ROWS)
    off = rows - pstarts[row_e]
    src = jnp.clip(starts[row_e] + off, 0, n_assign - 1)
    row_tok = jnp.where(off < counts[row_e], sorted_tok[src], 0)
    dest = jnp.zeros((n_assign,), jnp.int32).at[order].set(dest_sorted)
    n_used = (pends[-1] // MOE_ROWS).astype(jnp.int32).reshape(1)
    return gates, dest.reshape(n_tok, TOP_K), row_tok, block_e, n_used


def _combine_kernel(yg_ref, gate_ref, h_ref, gf_ref, fg_ref, o_ref, *, final):
    gate = gate_ref[...]
    acc = yg_ref[0] * gate[:, 0:1]
    for k in range(1, TOP_K):
        acc = acc + yg_ref[k] * gate[:, k:k + 1]
    hn = h_ref[...] + gf_ref[...] * acc
    if final:
        ms = jnp.mean(hn * hn, axis=-1, keepdims=True)
        hn = hn * lax.rsqrt(ms + NORM_EPS) * fg_ref[...]
    o_ref[...] = hn


def _combine(yg, gates, h, gf, final_g, *, final, tm):
    B, T, D = h.shape
    bm = _bmap(gf)
    kern = functools.partial(_combine_kernel, final=final)
    return pl.pallas_call(
        kern,
        out_shape=jax.ShapeDtypeStruct((B, T, D), F32),
        grid=(B, T // tm),
        in_specs=[
            pl.BlockSpec((TOP_K, None, tm, D), lambda b, s: (0, b, s, 0)),
            pl.BlockSpec((None, tm, TOP_K), lambda b, s: (b, s, 0)),
            pl.BlockSpec((None, tm, D), lambda b, s: (b, s, 0)),
            pl.BlockSpec((None, 1, D), lambda b, s: (bm(b), 0, 0)),
            pl.BlockSpec((1, D), lambda b, s: (0, 0)),
        ],
        out_specs=pl.BlockSpec((None, tm, D), lambda b, s: (b, s, 0)),
        compiler_params=_cparams(2, 48),
        name="moe_combine",
    )(yg, gates, h, gf, final_g)


def _block_diag(w):
    G, c, d = w.shape
    eye = jnp.eye(G, dtype=w.dtype)
    return (eye[:, None, :, None] * w[:, :, None, :]).reshape(G * c, G * d)


def kernel(x, c, ctx, c_ctx, ada_w, ada_b, norm_mix_g, norm_ffn_g, ev_w_in, ev_w_out, pool_w, pool_scale,
           att_sink, od_w_in, od_w_out, rw_mu, rw_w0, rw_w2, rw_a0, rw_a2, rw_g2, rw_k_k, rw_k_a, rw_r_k,
           rw_gn_g, rw_gn_b, router_w, router_b, exp_w_gu, exp_b_gu, exp_w_dn, exp_b_dn, final_g):
    B, S, D = x.shape
    C = ctx.shape[1]
    depth = ada_w.shape[0]
    DE = exp_w_dn.shape[2]

    n_mod = -(-(B + 1) // SUBLANES) * SUBLANES
    cvec = jnp.concatenate([c, c_ctx[None, :], jnp.zeros((n_mod - B - 1, D), F32)], axis=0)
    mods = _ada_all(cvec, ada_w, ada_b)

    cos_t, sin_t = _rope_tables(S)
    dft_x = dft_c = cs64 = None
    if depth > 1:
        dft_x, dft_c, cs64 = _dft_tables(S), _dft_tables(C), _channel_dft()

    tm_x = 512 if S % 512 == 0 else S
    tm_o = 256 if S % 256 == 0 else S

    h, hc = x, ctx
    for i in range(depth):
        last = i == depth - 1
        j = i // 2
        m6 = mods[i].reshape(n_mod, 6, D)
        mx = [m6[:B, k][:, None, :] for k in range(6)]
        mc = [m6[B:B + 1, k][:, None, :] for k in range(6)]
        g_mix = norm_mix_g[i][None, :]
        g_ffn = norm_ffn_g[i][None, :]
        rw = router_w[i]
        rb = router_b[i][None, :]

        if i % 2 == 0:
            assert not last, "an even final layer is not part of this block"
            w_in = ev_w_in[j].astype(BF16)
            w_out = ev_w_out[j].astype(BF16)
            pw_bd = _block_diag(pool_w[j]).astype(BF16)
            ps = pool_scale[j][None, :]
            sink = att_sink[j]
            pool_ux, qx, kvx = _even_in(h, mx[0], mx[1], g_mix, w_in, cos_t, sin_t, rope=True, tm=tm_x)
            pool_uc, qc, kvc = _even_in(hc, mc[0], mc[1], g_mix, w_in, cos_t[:C], sin_t[:C], rope=False, tm=C)
            h, fx_x, lg_x = _even_mix(sink, qx, kvx, kvc, pool_ux, h, mx[2], mx[3], mx[4], g_ffn, w_out,
                                      pw_bd, ps, rw, rb, local=True)
            hc, fx_c, lg_c = _even_mix(sink, qc, None, kvc, pool_uc, hc, mc[2], mc[3], mc[4], g_ffn, w_out,
                                       pw_bd, ps, rw, rb, local=False)
        else:
            w_in = od_w_in[j].astype(BF16)
            w_out = od_w_out[j].astype(BF16)
            od_args = (w_in, rw_mu[j], rw_w0[j], rw_w2[j].astype(BF16), rw_a0[j], rw_a2[j].astype(BF16),
                       rw_k_k[j][None, :], rw_k_a[j][None, :], rw_r_k[j].reshape(1, RWKV_WIDTH))
            rx, vx, kkx, bonus_x, lwx, kdx, bdx, gcx, ux = _odd_in(h, mx[0], mx[1], g_mix, *od_args, tm=tm_o)
            rc, vc, kkc, bonus_c, lwc, kdc, bdc, gcc, uc = _odd_in(hc, mc[0], mc[1], g_mix, *od_args, tm=C)
            s0 = jnp.zeros(_scan_state_shape(B), F32)
            y_c, s_c = _scan(rc, vc, kkc, lwc, kdc, bdc, s0)
            y_x, _ = _scan(rx, vx, kkx, lwx, kdx, bdx, s_c)
            ro_args = (g_ffn, rw_gn_g[j][None, :], rw_gn_b[j][None, :], rw_g2[j].astype(BF16), w_out, rw, rb)
            h, fx_x, lg_x = _odd_out(y_x, bonus_x, gcx, ux, dft_x, cs64, h, mx[2], mx[3], mx[4], *ro_args,
                                     tm=tm_o)
            if not last:
                hc, fx_c, lg_c = _odd_out(y_c, bonus_c, gcc, uc, dft_c, cs64, hc, mc[2], mc[3], mc[4],
                                          *ro_args, tm=C)

        n_x = B * S
        if last:
            fx_all = fx_x.reshape(n_x, D)
            lg_all = lg_x.reshape(n_x, N_EXPERTS)
        else:
            fx_all = jnp.concatenate([fx_x.reshape(n_x, D), fx_c.reshape(B * C, D)], axis=0)
            lg_all = jnp.concatenate([lg_x.reshape(n_x, N_EXPERTS), lg_c.reshape(B * C, N_EXPERTS)], axis=0)
        gates, dest, row_tok, block_e, n_used = _route(lg_all)
        xg = jnp.take(fx_all, row_tok, axis=0)
        wg, wl = _split_gu(exp_w_gu[i])
        bg = exp_b_gu[i][:, None, 0::2]
        bl = exp_b_gu[i][:, None, 1::2]
        wd = exp_w_dn[i].astype(BF16)
        bd = exp_b_dn[i][:, None, :]
        y = _moe_experts(block_e, n_used, xg, wg, wl, bg, bl, wd, bd)
        tm_cx = 256 if S % 256 == 0 else S
        yg_x = jnp.take(y, dest[:n_x].T.reshape(TOP_K, B, S), axis=0)
        h = _combine(yg_x, gates[:n_x].reshape(B, S, TOP_K), h, mx[5], final_g[None, :], final=last, tm=tm_cx)
        if not last:
            yg_c = jnp.take(y, dest[n_x:].T.reshape(TOP_K, B, C), axis=0)
            hc = _combine(yg_c, gates[n_x:].reshape(B, C, TOP_K), hc, mc[5], final_g[None, :], final=False, tm=C)
    return h
```

```python
import functools
import math

import jax
import jax.numpy as jnp
import numpy as np
from jax import lax
from jax.experimental import pallas as pl
from jax.experimental.pallas import tpu as pltpu

F32 = jnp.float32
BF16 = jnp.bfloat16

GRID_W = 64
HEAD_DIM = 64
ROT_FREQS = HEAD_DIM // 4
ROPE_THETA = 10000.0
NORM_EPS = 1e-5
POOL_WINDOWS = (2, 4, 8, 16)
POOL_GROUP = 64
POOL_WIDTH = POOL_GROUP * len(POOL_WINDOWS)
POOL_HALO = max(POOL_WINDOWS) // 2
ATT_HEADS = 12
ATT_KV_HEADS = 3
ATT_GROUP = ATT_HEADS // ATT_KV_HEADS
ATT_WINDOW = 128
ATT_BLOCK = 128
ATT_Q = ATT_HEADS * HEAD_DIM
ATT_KV = ATT_KV_HEADS * HEAD_DIM
EVEN_IN = POOL_WIDTH + ATT_Q + 2 * ATT_KV
RWKV_HEADS = 12
RWKV_WIDTH = RWKV_HEADS * HEAD_DIM
DECAY_LORA = 64
ICLR_LORA = 64
GATE_LORA = 160
GN_EPS = 64e-5
LORA_LO = 3 * RWKV_WIDTH
STATE_HI = LORA_LO + 2 * DECAY_LORA + 2 * ICLR_LORA
RWKV_IN = STATE_HI + GATE_LORA
FNET_GROUPS = 4
FNET_GROUP = 64
FNET_WIDTH = FNET_GROUPS * FNET_GROUP
ODD_IN = RWKV_IN + FNET_WIDTH
N_EXPERTS = 32
TOP_K = 4
SWIGLU_LIMIT = 7.0
SWIGLU_ALPHA = 1.702

LANES = 128
SUBLANES = 8
SCAN_CHUNK = 64
SCAN_GROUP = 4
SPLIT_ROWS = 256
MOE_ROWS = 512
MASK_NEG = -1e30


def _cparams(n_axes, vmem_mb):
    return pltpu.CompilerParams(
        dimension_semantics=("arbitrary",) * n_axes,
        vmem_limit_bytes=vmem_mb * 1024 * 1024,
    )


def _dot(a, b):
    return jnp.dot(a, b, preferred_element_type=F32)


def _dot_nt(a, b):
    return lax.dot_general(a, b, (((1,), (1,)), ((), ())), preferred_element_type=F32)


def _dot_tn(a, b):
    return lax.dot_general(a, b, (((0,), (0,)), ((), ())), preferred_element_type=F32)


def _split(x):
    hi = x.astype(BF16)
    lo = (x - hi.astype(F32)).astype(BF16)
    return hi, lo


def _dot3(a, b):
    ah, al = _split(a)
    bh, bl = _split(b)
    return _dot(ah, bh) + (_dot(al, bh) + _dot(ah, bl))


def _dot_exact_lhs(a_exact, b):
    a16 = a_exact.astype(BF16)
    b1 = b.astype(BF16)
    r1 = b - b1.astype(F32)
    b2 = r1.astype(BF16)
    b3 = (r1 - b2.astype(F32)).astype(BF16)
    return _dot(a16, b1) + (_dot(a16, b2) + _dot(a16, b3))


def _modnorm(x, g, sh, sc):
    ms = jnp.mean(x * x, axis=-1, keepdims=True)
    xn = x * lax.rsqrt(ms + NORM_EPS) * g
    return xn * (1.0 + sc) + sh


def _sigmoid(x):
    return 1.0 / (1.0 + jnp.exp(-x))


def _head_sum(x):
    n = x.shape[1] // LANES
    lane = lax.broadcasted_iota(jnp.int32, (x.shape[0], LANES), 1)
    lo_mask = lane < HEAD_DIM
    parts = []
    for c in range(n):
        xc = x[:, c * LANES:(c + 1) * LANES]
        s_lo = jnp.sum(jnp.where(lo_mask, xc, 0.0), axis=-1, keepdims=True)
        s_hi = jnp.sum(jnp.where(lo_mask, 0.0, xc), axis=-1, keepdims=True)
        parts.append(jnp.where(lo_mask, s_lo, s_hi))
    return jnp.concatenate(parts, axis=1)


def _ada_kernel(c_ref, w_ref, b_ref, o_ref):
    x = c_ref[...]
    x = x * _sigmoid(x)
    o_ref[...] = _dot3(x, w_ref[...]) + b_ref[...]


def _ada_all(cvec, ada_w, ada_b):
    depth, d, n6 = ada_w.shape
    r = cvec.shape[0]
    tn = 1536 if n6 % 1536 == 0 else n6
    return pl.pallas_call(
        _ada_kernel,
        out_shape=jax.ShapeDtypeStruct((depth, r, n6), F32),
        grid=(depth, n6 // tn),
        in_specs=[
            pl.BlockSpec((r, d), lambda i, j: (0, 0)),
            pl.BlockSpec((None, d, tn), lambda i, j: (i, 0, j)),
            pl.BlockSpec((None, 1, tn), lambda i, j: (i, 0, j)),
        ],
        out_specs=pl.BlockSpec((None, r, tn), lambda i, j: (i, 0, j)),
        compiler_params=_cparams(2, 48),
        name="ada_mod",
    )(cvec, ada_w, ada_b.reshape(depth, 1, n6))


def _bmap(arr):
    if arr.shape[0] == 1:
        return lambda b: 0
    return lambda b: b


def _even_in_kernel(h_ref, sh_ref, sc_ref, g_ref, w_ref, cos_ref, sin_ref,
                    pool_ref, q_ref, kv_ref, *, rope):
    a = _modnorm(h_ref[...], g_ref[...], sh_ref[...], sc_ref[...]).astype(BF16)
    px = _dot(a, w_ref[...])
    pool_ref[...] = px[:, :POOL_WIDTH]
    n_chunks = (ATT_Q + 2 * ATT_KV) // LANES
    n_full = (ATT_Q + ATT_KV) // LANES
    outs = []
    if rope:
        lane = lax.broadcasted_iota(jnp.int32, (px.shape[0], LANES), 1)
        first = (lane % (2 * ROT_FREQS)) < ROT_FREQS
    for c in range(n_chunks):
        x = px[:, POOL_WIDTH + c * LANES:POOL_WIDTH + (c + 1) * LANES]
        if rope and c <= n_full:
            t0 = 0 if c < n_full else LANES
            cs = cos_ref[:, t0:t0 + LANES]
            sn = sin_ref[:, t0:t0 + LANES]
            rot = jnp.where(first, pltpu.roll(x, LANES - ROT_FREQS, 1), pltpu.roll(x, ROT_FREQS, 1))
            x = x * cs + rot * sn
        if c < ATT_Q // LANES:
            x = x * (HEAD_DIM ** -0.5)
        outs.append(x.astype(BF16))
    nq = ATT_Q // LANES
    q_ref[...] = jnp.concatenate(outs[:nq], axis=1)
    kv_ref[...] = jnp.concatenate(outs[nq:], axis=1)


def _even_in(h, sh, sc, g, w_bf, cos_t, sin_t, *, rope, tm):
    B, T, D = h.shape
    nt = T // tm
    bm = _bmap(sh)
    kern = functools.partial(_even_in_kernel, rope=rope)
    return pl.pallas_call(
        kern,
        out_shape=(
            jax.ShapeDtypeStruct((B, T, POOL_WIDTH), F32),
            jax.ShapeDtypeStruct((B, T, ATT_Q), BF16),
            jax.ShapeDtypeStruct((B, T, 2 * ATT_KV), BF16),
        ),
        grid=(nt, B),
        in_specs=[
            pl.BlockSpec((None, tm, D), lambda s, b: (b, s, 0)),
            pl.BlockSpec((None, 1, D), lambda s, b: (bm(b), 0, 0)),
            pl.BlockSpec((None, 1, D), lambda s, b: (bm(b), 0, 0)),
            pl.BlockSpec((1, D), lambda s, b: (0, 0)),
            pl.BlockSpec((D, EVEN_IN), lambda s, b: (0, 0)),
            pl.BlockSpec((tm, 2 * LANES), lambda s, b: (s, 0)),
            pl.BlockSpec((tm, 2 * LANES), lambda s, b: (s, 0)),
        ],
        out_specs=(
            pl.BlockSpec((None, tm, POOL_WIDTH), lambda s, b: (b, s, 0)),
            pl.BlockSpec((None, tm, ATT_Q), lambda s, b: (b, s, 0)),
            pl.BlockSpec((None, tm, 2 * ATT_KV), lambda s, b: (b, s, 0)),
        ),
        compiler_params=_cparams(2, 48),
        name="even_in",
    )(h, sh, sc, g, w_bf, cos_t, sin_t)


def _rope_tables(T):
    t = jnp.arange(T, dtype=jnp.int32)
    row = (t // GRID_W).astype(F32)
    col = (t % GRID_W).astype(F32)
    inv_freq = ROPE_THETA ** (-jnp.arange(ROT_FREQS, dtype=F32) / ROT_FREQS)
    ang_r = row[:, None] * inv_freq
    ang_c = col[:, None] * inv_freq
    cos_h = jnp.concatenate([jnp.cos(ang_r)] * 2 + [jnp.cos(ang_c)] * 2, axis=1)
    sin_h = jnp.concatenate([-jnp.sin(ang_r), jnp.sin(ang_r), -jnp.sin(ang_c), jnp.sin(ang_c)], axis=1)
    one = jnp.ones_like(cos_h)
    zero = jnp.zeros_like(sin_h)
    cos_t = jnp.concatenate([cos_h, cos_h, cos_h, one], axis=1)
    sin_t = jnp.concatenate([sin_h, sin_h, sin_h, zero], axis=1)
    return cos_t, sin_t


def _residual_and_router(h, y, gm, gf_norm, shf, scf, rw, rb, hn_ref, fx_ref, lg_ref):
    hn = h + gm * y
    hn_ref[...] = hn
    fx = _modnorm(hn, gf_norm, shf, scf)
    fx_ref[...] = fx.astype(BF16)
    lg_ref[...] = _dot3(fx, rw) + rb


def _even_mix_kernel(*refs, local, tq, T, n_ctx):
    it = iter(refs)
    sink_ref = next(it)
    q_ref = next(it)
    if local:
        kvp_ref, kvc_ref, kvn_ref = next(it), next(it), next(it)
    ckv_ref = next(it)
    up_ref, uc_ref, un_ref = next(it), next(it), next(it)
    h_ref, gm_ref, shf_ref, scf_ref, gfn_ref = next(it), next(it), next(it), next(it), next(it)
    wout_ref, pw_ref, ps_ref, rw_ref, rb_ref = next(it), next(it), next(it), next(it), next(it)
    hn_ref, fx_ref, lg_ref = next(it), next(it), next(it)

    j = pl.program_id(1)
    nb = pl.num_programs(1)

    n_loc = 3 * ATT_BLOCK if local else 0
    n_keys = n_loc + n_ctx
    rows = ATT_GROUP * tq
    if local:
        r_i = lax.broadcasted_iota(jnp.int32, (rows, n_loc), 0)
        c_i = lax.broadcasted_iota(jnp.int32, (rows, n_loc), 1)
        q_pos = j * tq + r_i % tq
        k_pos = (j - 1) * ATT_BLOCK + c_i
        valid = (jnp.abs(q_pos - k_pos) <= ATT_WINDOW) & (k_pos >= 0) & (k_pos < T)
    row_head = lax.broadcasted_iota(jnp.int32, (rows, 1), 0) // tq
    q = q_ref[...]
    ckv = ckv_ref[...]
    if local:
        kvl = jnp.concatenate([kvp_ref[...], kvc_ref[...], kvn_ref[...]], axis=0)
    att = []
    for g in range(ATT_KV_HEADS):
        qs = jnp.concatenate(
            [q[:, (g * ATT_GROUP + i) * HEAD_DIM:(g * ATT_GROUP + i + 1) * HEAD_DIM] for i in range(ATT_GROUP)],
            axis=0)
        kc = ckv[:, g * HEAD_DIM:(g + 1) * HEAD_DIM]
        vc = ckv[:, ATT_KV + g * HEAD_DIM:ATT_KV + (g + 1) * HEAD_DIM]
        s_ctx = _dot_nt(qs, kc)
        sink = jnp.zeros((rows, 1), F32)
        for i in range(ATT_GROUP):
            sink = jnp.where(row_head == i, sink_ref[g * ATT_GROUP + i], sink)
        if local:
            kl = kvl[:, g * HEAD_DIM:(g + 1) * HEAD_DIM]
            vl = kvl[:, ATT_KV + g * HEAD_DIM:ATT_KV + (g + 1) * HEAD_DIM]
            s_loc = jnp.where(valid, _dot_nt(qs, kl), MASK_NEG)
            m = jnp.maximum(jnp.maximum(jnp.max(s_loc, axis=-1, keepdims=True),
                                        jnp.max(s_ctx, axis=-1, keepdims=True)), sink)
            p_loc = jnp.exp(s_loc - m)
            p_ctx = jnp.exp(s_ctx - m)
            den = (jnp.sum(p_loc, axis=-1, keepdims=True) + jnp.sum(p_ctx, axis=-1, keepdims=True)
                   + jnp.exp(sink - m))
            o = _dot(p_loc.astype(BF16), vl) + _dot(p_ctx.astype(BF16), vc)
        else:
            m = jnp.maximum(jnp.max(s_ctx, axis=-1, keepdims=True), sink)
            p_ctx = jnp.exp(s_ctx - m)
            den = jnp.sum(p_ctx, axis=-1, keepdims=True) + jnp.exp(sink - m)
            o = _dot(p_ctx.astype(BF16), vc)
        o = o / den
        for i in range(ATT_GROUP):
            att.append(o[i * tq:(i + 1) * tq])
    att_x = jnp.concatenate(att, axis=1)

    u = uc_ref[...]
    up = jnp.where(j > 0, up_ref[...], 0.0)
    un = jnp.where(j < nb - 1, un_ref[...], 0.0)
    e = jnp.concatenate([up, u, un], axis=0)
    n_e = tq + 2 * POOL_HALO

    def shifted(x, k):
        return pltpu.roll(x, k % n_e, 0)

    a1 = e + shifted(e, 1)
    a2 = shifted(a1, 1) + shifted(a1, -1)
    a3 = shifted(a2, 2) + shifted(a2, -2)
    a4 = shifted(a3, 4) + shifted(a3, -4)
    lane = lax.broadcasted_iota(jnp.int32, (n_e, POOL_WIDTH), 1)
    grp = lane // POOL_GROUP
    win = jnp.where(grp == 0, a1, jnp.where(grp == 1, a2, jnp.where(grp == 2, a3, a4)))
    win = win[POOL_HALO:POOL_HALO + tq]
    t_i = j * tq + lax.broadcasted_iota(jnp.int32, (tq, POOL_WIDTH), 0)
    half = jnp.left_shift(1, lax.broadcasted_iota(jnp.int32, (tq, POOL_WIDTH), 1) // POOL_GROUP)
    cnt = (jnp.minimum(t_i + half, T) - jnp.maximum(t_i - half, 0)).astype(F32)
    pooled = win / cnt - u
    pool_x = _dot(pooled.astype(BF16), pw_ref[...]) * ps_ref[...]

    mix = jnp.concatenate([pool_x, att_x], axis=1).astype(BF16)
    y = _dot(mix, wout_ref[...])
    _residual_and_router(h_ref[...], y, gm_ref[...], gfn_ref[...], shf_ref[...], scf_ref[...],
                         rw_ref[...], rb_ref[...], hn_ref, fx_ref, lg_ref)


def _even_mix(sink, q, kv, ckv, pool_u, h, gm, shf, scf, gfn, wout, pw_bd, pscale, rw, rb, *, local):
    B, T, D = h.shape
    n_ctx = ckv.shape[1]
    tq = ATT_BLOCK if local else T
    nb = T // tq
    r8 = tq // SUBLANES
    n8 = T // SUBLANES
    bm = _bmap(gm)
    kern = functools.partial(_even_mix_kernel, local=local, tq=tq, T=T, n_ctx=n_ctx)
    vec = lambda: pl.BlockSpec((None, 1, D), lambda b, j: (bm(b), 0, 0))
    full = lambda a: pl.BlockSpec(a.shape, lambda b, j: (0,) * a.ndim)
    in_specs = [pl.BlockSpec(memory_space=pltpu.SMEM),
                pl.BlockSpec((None, tq, ATT_Q), lambda b, j: (b, j, 0))]
    args = [sink, q]
    if local:
        in_specs += [
            pl.BlockSpec((None, ATT_BLOCK, 2 * ATT_KV), lambda b, j: (b, jnp.maximum(j - 1, 0), 0)),
            pl.BlockSpec((None, ATT_BLOCK, 2 * ATT_KV), lambda b, j: (b, j, 0)),
            pl.BlockSpec((None, ATT_BLOCK, 2 * ATT_KV), lambda b, j: (b, jnp.minimum(j + 1, nb - 1), 0)),
        ]
        args += [kv, kv, kv]
    in_specs += [
        pl.BlockSpec((None, n_ctx, 2 * ATT_KV), lambda b, j: (b, 0, 0)),
        pl.BlockSpec((None, SUBLANES, POOL_WIDTH), lambda b, j: (b, jnp.maximum(j * r8 - 1, 0), 0)),
        pl.BlockSpec((None, tq, POOL_WIDTH), lambda b, j: (b, j, 0)),
        pl.BlockSpec((None, SUBLANES, POOL_WIDTH), lambda b, j: (b, jnp.minimum((j + 1) * r8, n8 - 1), 0)),
        pl.BlockSpec((None, tq, D), lambda b, j: (b, j, 0)),
        vec(), vec(), vec(), full(gfn), full(wout), full(pw_bd), full(pscale), full(rw), full(rb),
    ]
    args += [ckv, pool_u, pool_u, pool_u, h, gm, shf, scf, gfn, wout, pw_bd, pscale, rw, rb]
    return pl.pallas_call(
        kern,
        out_shape=(
            jax.ShapeDtypeStruct((B, T, D), F32),
            jax.ShapeDtypeStruct((B, T, D), BF16),
            jax.ShapeDtypeStruct((B, T, N_EXPERTS), F32),
        ),
        grid=(B, nb),
        in_specs=in_specs,
        out_specs=(
            pl.BlockSpec((None, tq, D), lambda b, j: (b, j, 0)),
            pl.BlockSpec((None, tq, D), lambda b, j: (b, j, 0)),
            pl.BlockSpec((None, tq, N_EXPERTS), lambda b, j: (b, j, 0)),
        ),
        compiler_params=_cparams(2, 48),
        name="even_mix_local" if local else "even_mix_ctx",
    )(*args)


def _odd_in_kernel(h_ref, hp_ref, hn_ref, sh_ref, sc_ref, g_ref, w_ref, mu_ref, w0_ref, w2_ref,
                   a0_ref, a2_ref, kk_ref, ka_ref, rk_ref,
                   r_out, v_out, kkn_out, bonus_out, lw_out, kd_out, bd_out, gc_out, fn_out, *, tm):
    s = pl.program_id(1)
    ns = pl.num_programs(1)
    g, sh, sc = g_ref[...], sh_ref[...], sc_ref[...]
    w = w_ref[...]
    a = _modnorm(h_ref[...], g, sh, sc).astype(BF16)
    px = _dot(a, w)
    fn_out[...] = px[:, RWKV_IN:]
    main = px[:, :RWKV_IN]
    ap = _modnorm(hp_ref[...], g, sh, sc).astype(BF16)
    an = _modnorm(hn_ref[...], g, sh, sc).astype(BF16)
    halo = _dot(jnp.concatenate([ap, an], axis=0), w[:, :RWKV_IN])
    prev_row = jnp.where(s > 0, halo[SUBLANES - 1:SUBLANES], 0.0)
    next_row = jnp.where(s < ns - 1, halo[SUBLANES:SUBLANES + 1], 0.0)
    row = lax.broadcasted_iota(jnp.int32, (tm, 1), 0)
    prev = jnp.where(row == 0, prev_row, pltpu.roll(main, 1, 0))
    nxt = jnp.where(row == tm - 1, next_row, pltpu.roll(main, tm - 1, 0))
    mu = mu_ref[...]
    fs = main + mu[0:1] * (prev - main) + mu[1:2] * (nxt - main)

    W = RWKV_WIDTH
    r = fs[:, :W]
    k = fs[:, W:2 * W]
    v = fs[:, 2 * W:3 * W]
    lora = fs[:, LORA_LO:STATE_HI]
    gc_out[...] = fs[:, STATE_HI:RWKV_IN]
    r_out[...] = r
    v_out[...] = v

    kx = k * kk_ref[...]
    nrm = jnp.sqrt(_head_sum(kx * kx))
    kkn = kx / jnp.maximum(nrm, 1e-12)
    kkn_out[...] = kkn
    ka = ka_ref[...]
    ksum = None
    for d in range(2):
        wd = lora[:, d * DECAY_LORA:(d + 1) * DECAY_LORA]
        o_a = 2 * DECAY_LORA
        ad = lora[:, o_a + d * ICLR_LORA:o_a + (d + 1) * ICLR_LORA]
        xw = _dot(jnp.tanh(wd).astype(BF16), w2_ref[d]) + w0_ref[d:d + 1]
        z = -xw
        softplus = jnp.maximum(z, 0.0) + jnp.log(1.0 + jnp.exp(-jnp.abs(z)))
        w_log = -softplus - 0.5
        lw_out[d] = -jnp.exp(w_log)
        xa = _dot(ad.astype(BF16), a2_ref[d]) + a0_ref[d:d + 1]
        a_d = _sigmoid(xa)
        k_d = k * (1.0 + (a_d - 1.0) * ka)
        kd_out[d] = k_d
        bd_out[d] = kkn * a_d
        ksum = k_d if ksum is None else ksum + k_d
    coef = _head_sum(r * ksum * rk_ref[...])
    bonus_out[...] = coef * v


def _odd_in(h, sh, sc, g, w_bf, mu, w0, w2_bf, a0, a2_bf, k_k, k_a, r_k, *, tm):
    B, T, D = h.shape
    ns = T // tm
    r8 = tm // SUBLANES
    n8 = T // SUBLANES
    bm = _bmap(sh)
    W = RWKV_WIDTH
    full = lambda a: pl.BlockSpec(a.shape, lambda b, s: (0,) * a.ndim)
    tok = lambda n: pl.BlockSpec((None, tm, n), lambda b, s: (b, s, 0))
    tok2 = lambda n: pl.BlockSpec((2, None, tm, n), lambda b, s: (0, b, s, 0))
    kern = functools.partial(_odd_in_kernel, tm=tm)
    return pl.pallas_call(
        kern,
        out_shape=(
            jax.ShapeDtypeStruct((B, T, W), F32),
            jax.ShapeDtypeStruct((B, T, W), F32),
            jax.ShapeDtypeStruct((B, T, W), F32),
            jax.ShapeDtypeStruct((B, T, W), F32),
            jax.ShapeDtypeStruct((2, B, T, W), F32),
            jax.ShapeDtypeStruct((2, B, T, W), F32),
            jax.ShapeDtypeStruct((2, B, T, W), F32),
            jax.ShapeDtypeStruct((B, T, GATE_LORA), F32),
            jax.ShapeDtypeStruct((B, T, FNET_WIDTH), F32),
        ),
        grid=(B, ns),
        in_specs=[
            pl.BlockSpec((None, tm, D), lambda b, s: (b, s, 0)),
            pl.BlockSpec((None, SUBLANES, D), lambda b, s: (b, jnp.maximum(s * r8 - 1, 0), 0)),
            pl.BlockSpec((None, SUBLANES, D), lambda b, s: (b, jnp.minimum((s + 1) * r8, n8 - 1), 0)),
            pl.BlockSpec((None, 1, D), lambda b, s: (bm(b), 0, 0)),
            pl.BlockSpec((None, 1, D), lambda b, s: (bm(b), 0, 0)),
            full(g), full(w_bf), full(mu), full(w0), full(w2_bf), full(a0), full(a2_bf),
            full(k_k), full(k_a), full(r_k),
        ],
        out_specs=(tok(W), tok(W), tok(W), tok(W), tok2(W), tok2(W), tok2(W), tok(GATE_LORA), tok(FNET_WIDTH)),
        compiler_params=_cparams(2, 60),
        name="odd_in",
    )(h, h, h, sh, sc, g, w_bf, mu, w0, w2_bf, a0, a2_bf, k_k, k_a, r_k)


SCAN_LEVELS = tuple(2 ** k for k in range(int(math.log2(SCAN_CHUNK))))
MASK_BEFORE, MASK_UPTO, MASK_LEVEL0 = 0, 1, 2


def _scan_masks():
    R = SCAN_GROUP * SCAN_CHUNK
    idx = np.arange(R)
    same = (idx[:, None] // SCAN_CHUNK) == (idx[None, :] // SCAN_CHUNK)
    t = (idx % SCAN_CHUNK)[:, None]
    s = (idx % SCAN_CHUNK)[None, :]
    out = []
    for sign in (1, -1):
        order = (t - s) * sign
        ms = [same & (order > 0), same & (order >= 0)]
        for m in SCAN_LEVELS:
            ms.append(same & (t // (2 * m) == s // (2 * m)) & ((((t // m) % 2) - ((s // m) % 2)) * sign == 1))
        out.append(np.stack(ms))
    return jnp.asarray(np.stack(out), F32), jnp.asarray(same, F32)


def _scan_kernel(rf_ref, vf_ref, kkf_ref, rb_ref, vb_ref, kkb_ref, lwf_ref, kdf_ref, bdf_ref,
                 lwb_ref, kdb_ref, bdb_ref, mask_ref, hb_ref, s0_ref, yf_ref, yb_ref, sT_ref, s_scr, *, L):
    c = pl.program_id(1)
    nc = pl.num_programs(1)
    R = SCAN_GROUP * L
    GW = SCAN_GROUP * HEAD_DIM

    @pl.when(c == 0)
    def _():
        s_scr[...] = s0_ref[...]

    head_blk = hb_ref[...]

    def wide(x):
        return (jnp.concatenate([x] * SCAN_GROUP, axis=0) * head_blk).astype(BF16)

    dirs = ((0, (rf_ref, vf_ref, kkf_ref, lwf_ref, kdf_ref, bdf_ref), yf_ref),
            (1, (rb_ref, vb_ref, kkb_ref, lwb_ref, kdb_ref, bdb_ref), yb_ref))
    for d, (r_ref, v_ref, kk_ref, lw_ref, kd_ref, bd_ref), y_ref in dirs:
        before = mask_ref[d, MASK_BEFORE]
        upto = mask_ref[d, MASK_UPTO]
        lw = lw_ref[...]
        c_in = _dot_exact_lhs(upto[:L, :L], lw)
        c_ex = c_in - lw
        tot = jnp.sum(lw, axis=0, keepdims=True)
        kd = kd_ref[...]
        bd = bd_ref[...]
        inv = jnp.exp(-c_in)
        end = jnp.exp(tot - c_in)
        g_tot = jnp.exp(tot)
        cols = dict(
            kt=kk_ref[...] * jnp.exp(c_ex), rt=r_ref[...] * jnp.exp(c_in), kh=kd * inv, bh=bd * inv,
            v=v_ref[...], ke=kd * end, be=bd * end)
        ys = []
        for g in range(RWKV_HEADS // SCAN_GROUP):
            sl = slice(g * GW, (g + 1) * GW)
            w = {k: wide(a[:, sl]) for k, a in cols.items()}
            lhs = jnp.concatenate([w["kt"], w["rt"]], axis=0)
            s4 = _dot_nt(lhs, jnp.concatenate([w["kh"], w["bh"]], axis=0))
            a_kk = (s4[:R, :R] * before).astype(BF16)
            m = s4[:R, R:] * before
            a_rk = (s4[R:, :R] * upto).astype(BF16)
            a_rb = (s4[R:, R:] * upto).astype(BF16)
            eye = upto - before
            dinv = eye - m * mask_ref[d, MASK_LEVEL0]
            for li in range(1, len(SCAN_LEVELS)):
                d16 = dinv.astype(BF16)
                e16 = (m * mask_ref[d, MASK_LEVEL0 + li]).astype(BF16)
                dinv = dinv - _dot(_dot(d16, e16).astype(BF16), d16)
            s_g = s_scr[d, g]
            ks = _dot_nt(lhs, s_g.astype(BF16))
            u16 = _dot(dinv.astype(BF16), (ks[:R] + _dot(a_kk, w["v"])).astype(BF16)).astype(BF16)
            yw = ks[R:] + _dot(a_rk, w["v"]) - _dot(a_rb, u16)
            y_g = yw[0:L]
            for hh in range(1, SCAN_GROUP):
                y_g = y_g + yw[hh * L:(hh + 1) * L]
            ys.append(y_g)
            vu = jnp.concatenate([w["v"], -u16], axis=0)
            ke = jnp.concatenate([w["ke"], w["be"]], axis=0)
            s_scr[d, g] = s_g * g_tot[:, sl] + _dot_tn(vu, ke)
        y_ref[...] = jnp.concatenate(ys, axis=1)

    @pl.when(c == nc - 1)
    def _():
        sT_ref[...] = s_scr[...]


def _scan_state_shape(B):
    gw = SCAN_GROUP * HEAD_DIM
    return (2, B, RWKV_HEADS // SCAN_GROUP, gw, gw)


def _scan(r, v, kk, lw, kd, bd, s0, masks, head_blk):
    B, T, W = r.shape
    L = SCAN_CHUNK
    assert L == HEAD_DIM and T % L == 0
    nc = T // L
    st_shape = _scan_state_shape(B)
    fwd = pl.BlockSpec((None, L, W), lambda b, c: (b, c, 0))
    bwd = pl.BlockSpec((None, L, W), lambda b, c: (b, nc - 1 - c, 0))
    fwd_d = pl.BlockSpec((None, None, L, W), lambda b, c: (0, b, c, 0))
    bwd_d = pl.BlockSpec((None, None, L, W), lambda b, c: (1, b, nc - 1 - c, 0))
    state = pl.BlockSpec((2, None) + st_shape[2:], lambda b, c: (0, b, 0, 0, 0))
    full = lambda a: pl.BlockSpec(a.shape, lambda b, c: (0,) * a.ndim)
    kern = functools.partial(_scan_kernel, L=L)
    return pl.pallas_call(
        kern,
        out_shape=(
            jax.ShapeDtypeStruct((B, T, W), F32),
            jax.ShapeDtypeStruct((B, T, W), F32),
            jax.ShapeDtypeStruct(st_shape, F32),
        ),
        grid=(B, nc),
        in_specs=[fwd, fwd, fwd, bwd, bwd, bwd, fwd_d, fwd_d, fwd_d, bwd_d, bwd_d, bwd_d,
                  full(masks), full(head_blk), state],
        out_specs=(fwd, bwd, state),
        scratch_shapes=[pltpu.VMEM((2,) + st_shape[2:], F32)],
        compiler_params=_cparams(2, 56),
        name="rwkv_scan",
    )(r, v, kk, r, v, kk, lw, kd, bd, lw, kd, bd, masks, head_blk, s0)


def _odd_out_kernel(yf_ref, yb_ref, bonus_ref, gc_ref, u_ref, dft_ref, cs_ref, h_ref, gm_ref, shf_ref, scf_ref,
                    gfn_ref, gng_ref, gnb_ref, g2_ref, wout_ref, rw_ref, rb_ref,
                    hn_ref, fx_ref, lg_ref, ucs_scr, *, T):
    s = pl.program_id(1)

    @pl.when(s == 0)
    def _():
        t = _dot(u_ref[...].astype(BF16), cs_ref[...])
        ucs_scr[0:T, :] = t[:, :FNET_WIDTH].astype(BF16)
        ucs_scr[T:2 * T, :] = t[:, FNET_WIDTH:].astype(BF16)

    f = _dot(dft_ref[...], ucs_scr[...])
    y = yf_ref[...] + yb_ref[...]
    mean = _head_sum(y) * (1.0 / HEAD_DIM)
    yc = y - mean
    var = _head_sum(yc * yc) * (1.0 / HEAD_DIM)
    yn = yc * lax.rsqrt(var + GN_EPS) * gng_ref[...] + gnb_ref[...]
    out = yn + bonus_ref[...]
    gate = _dot(_sigmoid(gc_ref[...]).astype(BF16), g2_ref[...])
    o = out * gate
    mix = jnp.concatenate([o, f], axis=1).astype(BF16)
    ymix = _dot(mix, wout_ref[...])
    _residual_and_router(h_ref[...], ymix, gm_ref[...], gfn_ref[...], shf_ref[...], scf_ref[...],
                         rw_ref[...], rb_ref[...], hn_ref, fx_ref, lg_ref)


def _odd_out(yf, yb, bonus, gc, u, dft, cs64, h, gm, shf, scf, gfn, gn_g, gn_b, g2_bf, wout, rw, rb, *, tm):
    B, T, D = h.shape
    ns = T // tm
    W = RWKV_WIDTH
    bm = _bmap(gm)
    vec = lambda: pl.BlockSpec((None, 1, D), lambda b, s: (bm(b), 0, 0))
    full = lambda a: pl.BlockSpec(a.shape, lambda b, s: (0,) * a.ndim)
    kern = functools.partial(_odd_out_kernel, T=T)
    return pl.pallas_call(
        kern,
        out_shape=(
            jax.ShapeDtypeStruct((B, T, D), F32),
            jax.ShapeDtypeStruct((B, T, D), BF16),
            jax.ShapeDtypeStruct((B, T, N_EXPERTS), F32),
        ),
        grid=(B, ns),
        in_specs=[
            pl.BlockSpec((None, tm, W), lambda b, s: (b, s, 0)),
            pl.BlockSpec((None, tm, W), lambda b, s: (b, s, 0)),
            pl.BlockSpec((None, tm, W), lambda b, s: (b, s, 0)),
            pl.BlockSpec((None, tm, GATE_LORA), lambda b, s: (b, s, 0)),
            pl.BlockSpec((None, T, FNET_WIDTH), lambda b, s: (b, 0, 0)),
            pl.BlockSpec((tm, 2 * T), lambda b, s: (s, 0)),
            full(cs64),
            pl.BlockSpec((None, tm, D), lambda b, s: (b, s, 0)),
            vec(), vec(), vec(), full(gfn), full(gn_g), full(gn_b), full(g2_bf), full(wout), full(rw), full(rb),
        ],
        out_specs=(
            pl.BlockSpec((None, tm, D), lambda b, s: (b, s, 0)),
            pl.BlockSpec((None, tm, D), lambda b, s: (b, s, 0)),
            pl.BlockSpec((None, tm, N_EXPERTS), lambda b, s: (b, s, 0)),
        ),
        scratch_shapes=[pltpu.VMEM((2 * T, FNET_WIDTH), BF16)],
        compiler_params=_cparams(2, 48),
        name="odd_out",
    )(yf, yb, bonus, gc, u, dft, cs64, h, gm, shf, scf, gfn, gn_g, gn_b, g2_bf, wout, rw, rb)


def _dft_tables(T):
    t = jnp.arange(T, dtype=jnp.int32)
    ang = ((t[:, None] * t[None, :]) % T).astype(F32) * (2.0 * math.pi / T)
    scale = 1.0 / math.sqrt(T * FNET_GROUP)
    return (jnp.concatenate([jnp.cos(ang), -jnp.sin(ang)], axis=1) * scale).astype(BF16)


def _channel_dft():
    c = jnp.arange(FNET_GROUP, dtype=jnp.int32)
    ang = ((c[:, None] * c[None, :]) % FNET_GROUP).astype(F32) * (2.0 * math.pi / FNET_GROUP)
    eye = jnp.eye(FNET_GROUPS, dtype=F32)
    return jnp.concatenate([jnp.kron(eye, jnp.cos(ang)), jnp.kron(eye, jnp.sin(ang))], axis=1).astype(BF16)


def _split_gu_kernel(w_ref, wg_ref, wl_ref, t_scr):
    n_slab = w_ref.shape[0] // LANES
    de = w_ref.shape[1] // 2
    for s in range(n_slab):
        t_scr[s] = w_ref[s * LANES:(s + 1) * LANES, :].T
    ev = [t_scr[s, pl.ds(0, de, stride=2), :].T for s in range(n_slab)]
    od = [t_scr[s, pl.ds(1, de, stride=2), :].T for s in range(n_slab)]
    wg_ref[...] = jnp.concatenate(ev, axis=0).astype(BF16)
    wl_ref[...] = jnp.concatenate(od, axis=0).astype(BF16)


def _split_gu(w_gu):
    E, D, de2 = w_gu.shape
    de = de2 // 2
    rows = E * D
    tr = SPLIT_ROWS if rows % SPLIT_ROWS == 0 else LANES
    wg, wl = pl.pallas_call(
        _split_gu_kernel,
        out_shape=(jax.ShapeDtypeStruct((rows, de), BF16), jax.ShapeDtypeStruct((rows, de), BF16)),
        grid=(rows // tr,),
        in_specs=[pl.BlockSpec((tr, de2), lambda i: (i, 0))],
        out_specs=(pl.BlockSpec((tr, de), lambda i: (i, 0)), pl.BlockSpec((tr, de), lambda i: (i, 0))),
        scratch_shapes=[pltpu.VMEM((tr // LANES, de2, LANES), F32)],
        compiler_params=_cparams(1, 48),
        name="split_gu",
    )(w_gu.reshape(rows, de2))
    return wg.reshape(E, D, de), wl.reshape(E, D, de)


def _moe_kernel(vb_ref, ve_ref, vlo_ref, nv_ref, x_ref, wg_ref, wl_ref, bg_ref, bl_ref, wd_ref, bd_ref, y_ref):
    v = pl.program_id(0)

    @pl.when(v < nv_ref[0])
    def _():
        x = x_ref[...]
        glu = jnp.minimum(_dot(x, wg_ref[...]) + bg_ref[...], SWIGLU_LIMIT)
        lin = jnp.clip(_dot(x, wl_ref[...]) + bl_ref[...], -SWIGLU_LIMIT, SWIGLU_LIMIT)
        act = glu * _sigmoid(SWIGLU_ALPHA * glu) * (lin + 1.0)
        y = _dot(act.astype(BF16), wd_ref[...]) + bd_ref[...]
        lo = vlo_ref[v]

        @pl.when(lo == 0)
        def _():
            y_ref[...] = y

        @pl.when(lo > 0)
        def _():
            row = lax.broadcasted_iota(jnp.int32, y.shape, 0)
            y_ref[...] = jnp.where(row >= lo, y, y_ref[...])


def _moe_experts(vis_blk, vis_e, vis_lo, n_vis, xg, wg, wl, bg, bl, wd, bd):
    n_rows, D = xg.shape
    E, _, DE = wg.shape
    n_visits = vis_blk.shape[0]
    wmap = lambda v, vb, ve, vlo, nv: (ve[v], 0, 0)
    xmap = lambda v, vb, ve, vlo, nv: (vb[v], 0)
    grid_spec = pltpu.PrefetchScalarGridSpec(
        num_scalar_prefetch=4,
        grid=(n_visits,),
        in_specs=[
            pl.BlockSpec((MOE_ROWS, D), xmap),
            pl.BlockSpec((None, D, DE), wmap),
            pl.BlockSpec((None, D, DE), wmap),
            pl.BlockSpec((None, 1, DE), wmap),
            pl.BlockSpec((None, 1, DE), wmap),
            pl.BlockSpec((None, DE, D), wmap),
            pl.BlockSpec((None, 1, D), wmap),
        ],
        out_specs=pl.BlockSpec((MOE_ROWS, D), xmap),
    )
    return pl.pallas_call(
        _moe_kernel,
        out_shape=jax.ShapeDtypeStruct((n_rows, D), F32),
        grid_spec=grid_spec,
        compiler_params=_cparams(1, 56),
        name="moe_experts",
    )(vis_blk, vis_e, vis_lo, n_vis, xg, wg, wl, bg, bl, wd, bd)


def _route(logits):
    n_tok = logits.shape[0]
    top_val, top_idx = lax.top_k(logits, TOP_K)
    gates = jax.nn.softmax(top_val, axis=-1)
    n_assign = n_tok * TOP_K
    assert n_assign % MOE_ROWS == 0
    n_blocks = n_assign // MOE_ROWS
    flat_e = top_idx.reshape(-1).astype(jnp.int32)
    iota = jnp.arange(n_assign, dtype=jnp.int32)
    sorted_e, order = lax.sort((flat_e, iota), num_keys=1)
    _, dest = lax.sort((order, iota), num_keys=1)
    sorted_tok = order // TOP_K
    experts = jnp.arange(N_EXPERTS + 1, dtype=jnp.int32)
    bounds = jnp.sum((sorted_e[None, :] < experts[:, None]).astype(jnp.int32), axis=1)
    starts, ends = bounds[:-1], bounds[1:]
    first_blk = starts // MOE_ROWS
    n_vis_e = jnp.where(ends > starts, (ends - 1) // MOE_ROWS - first_blk + 1, 0)
    vis_end = jnp.cumsum(n_vis_e)
    vis_start = vis_end - n_vis_e
    n_vis = vis_end[-1]
    n_visits = n_blocks + N_EXPERTS - 1
    v = jnp.minimum(jnp.arange(n_visits, dtype=jnp.int32), n_vis - 1)
    vis_e = jnp.sum((vis_end[None, :] <= v[:, None]).astype(jnp.int32), axis=1)
    vis_blk = first_blk[vis_e] + (v - vis_start[vis_e])
    vis_lo = jnp.maximum(starts[vis_e] - vis_blk * MOE_ROWS, 0)
    return (gates, dest.reshape(n_tok, TOP_K), sorted_tok, vis_blk.astype(jnp.int32), vis_e.astype(jnp.int32),
            vis_lo.astype(jnp.int32), n_vis.astype(jnp.int32).reshape(1))


def _combine_kernel(yg_ref, gate_ref, h_ref, gf_ref, fg_ref, o_ref, *, final):
    gate = gate_ref[...]
    acc = yg_ref[0] * gate[:, 0:1]
    for k in range(1, TOP_K):
        acc = acc + yg_ref[k] * gate[:, k:k + 1]
    hn = h_ref[...] + gf_ref[...] * acc
    if final:
        ms = jnp.mean(hn * hn, axis=-1, keepdims=True)
        hn = hn * lax.rsqrt(ms + NORM_EPS) * fg_ref[...]
    o_ref[...] = hn


def _combine(yg, gates, h, gf, final_g, *, final, tm):
    B, T, D = h.shape
    bm = _bmap(gf)
    kern = functools.partial(_combine_kernel, final=final)
    return pl.pallas_call(
        kern,
        out_shape=jax.ShapeDtypeStruct((B, T, D), F32),
        grid=(B, T // tm),
        in_specs=[
            pl.BlockSpec((TOP_K, None, tm, D), lambda b, s: (0, b, s, 0)),
            pl.BlockSpec((None, tm, TOP_K), lambda b, s: (b, s, 0)),
            pl.BlockSpec((None, tm, D), lambda b, s: (b, s, 0)),
            pl.BlockSpec((None, 1, D), lambda b, s: (bm(b), 0, 0)),
            pl.BlockSpec((1, D), lambda b, s: (0, 0)),
        ],
        out_specs=pl.BlockSpec((None, tm, D), lambda b, s: (b, s, 0)),
        compiler_params=_cparams(2, 48),
        name="moe_combine",
    )(yg, gates, h, gf, final_g)


def _block_diag(w):
    G, c, d = w.shape
    eye = jnp.eye(G, dtype=w.dtype)
    return (eye[:, None, :, None] * w[:, :, None, :]).reshape(G * c, G * d)


def kernel(x, c, ctx, c_ctx, ada_w, ada_b, norm_mix_g, norm_ffn_g, ev_w_in, ev_w_out, pool_w, pool_scale,
           att_sink, od_w_in, od_w_out, rw_mu, rw_w0, rw_w2, rw_a0, rw_a2, rw_g2, rw_k_k, rw_k_a, rw_r_k,
           rw_gn_g, rw_gn_b, router_w, router_b, exp_w_gu, exp_b_gu, exp_w_dn, exp_b_dn, final_g):
    B, S, D = x.shape
    C = ctx.shape[1]
    depth = ada_w.shape[0]
    DE = exp_w_dn.shape[2]

    n_mod = -(-(B + 1) // SUBLANES) * SUBLANES
    cvec = jnp.concatenate([c, c_ctx[None, :], jnp.zeros((n_mod - B - 1, D), F32)], axis=0)
    mods = _ada_all(cvec, ada_w, ada_b)

    cos_t, sin_t = _rope_tables(S)
    dft_x = dft_c = cs64 = scan_masks = head_blk = None
    if depth > 1:
        dft_x, dft_c, cs64 = _dft_tables(S), _dft_tables(C), _channel_dft()
        scan_masks, head_blk = _scan_masks()

    tm_x = 512 if S % 512 == 0 else S
    tm_o = 256 if S % 256 == 0 else S

    h, hc = x, ctx
    for i in range(depth):
        last = i == depth - 1
        j = i // 2
        m6 = mods[i].reshape(n_mod, 6, D)
        mx = [m6[:B, k][:, None, :] for k in range(6)]
        mc = [m6[B:B + 1, k][:, None, :] for k in range(6)]
        g_mix = norm_mix_g[i][None, :]
        g_ffn = norm_ffn_g[i][None, :]
        rw = router_w[i]
        rb = router_b[i][None, :]

        if i % 2 == 0:
            assert not last, "an even final layer is not part of this block"
            w_in = ev_w_in[j].astype(BF16)
            w_out = ev_w_out[j].astype(BF16)
            pw_bd = _block_diag(pool_w[j]).astype(BF16)
            ps = pool_scale[j][None, :]
            sink = att_sink[j]
            pool_ux, qx, kvx = _even_in(h, mx[0], mx[1], g_mix, w_in, cos_t, sin_t, rope=True, tm=tm_x)
            pool_uc, qc, kvc = _even_in(hc, mc[0], mc[1], g_mix, w_in, cos_t[:C], sin_t[:C], rope=False, tm=C)
            h, fx_x, lg_x = _even_mix(sink, qx, kvx, kvc, pool_ux, h, mx[2], mx[3], mx[4], g_ffn, w_out,
                                      pw_bd, ps, rw, rb, local=True)
            hc, fx_c, lg_c = _even_mix(sink, qc, None, kvc, pool_uc, hc, mc[2], mc[3], mc[4], g_ffn, w_out,
                                       pw_bd, ps, rw, rb, local=False)
        else:
            w_in = od_w_in[j].astype(BF16)
            w_out = od_w_out[j].astype(BF16)
            od_args = (w_in, rw_mu[j], rw_w0[j], rw_w2[j].astype(BF16), rw_a0[j], rw_a2[j].astype(BF16),
                       rw_k_k[j][None, :], rw_k_a[j][None, :], rw_r_k[j].reshape(1, RWKV_WIDTH))
            rx, vx, kkx, bonus_x, lwx, kdx, bdx, gcx, ux = _odd_in(h, mx[0], mx[1], g_mix, *od_args, tm=tm_o)
            rc, vc, kkc, bonus_c, lwc, kdc, bdc, gcc, uc = _odd_in(hc, mc[0], mc[1], g_mix, *od_args, tm=C)
            s0 = jnp.zeros(_scan_state_shape(B), F32)
            yf_c, yb_c, s_c = _scan(rc, vc, kkc, lwc, kdc, bdc, s0, scan_masks, head_blk)
            yf_x, yb_x, _ = _scan(rx, vx, kkx, lwx, kdx, bdx, s_c, scan_masks, head_blk)
            ro_args = (g_ffn, rw_gn_g[j][None, :], rw_gn_b[j][None, :], rw_g2[j].astype(BF16), w_out, rw, rb)
            h, fx_x, lg_x = _odd_out(yf_x, yb_x, bonus_x, gcx, ux, dft_x, cs64, h, mx[2], mx[3], mx[4], *ro_args,
                                     tm=tm_o)
            if not last:
                hc, fx_c, lg_c = _odd_out(yf_c, yb_c, bonus_c, gcc, uc, dft_c, cs64, hc, mc[2], mc[3], mc[4],
                                          *ro_args, tm=C)

        n_x = B * S
        if last:
            fx_all = fx_x.reshape(n_x, D)
            lg_all = lg_x.reshape(n_x, N_EXPERTS)
        else:
            fx_all = jnp.concatenate([fx_x.reshape(n_x, D), fx_c.reshape(B * C, D)], axis=0)
            lg_all = jnp.concatenate([lg_x.reshape(n_x, N_EXPERTS), lg_c.reshape(B * C, N_EXPERTS)], axis=0)
        gates, dest, sorted_tok, vis_blk, vis_e, vis_lo, n_vis = _route(lg_all)
        xg = jnp.take(fx_all, sorted_tok, axis=0)
        wg, wl = _split_gu(exp_w_gu[i])
        bg = exp_b_gu[i][:, None, 0::2]
        bl = exp_b_gu[i][:, None, 1::2]
        wd = exp_w_dn[i].astype(BF16)
        bd = exp_b_dn[i][:, None, :]
        y = _moe_experts(vis_blk, vis_e, vis_lo, n_vis, xg, wg, wl, bg, bl, wd, bd)
        tm_cx = 256 if S % 256 == 0 else S
        yg_x = jnp.take(y, dest[:n_x].T.reshape(TOP_K, B, S), axis=0)
        h = _combine(yg_x, gates[:n_x].reshape(B, S, TOP_K), h, mx[5], final_g[None, :], final=last, tm=tm_cx)
        if not last:
            yg_c = jnp.take(y, dest[n_x:].T.reshape(TOP_K, B, C), axis=0)
            hc = _combine(yg_c, gates[n_x:].reshape(B, C, TOP_K), hc, mc[5], final_g[None, :], final=False, tm=C)
    return h
```

```python
import functools
import math

import jax
import jax.numpy as jnp
import numpy as np
from jax import lax
from jax.experimental import pallas as pl
from jax.experimental.pallas import tpu as pltpu

F32 = jnp.float32
BF16 = jnp.bfloat16

GRID_W = 64
HEAD_DIM = 64
ROT_FREQS = HEAD_DIM // 4
ROPE_THETA = 10000.0
NORM_EPS = 1e-5
POOL_WINDOWS = (2, 4, 8, 16)
POOL_GROUP = 64
POOL_WIDTH = POOL_GROUP * len(POOL_WINDOWS)
POOL_HALO = max(POOL_WINDOWS) // 2
ATT_HEADS = 12
ATT_KV_HEADS = 3
ATT_GROUP = ATT_HEADS // ATT_KV_HEADS
ATT_WINDOW = 128
ATT_BLOCK = 128
ATT_Q = ATT_HEADS * HEAD_DIM
ATT_KV = ATT_KV_HEADS * HEAD_DIM
EVEN_IN = POOL_WIDTH + ATT_Q + 2 * ATT_KV
RWKV_HEADS = 12
RWKV_WIDTH = RWKV_HEADS * HEAD_DIM
DECAY_LORA = 64
ICLR_LORA = 64
GATE_LORA = 160
GN_EPS = 64e-5
LORA_LO = 3 * RWKV_WIDTH
STATE_HI = LORA_LO + 2 * DECAY_LORA + 2 * ICLR_LORA
RWKV_IN = STATE_HI + GATE_LORA
FNET_GROUPS = 4
FNET_GROUP = 64
FNET_WIDTH = FNET_GROUPS * FNET_GROUP
ODD_IN = RWKV_IN + FNET_WIDTH
N_EXPERTS = 32
TOP_K = 4
SWIGLU_LIMIT = 7.0
SWIGLU_ALPHA = 1.702

LANES = 128
SUBLANES = 8
SCAN_CHUNK = 64
SCAN_GROUP = 4
SPLIT_ROWS = 256
MOE_ROWS = 512
MASK_NEG = -1e30


def _cparams(n_axes, vmem_mb):
    return pltpu.CompilerParams(
        dimension_semantics=("arbitrary",) * n_axes,
        vmem_limit_bytes=vmem_mb * 1024 * 1024,
    )


def _dot(a, b):
    return jnp.dot(a, b, preferred_element_type=F32)


def _dot_nt(a, b):
    return lax.dot_general(a, b, (((1,), (1,)), ((), ())), preferred_element_type=F32)


def _dot_tn(a, b):
    return lax.dot_general(a, b, (((0,), (0,)), ((), ())), preferred_element_type=F32)


def _split(x):
    hi = x.astype(BF16)
    lo = (x - hi.astype(F32)).astype(BF16)
    return hi, lo


def _dot3(a, b):
    ah, al = _split(a)
    bh, bl = _split(b)
    return _dot(ah, bh) + (_dot(al, bh) + _dot(ah, bl))


def _dot_exact_lhs(a_exact, b):
    a16 = a_exact.astype(BF16)
    b1 = b.astype(BF16)
    r1 = b - b1.astype(F32)
    b2 = r1.astype(BF16)
    b3 = (r1 - b2.astype(F32)).astype(BF16)
    return _dot(a16, b1) + (_dot(a16, b2) + _dot(a16, b3))


def _modnorm(x, g, sh, sc):
    ms = jnp.mean(x * x, axis=-1, keepdims=True)
    xn = x * lax.rsqrt(ms + NORM_EPS) * g
    return xn * (1.0 + sc) + sh


def _sigmoid(x):
    return 1.0 / (1.0 + jnp.exp(-x))


def _head_sum(x):
    n = x.shape[1] // LANES
    lane = lax.broadcasted_iota(jnp.int32, (x.shape[0], LANES), 1)
    lo_mask = lane < HEAD_DIM
    parts = []
    for c in range(n):
        xc = x[:, c * LANES:(c + 1) * LANES]
        s_lo = jnp.sum(jnp.where(lo_mask, xc, 0.0), axis=-1, keepdims=True)
        s_hi = jnp.sum(jnp.where(lo_mask, 0.0, xc), axis=-1, keepdims=True)
        parts.append(jnp.where(lo_mask, s_lo, s_hi))
    return jnp.concatenate(parts, axis=1)


def _ada_kernel(c_ref, w_ref, b_ref, o_ref):
    x = c_ref[...]
    x = x * _sigmoid(x)
    o_ref[...] = _dot3(x, w_ref[...]) + b_ref[...]


def _ada_all(cvec, ada_w, ada_b):
    depth, d, n6 = ada_w.shape
    r = cvec.shape[0]
    tn = 1536 if n6 % 1536 == 0 else n6
    return pl.pallas_call(
        _ada_kernel,
        out_shape=jax.ShapeDtypeStruct((depth, r, n6), F32),
        grid=(depth, n6 // tn),
        in_specs=[
            pl.BlockSpec((r, d), lambda i, j: (0, 0)),
            pl.BlockSpec((None, d, tn), lambda i, j: (i, 0, j)),
            pl.BlockSpec((None, 1, tn), lambda i, j: (i, 0, j)),
        ],
        out_specs=pl.BlockSpec((None, r, tn), lambda i, j: (i, 0, j)),
        compiler_params=_cparams(2, 48),
        name="ada_mod",
    )(cvec, ada_w, ada_b.reshape(depth, 1, n6))


def _bmap(arr):
    if arr.shape[0] == 1:
        return lambda b: 0
    return lambda b: b


def _even_in_kernel(h_ref, sh_ref, sc_ref, g_ref, w_ref, cos_ref, sin_ref,
                    pool_ref, q_ref, kv_ref, *, rope):
    a = _modnorm(h_ref[...], g_ref[...], sh_ref[...], sc_ref[...]).astype(BF16)
    px = _dot(a, w_ref[...])
    pool_ref[...] = px[:, :POOL_WIDTH]
    n_chunks = (ATT_Q + 2 * ATT_KV) // LANES
    n_full = (ATT_Q + ATT_KV) // LANES
    outs = []
    if rope:
        lane = lax.broadcasted_iota(jnp.int32, (px.shape[0], LANES), 1)
        first = (lane % (2 * ROT_FREQS)) < ROT_FREQS
    for c in range(n_chunks):
        x = px[:, POOL_WIDTH + c * LANES:POOL_WIDTH + (c + 1) * LANES]
        if rope and c <= n_full:
            t0 = 0 if c < n_full else LANES
            cs = cos_ref[:, t0:t0 + LANES]
            sn = sin_ref[:, t0:t0 + LANES]
            rot = jnp.where(first, pltpu.roll(x, LANES - ROT_FREQS, 1), pltpu.roll(x, ROT_FREQS, 1))
            x = x * cs + rot * sn
        if c < ATT_Q // LANES:
            x = x * (HEAD_DIM ** -0.5)
        outs.append(x.astype(BF16))
    nq = ATT_Q // LANES
    q_ref[...] = jnp.concatenate(outs[:nq], axis=1)
    kv_ref[...] = jnp.concatenate(outs[nq:], axis=1)


def _even_in(h, sh, sc, g, w_bf, cos_t, sin_t, *, rope, tm):
    B, T, D = h.shape
    nt = T // tm
    bm = _bmap(sh)
    kern = functools.partial(_even_in_kernel, rope=rope)
    return pl.pallas_call(
        kern,
        out_shape=(
            jax.ShapeDtypeStruct((B, T, POOL_WIDTH), F32),
            jax.ShapeDtypeStruct((B, T, ATT_Q), BF16),
            jax.ShapeDtypeStruct((B, T, 2 * ATT_KV), BF16),
        ),
        grid=(nt, B),
        in_specs=[
            pl.BlockSpec((None, tm, D), lambda s, b: (b, s, 0)),
            pl.BlockSpec((None, 1, D), lambda s, b: (bm(b), 0, 0)),
            pl.BlockSpec((None, 1, D), lambda s, b: (bm(b), 0, 0)),
            pl.BlockSpec((1, D), lambda s, b: (0, 0)),
            pl.BlockSpec((D, EVEN_IN), lambda s, b: (0, 0)),
            pl.BlockSpec((tm, 2 * LANES), lambda s, b: (s, 0)),
            pl.BlockSpec((tm, 2 * LANES), lambda s, b: (s, 0)),
        ],
        out_specs=(
            pl.BlockSpec((None, tm, POOL_WIDTH), lambda s, b: (b, s, 0)),
            pl.BlockSpec((None, tm, ATT_Q), lambda s, b: (b, s, 0)),
            pl.BlockSpec((None, tm, 2 * ATT_KV), lambda s, b: (b, s, 0)),
        ),
        compiler_params=_cparams(2, 48),
        name="even_in",
    )(h, sh, sc, g, w_bf, cos_t, sin_t)


def _rope_tables(T):
    t = jnp.arange(T, dtype=jnp.int32)
    row = (t // GRID_W).astype(F32)
    col = (t % GRID_W).astype(F32)
    inv_freq = ROPE_THETA ** (-jnp.arange(ROT_FREQS, dtype=F32) / ROT_FREQS)
    ang_r = row[:, None] * inv_freq
    ang_c = col[:, None] * inv_freq
    cos_h = jnp.concatenate([jnp.cos(ang_r)] * 2 + [jnp.cos(ang_c)] * 2, axis=1)
    sin_h = jnp.concatenate([-jnp.sin(ang_r), jnp.sin(ang_r), -jnp.sin(ang_c), jnp.sin(ang_c)], axis=1)
    one = jnp.ones_like(cos_h)
    zero = jnp.zeros_like(sin_h)
    cos_t = jnp.concatenate([cos_h, cos_h, cos_h, one], axis=1)
    sin_t = jnp.concatenate([sin_h, sin_h, sin_h, zero], axis=1)
    return cos_t, sin_t


def _residual_and_router(h, y, gm, gf_norm, shf, scf, rw, rb, hn_ref, fx_ref, ti_ref, tg_ref):
    hn = h + gm * y
    hn_ref[...] = hn
    fx = _modnorm(hn, gf_norm, shf, scf)
    fx_ref[...] = fx.astype(BF16)
    fh, fl = _split(fx)
    wh, wl = _split(rw)
    logits = _dot_nt(wh, fh) + (_dot_nt(wl, fh) + _dot_nt(wh, fl)) + rb
    rows = logits.shape[1]
    eidx = lax.broadcasted_iota(jnp.int32, (N_EXPERTS, rows), 0).astype(F32)
    vals, idxs = [], []
    for _ in range(TOP_K):
        mx = jnp.max(logits, axis=0, keepdims=True)
        ix = jnp.min(jnp.where(logits == mx, eidx, float(N_EXPERTS)), axis=0, keepdims=True)
        vals.append(mx)
        idxs.append(ix)
        logits = jnp.where(eidx == ix, MASK_NEG, logits)
    es = [jnp.exp(v - vals[0]) for v in vals]
    den = es[0]
    for e in es[1:]:
        den = den + e
    tg_ref[...] = jnp.concatenate(es, axis=0) / den
    ti_ref[...] = jnp.concatenate(idxs, axis=0).astype(jnp.int32)


def _even_mix_kernel(*refs, local, tq, T, n_ctx):
    it = iter(refs)
    sink_ref = next(it)
    q_ref = next(it)
    if local:
        kvp_ref, kvc_ref, kvn_ref = next(it), next(it), next(it)
    ckv_ref = next(it)
    up_ref, uc_ref, un_ref = next(it), next(it), next(it)
    h_ref, gm_ref, shf_ref, scf_ref, gfn_ref = next(it), next(it), next(it), next(it), next(it)
    wout_ref, pw_ref, ps_ref, rw_ref, rb_ref = next(it), next(it), next(it), next(it), next(it)
    hn_ref, fx_ref, ti_ref, tg_ref = next(it), next(it), next(it), next(it)

    j = pl.program_id(1)
    nb = pl.num_programs(1)

    n_loc = 3 * ATT_BLOCK if local else 0
    n_keys = n_loc + n_ctx
    rows = ATT_GROUP * tq
    if local:
        r_i = lax.broadcasted_iota(jnp.int32, (rows, n_loc), 0)
        c_i = lax.broadcasted_iota(jnp.int32, (rows, n_loc), 1)
        q_pos = j * tq + r_i % tq
        k_pos = (j - 1) * ATT_BLOCK + c_i
        valid = (jnp.abs(q_pos - k_pos) <= ATT_WINDOW) & (k_pos >= 0) & (k_pos < T)
    row_head = lax.broadcasted_iota(jnp.int32, (rows, 1), 0) // tq
    q = q_ref[...]
    ckv = ckv_ref[...]
    if local:
        kvl = jnp.concatenate([kvp_ref[...], kvc_ref[...], kvn_ref[...]], axis=0)
    att = []
    for g in range(ATT_KV_HEADS):
        qs = jnp.concatenate(
            [q[:, (g * ATT_GROUP + i) * HEAD_DIM:(g * ATT_GROUP + i + 1) * HEAD_DIM] for i in range(ATT_GROUP)],
            axis=0)
        kc = ckv[:, g * HEAD_DIM:(g + 1) * HEAD_DIM]
        vc = ckv[:, ATT_KV + g * HEAD_DIM:ATT_KV + (g + 1) * HEAD_DIM]
        s_ctx = _dot_nt(qs, kc)
        sink = jnp.zeros((rows, 1), F32)
        for i in range(ATT_GROUP):
            sink = jnp.where(row_head == i, sink_ref[g * ATT_GROUP + i], sink)
        if local:
            kl = kvl[:, g * HEAD_DIM:(g + 1) * HEAD_DIM]
            vl = kvl[:, ATT_KV + g * HEAD_DIM:ATT_KV + (g + 1) * HEAD_DIM]
            s_loc = jnp.where(valid, _dot_nt(qs, kl), MASK_NEG)
            m = jnp.maximum(jnp.maximum(jnp.max(s_loc, axis=-1, keepdims=True),
                                        jnp.max(s_ctx, axis=-1, keepdims=True)), sink)
            p_loc = jnp.exp(s_loc - m)
            p_ctx = jnp.exp(s_ctx - m)
            den = (jnp.sum(p_loc, axis=-1, keepdims=True) + jnp.sum(p_ctx, axis=-1, keepdims=True)
                   + jnp.exp(sink - m))
            o = _dot(p_loc.astype(BF16), vl) + _dot(p_ctx.astype(BF16), vc)
        else:
            m = jnp.maximum(jnp.max(s_ctx, axis=-1, keepdims=True), sink)
            p_ctx = jnp.exp(s_ctx - m)
            den = jnp.sum(p_ctx, axis=-1, keepdims=True) + jnp.exp(sink - m)
            o = _dot(p_ctx.astype(BF16), vc)
        o = o / den
        for i in range(ATT_GROUP):
            att.append(o[i * tq:(i + 1) * tq])
    att_x = jnp.concatenate(att, axis=1)

    u = uc_ref[...]
    up = jnp.where(j > 0, up_ref[...], 0.0)
    un = jnp.where(j < nb - 1, un_ref[...], 0.0)
    e = jnp.concatenate([up, u, un], axis=0)
    n_e = tq + 2 * POOL_HALO

    def shifted(x, k):
        return pltpu.roll(x, k % n_e, 0)

    a1 = e + shifted(e, 1)
    a2 = shifted(a1, 1) + shifted(a1, -1)
    a3 = shifted(a2, 2) + shifted(a2, -2)
    a4 = shifted(a3, 4) + shifted(a3, -4)
    lane = lax.broadcasted_iota(jnp.int32, (n_e, POOL_WIDTH), 1)
    grp = lane // POOL_GROUP
    win = jnp.where(grp == 0, a1, jnp.where(grp == 1, a2, jnp.where(grp == 2, a3, a4)))
    win = win[POOL_HALO:POOL_HALO + tq]
    t_i = j * tq + lax.broadcasted_iota(jnp.int32, (tq, POOL_WIDTH), 0)
    half = jnp.left_shift(1, lax.broadcasted_iota(jnp.int32, (tq, POOL_WIDTH), 1) // POOL_GROUP)
    cnt = (jnp.minimum(t_i + half, T) - jnp.maximum(t_i - half, 0)).astype(F32)
    pooled = win / cnt - u
    pool_x = _dot(pooled.astype(BF16), pw_ref[...]) * ps_ref[...]

    mix = jnp.concatenate([pool_x, att_x], axis=1).astype(BF16)
    y = _dot(mix, wout_ref[...])
    _residual_and_router(h_ref[...], y, gm_ref[...], gfn_ref[...], shf_ref[...], scf_ref[...],
                         rw_ref[...], rb_ref[...], hn_ref, fx_ref, ti_ref, tg_ref)


def _even_mix(sink, q, kv, ckv, pool_u, h, gm, shf, scf, gfn, wout, pw_bd, pscale, rw, rb, *, local):
    B, T, D = h.shape
    n_ctx = ckv.shape[1]
    tq = ATT_BLOCK if local else T
    nb = T // tq
    r8 = tq // SUBLANES
    n8 = T // SUBLANES
    bm = _bmap(gm)
    kern = functools.partial(_even_mix_kernel, local=local, tq=tq, T=T, n_ctx=n_ctx)
    vec = lambda: pl.BlockSpec((None, 1, D), lambda b, j: (bm(b), 0, 0))
    full = lambda a: pl.BlockSpec(a.shape, lambda b, j: (0,) * a.ndim)
    in_specs = [pl.BlockSpec(memory_space=pltpu.SMEM),
                pl.BlockSpec((None, tq, ATT_Q), lambda b, j: (b, j, 0))]
    args = [sink, q]
    if local:
        in_specs += [
            pl.BlockSpec((None, ATT_BLOCK, 2 * ATT_KV), lambda b, j: (b, jnp.maximum(j - 1, 0), 0)),
            pl.BlockSpec((None, ATT_BLOCK, 2 * ATT_KV), lambda b, j: (b, j, 0)),
            pl.BlockSpec((None, ATT_BLOCK, 2 * ATT_KV), lambda b, j: (b, jnp.minimum(j + 1, nb - 1), 0)),
        ]
        args += [kv, kv, kv]
    in_specs += [
        pl.BlockSpec((None, n_ctx, 2 * ATT_KV), lambda b, j: (b, 0, 0)),
        pl.BlockSpec((None, SUBLANES, POOL_WIDTH), lambda b, j: (b, jnp.maximum(j * r8 - 1, 0), 0)),
        pl.BlockSpec((None, tq, POOL_WIDTH), lambda b, j: (b, j, 0)),
        pl.BlockSpec((None, SUBLANES, POOL_WIDTH), lambda b, j: (b, jnp.minimum((j + 1) * r8, n8 - 1), 0)),
        pl.BlockSpec((None, tq, D), lambda b, j: (b, j, 0)),
        vec(), vec(), vec(), full(gfn), full(wout), full(pw_bd), full(pscale), full(rw), full(rb),
    ]
    args += [ckv, pool_u, pool_u, pool_u, h, gm, shf, scf, gfn, wout, pw_bd, pscale, rw, rb]
    return pl.pallas_call(
        kern,
        out_shape=(
            jax.ShapeDtypeStruct((B, T, D), F32),
            jax.ShapeDtypeStruct((B, T, D), BF16),
            jax.ShapeDtypeStruct((B, TOP_K, T), jnp.int32),
            jax.ShapeDtypeStruct((B, TOP_K, T), F32),
        ),
        grid=(B, nb),
        in_specs=in_specs,
        out_specs=(
            pl.BlockSpec((None, tq, D), lambda b, j: (b, j, 0)),
            pl.BlockSpec((None, tq, D), lambda b, j: (b, j, 0)),
            pl.BlockSpec((None, TOP_K, tq), lambda b, j: (b, 0, j)),
            pl.BlockSpec((None, TOP_K, tq), lambda b, j: (b, 0, j)),
        ),
        compiler_params=_cparams(2, 48),
        name="even_mix_local" if local else "even_mix_ctx",
    )(*args)


def _odd_in_kernel(h_ref, hp_ref, hn_ref, sh_ref, sc_ref, g_ref, w_ref, mu_ref, w0_ref, w2_ref,
                   a0_ref, a2_ref, kk_ref, ka_ref, rk_ref,
                   r_out, v_out, kkn_out, bonus_out, lw_out, kd_out, bd_out, gc_out, fn_out, *, tm):
    s = pl.program_id(1)
    ns = pl.num_programs(1)
    g, sh, sc = g_ref[...], sh_ref[...], sc_ref[...]
    w = w_ref[...]
    a = _modnorm(h_ref[...], g, sh, sc).astype(BF16)
    px = _dot(a, w)
    fn_out[...] = px[:, RWKV_IN:]
    main = px[:, :RWKV_IN]
    ap = _modnorm(hp_ref[...], g, sh, sc).astype(BF16)
    an = _modnorm(hn_ref[...], g, sh, sc).astype(BF16)
    halo = _dot(jnp.concatenate([ap, an], axis=0), w[:, :RWKV_IN])
    prev_row = jnp.where(s > 0, halo[SUBLANES - 1:SUBLANES], 0.0)
    next_row = jnp.where(s < ns - 1, halo[SUBLANES:SUBLANES + 1], 0.0)
    row = lax.broadcasted_iota(jnp.int32, (tm, 1), 0)
    prev = jnp.where(row == 0, prev_row, pltpu.roll(main, 1, 0))
    nxt = jnp.where(row == tm - 1, next_row, pltpu.roll(main, tm - 1, 0))
    mu = mu_ref[...]
    fs = main + mu[0:1] * (prev - main) + mu[1:2] * (nxt - main)

    W = RWKV_WIDTH
    r = fs[:, :W]
    k = fs[:, W:2 * W]
    v = fs[:, 2 * W:3 * W]
    lora = fs[:, LORA_LO:STATE_HI]
    gc_out[...] = fs[:, STATE_HI:RWKV_IN]
    r_out[...] = r
    v_out[...] = v

    kx = k * kk_ref[...]
    nrm = jnp.sqrt(_head_sum(kx * kx))
    kkn = kx / jnp.maximum(nrm, 1e-12)
    kkn_out[...] = kkn
    ka = ka_ref[...]
    ksum = None
    for d in range(2):
        wd = lora[:, d * DECAY_LORA:(d + 1) * DECAY_LORA]
        o_a = 2 * DECAY_LORA
        ad = lora[:, o_a + d * ICLR_LORA:o_a + (d + 1) * ICLR_LORA]
        xw = _dot(jnp.tanh(wd).astype(BF16), w2_ref[d]) + w0_ref[d:d + 1]
        z = -xw
        softplus = jnp.maximum(z, 0.0) + jnp.log(1.0 + jnp.exp(-jnp.abs(z)))
        w_log = -softplus - 0.5
        lw_out[d] = -jnp.exp(w_log)
        xa = _dot(ad.astype(BF16), a2_ref[d]) + a0_ref[d:d + 1]
        a_d = _sigmoid(xa)
        k_d = k * (1.0 + (a_d - 1.0) * ka)
        kd_out[d] = k_d
        bd_out[d] = kkn * a_d
        ksum = k_d if ksum is None else ksum + k_d
    coef = _head_sum(r * ksum * rk_ref[...])
    bonus_out[...] = coef * v


def _odd_in(h, sh, sc, g, w_bf, mu, w0, w2_bf, a0, a2_bf, k_k, k_a, r_k, *, tm):
    B, T, D = h.shape
    ns = T // tm
    r8 = tm // SUBLANES
    n8 = T // SUBLANES
    bm = _bmap(sh)
    W = RWKV_WIDTH
    full = lambda a: pl.BlockSpec(a.shape, lambda b, s: (0,) * a.ndim)
    tok = lambda n: pl.BlockSpec((None, tm, n), lambda b, s: (b, s, 0))
    tok2 = lambda n: pl.BlockSpec((2, None, tm, n), lambda b, s: (0, b, s, 0))
    kern = functools.partial(_odd_in_kernel, tm=tm)
    return pl.pallas_call(
        kern,
        out_shape=(
            jax.ShapeDtypeStruct((B, T, W), F32),
            jax.ShapeDtypeStruct((B, T, W), F32),
            jax.ShapeDtypeStruct((B, T, W), F32),
            jax.ShapeDtypeStruct((B, T, W), F32),
            jax.ShapeDtypeStruct((2, B, T, W), F32),
            jax.ShapeDtypeStruct((2, B, T, W), F32),
            jax.ShapeDtypeStruct((2, B, T, W), F32),
            jax.ShapeDtypeStruct((B, T, GATE_LORA), F32),
            jax.ShapeDtypeStruct((B, T, FNET_WIDTH), F32),
        ),
        grid=(B, ns),
        in_specs=[
            pl.BlockSpec((None, tm, D), lambda b, s: (b, s, 0)),
            pl.BlockSpec((None, SUBLANES, D), lambda b, s: (b, jnp.maximum(s * r8 - 1, 0), 0)),
            pl.BlockSpec((None, SUBLANES, D), lambda b, s: (b, jnp.minimum((s + 1) * r8, n8 - 1), 0)),
            pl.BlockSpec((None, 1, D), lambda b, s: (bm(b), 0, 0)),
            pl.BlockSpec((None, 1, D), lambda b, s: (bm(b), 0, 0)),
            full(g), full(w_bf), full(mu), full(w0), full(w2_bf), full(a0), full(a2_bf),
            full(k_k), full(k_a), full(r_k),
        ],
        out_specs=(tok(W), tok(W), tok(W), tok(W), tok2(W), tok2(W), tok2(W), tok(GATE_LORA), tok(FNET_WIDTH)),
        compiler_params=_cparams(2, 60),
        name="odd_in",
    )(h, h, h, sh, sc, g, w_bf, mu, w0, w2_bf, a0, a2_bf, k_k, k_a, r_k)


SCAN_LEVELS = tuple(2 ** k for k in range(int(math.log2(SCAN_CHUNK))))
MASK_BEFORE, MASK_UPTO, MASK_LEVEL0 = 0, 1, 2


def _scan_masks():
    R = SCAN_GROUP * SCAN_CHUNK
    idx = np.arange(R)
    same = (idx[:, None] // SCAN_CHUNK) == (idx[None, :] // SCAN_CHUNK)
    t = (idx % SCAN_CHUNK)[:, None]
    s = (idx % SCAN_CHUNK)[None, :]
    out = []
    for sign in (1, -1):
        order = (t - s) * sign
        ms = [same & (order > 0), same & (order >= 0)]
        for m in SCAN_LEVELS:
            ms.append(same & (t // (2 * m) == s // (2 * m)) & ((((t // m) % 2) - ((s // m) % 2)) * sign == 1))
        out.append(np.stack(ms))
    return jnp.asarray(np.stack(out), F32), jnp.asarray(same, F32)


def _scan_kernel(rf_ref, vf_ref, kkf_ref, rb_ref, vb_ref, kkb_ref, lwf_ref, kdf_ref, bdf_ref,
                 lwb_ref, kdb_ref, bdb_ref, mask_ref, hb_ref, s0_ref, yf_ref, yb_ref, sT_ref, s_scr, *, L):
    c = pl.program_id(1)
    nc = pl.num_programs(1)
    R = SCAN_GROUP * L
    GW = SCAN_GROUP * HEAD_DIM

    @pl.when(c == 0)
    def _():
        s_scr[...] = s0_ref[...]

    head_blk = hb_ref[...]

    def wide(x):
        return (jnp.concatenate([x] * SCAN_GROUP, axis=0) * head_blk).astype(BF16)

    NG = RWKV_HEADS // SCAN_GROUP
    dirs = ((0, (rf_ref, vf_ref, kkf_ref, lwf_ref, kdf_ref, bdf_ref)),
            (1, (rb_ref, vb_ref, kkb_ref, lwb_ref, kdb_ref, bdb_ref)))
    wides = {k: [] for k in ("kt", "rt", "kh", "bh", "v", "ke", "be")}
    g_tots = []
    for d, (r_ref, v_ref, kk_ref, lw_ref, kd_ref, bd_ref) in dirs:
        lw = lw_ref[...]
        c_in = _dot_exact_lhs(mask_ref[d, MASK_UPTO][:L, :L], lw)
        c_ex = c_in - lw
        tot = jnp.sum(lw, axis=0, keepdims=True)
        kd = kd_ref[...]
        bd = bd_ref[...]
        inv = jnp.exp(-c_in)
        end = jnp.exp(tot - c_in)
        g_tot = jnp.exp(tot)
        cols = dict(
            kt=kk_ref[...] * jnp.exp(c_ex), rt=r_ref[...] * jnp.exp(c_in), kh=kd * inv, bh=bd * inv,
            v=v_ref[...], ke=kd * end, be=bd * end)
        for g in range(NG):
            sl = slice(g * GW, (g + 1) * GW)
            for k, a in cols.items():
                wides[k].append(wide(a[:, sl]))
            g_tots.append(jnp.broadcast_to(g_tot[:, sl], (GW, GW)))
    w = {k: jnp.stack(a) for k, a in wides.items()}
    g_tot_all = jnp.stack(g_tots)

    def masked(x, idx):
        return jnp.concatenate([x[:NG] * mask_ref[0, idx][None], x[NG:] * mask_ref[1, idx][None]], axis=0)

    bmm = lambda a, b: jnp.einsum("bij,bjk->bik", a, b, preferred_element_type=F32)
    bmm_nt = lambda a, b: jnp.einsum("bik,bjk->bij", a, b, preferred_element_type=F32)
    bmm_tn = lambda a, b: jnp.einsum("bki,bkj->bij", a, b, preferred_element_type=F32)

    lhs = jnp.concatenate([w["kt"], w["rt"]], axis=1)
    s4 = bmm_nt(lhs, jnp.concatenate([w["kh"], w["bh"]], axis=1))
    a_kk = masked(s4[:, :R, :R], MASK_BEFORE).astype(BF16)
    m = masked(s4[:, :R, R:], MASK_BEFORE)
    a_rk = masked(s4[:, R:, :R], MASK_UPTO).astype(BF16)
    a_rb = masked(s4[:, R:, R:], MASK_UPTO).astype(BF16)
    eye = (mask_ref[0, MASK_UPTO] - mask_ref[0, MASK_BEFORE])[None]
    dinv = eye - masked(m, MASK_LEVEL0)
    for li in range(1, len(SCAN_LEVELS)):
        d16 = dinv.astype(BF16)
        e16 = masked(m, MASK_LEVEL0 + li).astype(BF16)
        dinv = dinv - bmm(bmm(d16, e16).astype(BF16), d16)
    s_all = s_scr[...].reshape(2 * NG, GW, GW)
    ks = bmm_nt(lhs, s_all.astype(BF16))
    u16 = bmm(dinv.astype(BF16), (ks[:, :R] + bmm(a_kk, w["v"])).astype(BF16)).astype(BF16)
    yw = ks[:, R:] + bmm(a_rk, w["v"]) - bmm(a_rb, u16)
    y_all = yw[:, 0:L]
    for hh in range(1, SCAN_GROUP):
        y_all = y_all + yw[:, hh * L:(hh + 1) * L]
    yf_ref[...] = jnp.concatenate([y_all[g] for g in range(NG)], axis=1)
    yb_ref[...] = jnp.concatenate([y_all[NG + g] for g in range(NG)], axis=1)
    vu = jnp.concatenate([w["v"], -u16], axis=1)
    ke = jnp.concatenate([w["ke"], w["be"]], axis=1)
    s_scr[...] = (s_all * g_tot_all + bmm_tn(vu, ke)).reshape(2, NG, GW, GW)

    @pl.when(c == nc - 1)
    def _():
        sT_ref[...] = s_scr[...]


def _scan_state_shape(B):
    gw = SCAN_GROUP * HEAD_DIM
    return (2, B, RWKV_HEADS // SCAN_GROUP, gw, gw)


def _scan(r, v, kk, lw, kd, bd, s0, masks, head_blk):
    B, T, W = r.shape
    L = SCAN_CHUNK
    assert L == HEAD_DIM and T % L == 0
    nc = T // L
    st_shape = _scan_state_shape(B)
    fwd = pl.BlockSpec((None, L, W), lambda b, c: (b, c, 0))
    bwd = pl.BlockSpec((None, L, W), lambda b, c: (b, nc - 1 - c, 0))
    fwd_d = pl.BlockSpec((None, None, L, W), lambda b, c: (0, b, c, 0))
    bwd_d = pl.BlockSpec((None, None, L, W), lambda b, c: (1, b, nc - 1 - c, 0))
    state = pl.BlockSpec((2, None) + st_shape[2:], lambda b, c: (0, b, 0, 0, 0))
    full = lambda a: pl.BlockSpec(a.shape, lambda b, c: (0,) * a.ndim)
    kern = functools.partial(_scan_kernel, L=L)
    return pl.pallas_call(
        kern,
        out_shape=(
            jax.ShapeDtypeStruct((B, T, W), F32),
            jax.ShapeDtypeStruct((B, T, W), F32),
            jax.ShapeDtypeStruct(st_shape, F32),
        ),
        grid=(B, nc),
        in_specs=[fwd, fwd, fwd, bwd, bwd, bwd, fwd_d, fwd_d, fwd_d, bwd_d, bwd_d, bwd_d,
                  full(masks), full(head_blk), state],
        out_specs=(fwd, bwd, state),
        scratch_shapes=[pltpu.VMEM((2,) + st_shape[2:], F32)],
        compiler_params=_cparams(2, 56),
        name="rwkv_scan",
    )(r, v, kk, r, v, kk, lw, kd, bd, lw, kd, bd, masks, head_blk, s0)


def _odd_out_kernel(yf_ref, yb_ref, bonus_ref, gc_ref, u_ref, dft_ref, cs_ref, h_ref, gm_ref, shf_ref, scf_ref,
                    gfn_ref, gng_ref, gnb_ref, g2_ref, wout_ref, rw_ref, rb_ref,
                    hn_ref, fx_ref, ti_ref, tg_ref, ucs_scr, *, T):
    s = pl.program_id(1)

    @pl.when(s == 0)
    def _():
        t = _dot(u_ref[...].astype(BF16), cs_ref[...])
        ucs_scr[0:T, :] = t[:, :FNET_WIDTH].astype(BF16)
        ucs_scr[T:2 * T, :] = t[:, FNET_WIDTH:].astype(BF16)

    f = _dot(dft_ref[...], ucs_scr[...])
    y = yf_ref[...] + yb_ref[...]
    mean = _head_sum(y) * (1.0 / HEAD_DIM)
    yc = y - mean
    var = _head_sum(yc * yc) * (1.0 / HEAD_DIM)
    yn = yc * lax.rsqrt(var + GN_EPS) * gng_ref[...] + gnb_ref[...]
    out = yn + bonus_ref[...]
    gate = _dot(_sigmoid(gc_ref[...]).astype(BF16), g2_ref[...])
    o = out * gate
    mix = jnp.concatenate([o, f], axis=1).astype(BF16)
    ymix = _dot(mix, wout_ref[...])
    _residual_and_router(h_ref[...], ymix, gm_ref[...], gfn_ref[...], shf_ref[...], scf_ref[...],
                         rw_ref[...], rb_ref[...], hn_ref, fx_ref, ti_ref, tg_ref)


def _odd_out(yf, yb, bonus, gc, u, dft, cs64, h, gm, shf, scf, gfn, gn_g, gn_b, g2_bf, wout, rw, rb, *, tm):
    B, T, D = h.shape
    ns = T // tm
    W = RWKV_WIDTH
    bm = _bmap(gm)
    vec = lambda: pl.BlockSpec((None, 1, D), lambda b, s: (bm(b), 0, 0))
    full = lambda a: pl.BlockSpec(a.shape, lambda b, s: (0,) * a.ndim)
    kern = functools.partial(_odd_out_kernel, T=T)
    return pl.pallas_call(
        kern,
        out_shape=(
            jax.ShapeDtypeStruct((B, T, D), F32),
            jax.ShapeDtypeStruct((B, T, D), BF16),
            jax.ShapeDtypeStruct((B, TOP_K, T), jnp.int32),
            jax.ShapeDtypeStruct((B, TOP_K, T), F32),
        ),
        grid=(B, ns),
        in_specs=[
            pl.BlockSpec((None, tm, W), lambda b, s: (b, s, 0)),
            pl.BlockSpec((None, tm, W), lambda b, s: (b, s, 0)),
            pl.BlockSpec((None, tm, W), lambda b, s: (b, s, 0)),
            pl.BlockSpec((None, tm, GATE_LORA), lambda b, s: (b, s, 0)),
            pl.BlockSpec((None, T, FNET_WIDTH), lambda b, s: (b, 0, 0)),
            pl.BlockSpec((tm, 2 * T), lambda b, s: (s, 0)),
            full(cs64),
            pl.BlockSpec((None, tm, D), lambda b, s: (b, s, 0)),
            vec(), vec(), vec(), full(gfn), full(gn_g), full(gn_b), full(g2_bf), full(wout), full(rw), full(rb),
        ],
        out_specs=(
            pl.BlockSpec((None, tm, D), lambda b, s: (b, s, 0)),
            pl.BlockSpec((None, tm, D), lambda b, s: (b, s, 0)),
            pl.BlockSpec((None, TOP_K, tm), lambda b, s: (b, 0, s)),
            pl.BlockSpec((None, TOP_K, tm), lambda b, s: (b, 0, s)),
        ),
        scratch_shapes=[pltpu.VMEM((2 * T, FNET_WIDTH), BF16)],
        compiler_params=_cparams(2, 48),
        name="odd_out",
    )(yf, yb, bonus, gc, u, dft, cs64, h, gm, shf, scf, gfn, gn_g, gn_b, g2_bf, wout, rw, rb)


def _dft_tables(T):
    t = jnp.arange(T, dtype=jnp.int32)
    ang = ((t[:, None] * t[None, :]) % T).astype(F32) * (2.0 * math.pi / T)
    scale = 1.0 / math.sqrt(T * FNET_GROUP)
    return (jnp.concatenate([jnp.cos(ang), -jnp.sin(ang)], axis=1) * scale).astype(BF16)


def _channel_dft():
    c = jnp.arange(FNET_GROUP, dtype=jnp.int32)
    ang = ((c[:, None] * c[None, :]) % FNET_GROUP).astype(F32) * (2.0 * math.pi / FNET_GROUP)
    eye = jnp.eye(FNET_GROUPS, dtype=F32)
    return jnp.concatenate([jnp.kron(eye, jnp.cos(ang)), jnp.kron(eye, jnp.sin(ang))], axis=1).astype(BF16)


def _split_gu_kernel(w_ref, wg_ref, wl_ref, t_scr):
    n_slab = w_ref.shape[0] // LANES
    de = w_ref.shape[1] // 2
    for s in range(n_slab):
        t_scr[s] = w_ref[s * LANES:(s + 1) * LANES, :].T
    ev = [t_scr[s, pl.ds(0, de, stride=2), :].T for s in range(n_slab)]
    od = [t_scr[s, pl.ds(1, de, stride=2), :].T for s in range(n_slab)]
    wg_ref[...] = jnp.concatenate(ev, axis=0).astype(BF16)
    wl_ref[...] = jnp.concatenate(od, axis=0).astype(BF16)


def _split_gu(w_gu_all, layer):
    depth, E, D, de2 = w_gu_all.shape
    de = de2 // 2
    rows = E * D
    tr = SPLIT_ROWS if rows % SPLIT_ROWS == 0 else LANES
    nb = rows // tr
    wg, wl = pl.pallas_call(
        _split_gu_kernel,
        out_shape=(jax.ShapeDtypeStruct((rows, de), BF16), jax.ShapeDtypeStruct((rows, de), BF16)),
        grid=(nb,),
        in_specs=[pl.BlockSpec((tr, de2), lambda i: (layer * nb + i, 0))],
        out_specs=(pl.BlockSpec((tr, de), lambda i: (i, 0)), pl.BlockSpec((tr, de), lambda i: (i, 0))),
        scratch_shapes=[pltpu.VMEM((tr // LANES, de2, LANES), F32)],
        compiler_params=_cparams(1, 48),
        name="split_gu",
    )(w_gu_all.reshape(depth * rows, de2))
    return wg.reshape(E, D, de), wl.reshape(E, D, de)


def _moe_kernel(vb_ref, ve_ref, vlo_ref, nv_ref, x_ref, wg_ref, wl_ref, bg_ref, bl_ref, wd_ref, bd_ref, y_ref):
    v = pl.program_id(0)

    @pl.when(v < nv_ref[0])
    def _():
        x = x_ref[...]
        glu = jnp.minimum(_dot(x, wg_ref[...]) + bg_ref[...], SWIGLU_LIMIT)
        lin = jnp.clip(_dot(x, wl_ref[...]) + bl_ref[...], -SWIGLU_LIMIT, SWIGLU_LIMIT)
        act = glu * _sigmoid(SWIGLU_ALPHA * glu) * (lin + 1.0)
        y = _dot(act.astype(BF16), wd_ref[...].astype(BF16)) + bd_ref[...]
        lo = vlo_ref[v]

        @pl.when(lo == 0)
        def _():
            y_ref[...] = y

        @pl.when(lo > 0)
        def _():
            row = lax.broadcasted_iota(jnp.int32, y.shape, 0)
            y_ref[...] = jnp.where(row >= lo, y, y_ref[...])


def _moe_experts(vis_blk, vis_e, vis_lo, n_vis, xg, wg, wl, bg, bl, wd_all, layer, bd):
    n_rows, D = xg.shape
    E, _, DE = wg.shape
    n_visits = vis_blk.shape[0]
    wmap = lambda v, vb, ve, vlo, nv: (ve[v], 0, 0)
    dmap = lambda v, vb, ve, vlo, nv: (layer, ve[v], 0, 0)
    xmap = lambda v, vb, ve, vlo, nv: (vb[v], 0)
    grid_spec = pltpu.PrefetchScalarGridSpec(
        num_scalar_prefetch=4,
        grid=(n_visits,),
        in_specs=[
            pl.BlockSpec((MOE_ROWS, D), xmap),
            pl.BlockSpec((None, D, DE), wmap),
            pl.BlockSpec((None, D, DE), wmap),
            pl.BlockSpec((None, 1, DE), wmap),
            pl.BlockSpec((None, 1, DE), wmap),
            pl.BlockSpec((None, None, DE, D), dmap),
            pl.BlockSpec((None, 1, D), wmap),
        ],
        out_specs=pl.BlockSpec((MOE_ROWS, D), xmap),
    )
    return pl.pallas_call(
        _moe_kernel,
        out_shape=jax.ShapeDtypeStruct((n_rows, D), F32),
        grid_spec=grid_spec,
        compiler_params=_cparams(1, 56),
        name="moe_experts",
    )(vis_blk, vis_e, vis_lo, n_vis, xg, wg, wl, bg, bl, wd_all, bd)


def _route(top_idx):
    n_tok = top_idx.shape[0]
    n_assign = n_tok * TOP_K
    assert n_assign % MOE_ROWS == 0
    n_blocks = n_assign // MOE_ROWS
    flat_e = top_idx.reshape(-1).astype(jnp.int32)
    iota = jnp.arange(n_assign, dtype=jnp.int32)
    sorted_e, order = lax.sort((flat_e, iota), num_keys=1)
    _, dest = lax.sort((order, iota), num_keys=1)
    sorted_tok = order // TOP_K
    experts = jnp.arange(N_EXPERTS + 1, dtype=jnp.int32)
    bounds = jnp.sum((sorted_e[None, :] < experts[:, None]).astype(jnp.int32), axis=1)
    starts, ends = bounds[:-1], bounds[1:]
    first_blk = starts // MOE_ROWS
    n_vis_e = jnp.where(ends > starts, (ends - 1) // MOE_ROWS - first_blk + 1, 0)
    vis_end = jnp.cumsum(n_vis_e)
    vis_start = vis_end - n_vis_e
    n_vis = vis_end[-1]
    n_visits = n_blocks + N_EXPERTS - 1
    v = jnp.minimum(jnp.arange(n_visits, dtype=jnp.int32), n_vis - 1)
    vis_e = jnp.sum((vis_end[None, :] <= v[:, None]).astype(jnp.int32), axis=1)
    vis_blk = first_blk[vis_e] + (v - vis_start[vis_e])
    vis_lo = jnp.maximum(starts[vis_e] - vis_blk * MOE_ROWS, 0)
    return (dest.reshape(n_tok, TOP_K), sorted_tok, vis_blk.astype(jnp.int32), vis_e.astype(jnp.int32),
            vis_lo.astype(jnp.int32), n_vis.astype(jnp.int32).reshape(1))


def _combine_kernel(yg_ref, gate_ref, h_ref, gf_ref, fg_ref, o_ref, *, final):
    gate = gate_ref[...]
    acc = yg_ref[0] * gate[:, 0:1]
    for k in range(1, TOP_K):
        acc = acc + yg_ref[k] * gate[:, k:k + 1]
    hn = h_ref[...] + gf_ref[...] * acc
    if final:
        ms = jnp.mean(hn * hn, axis=-1, keepdims=True)
        hn = hn * lax.rsqrt(ms + NORM_EPS) * fg_ref[...]
    o_ref[...] = hn


def _combine(yg, gates, h, gf, final_g, *, final, tm):
    B, T, D = h.shape
    bm = _bmap(gf)
    kern = functools.partial(_combine_kernel, final=final)
    return pl.pallas_call(
        kern,
        out_shape=jax.ShapeDtypeStruct((B, T, D), F32),
        grid=(B, T // tm),
        in_specs=[
            pl.BlockSpec((TOP_K, None, tm, D), lambda b, s: (0, b, s, 0)),
            pl.BlockSpec((None, tm, TOP_K), lambda b, s: (b, s, 0)),
            pl.BlockSpec((None, tm, D), lambda b, s: (b, s, 0)),
            pl.BlockSpec((None, 1, D), lambda b, s: (bm(b), 0, 0)),
            pl.BlockSpec((1, D), lambda b, s: (0, 0)),
        ],
        out_specs=pl.BlockSpec((None, tm, D), lambda b, s: (b, s, 0)),
        compiler_params=_cparams(2, 48),
        name="moe_combine",
    )(yg, gates, h, gf, final_g)


def _take_rows(a, idx):
    return a.at[idx].get(mode="promise_in_bounds")


def _block_diag(w):
    G, c, d = w.shape
    eye = jnp.eye(G, dtype=w.dtype)
    return (eye[:, None, :, None] * w[:, :, None, :]).reshape(G * c, G * d)


def kernel(x, c, ctx, c_ctx, ada_w, ada_b, norm_mix_g, norm_ffn_g, ev_w_in, ev_w_out, pool_w, pool_scale,
           att_sink, od_w_in, od_w_out, rw_mu, rw_w0, rw_w2, rw_a0, rw_a2, rw_g2, rw_k_k, rw_k_a, rw_r_k,
           rw_gn_g, rw_gn_b, router_w, router_b, exp_w_gu, exp_b_gu, exp_w_dn, exp_b_dn, final_g):
    B, S, D = x.shape
    C = ctx.shape[1]
    depth = ada_w.shape[0]
    DE = exp_w_dn.shape[2]

    n_mod = -(-(B + 1) // SUBLANES) * SUBLANES
    cvec = jnp.concatenate([c, c_ctx[None, :], jnp.zeros((n_mod - B - 1, D), F32)], axis=0)
    mods = _ada_all(cvec, ada_w, ada_b)

    cos_t, sin_t = _rope_tables(S)
    dft_x = dft_c = cs64 = scan_masks = head_blk = None
    if depth > 1:
        dft_x, dft_c, cs64 = _dft_tables(S), _dft_tables(C), _channel_dft()
        scan_masks, head_blk = _scan_masks()

    tm_x = 512 if S % 512 == 0 else S
    tm_o = 256 if S % 256 == 0 else S

    h, hc = x, ctx
    for i in range(depth):
        last = i == depth - 1
        j = i // 2
        m6 = mods[i].reshape(n_mod, 6, D)
        mx = [m6[:B, k][:, None, :] for k in range(6)]
        mc = [m6[B:B + 1, k][:, None, :] for k in range(6)]
        g_mix = norm_mix_g[i][None, :]
        g_ffn = norm_ffn_g[i][None, :]
        rw = router_w[i].T
        rb = router_b[i][:, None]

        if i % 2 == 0:
            assert not last, "an even final layer is not part of this block"
            w_in = ev_w_in[j].astype(BF16)
            w_out = ev_w_out[j].astype(BF16)
            pw_bd = _block_diag(pool_w[j]).astype(BF16)
            ps = pool_scale[j][None, :]
            sink = att_sink[j]
            pool_ux, qx, kvx = _even_in(h, mx[0], mx[1], g_mix, w_in, cos_t, sin_t, rope=True, tm=tm_x)
            pool_uc, qc, kvc = _even_in(hc, mc[0], mc[1], g_mix, w_in, cos_t[:C], sin_t[:C], rope=False, tm=C)
            h, fx_x, ti_x, tg_x = _even_mix(sink, qx, kvx, kvc, pool_ux, h, mx[2], mx[3], mx[4], g_ffn, w_out,
                                      pw_bd, ps, rw, rb, local=True)
            hc, fx_c, ti_c, tg_c = _even_mix(sink, qc, None, kvc, pool_uc, hc, mc[2], mc[3], mc[4], g_ffn, w_out,
                                       pw_bd, ps, rw, rb, local=False)
        else:
            w_in = od_w_in[j].astype(BF16)
            w_out = od_w_out[j].astype(BF16)
            od_args = (w_in, rw_mu[j], rw_w0[j], rw_w2[j].astype(BF16), rw_a0[j], rw_a2[j].astype(BF16),
                       rw_k_k[j][None, :], rw_k_a[j][None, :], rw_r_k[j].reshape(1, RWKV_WIDTH))
            rx, vx, kkx, bonus_x, lwx, kdx, bdx, gcx, ux = _odd_in(h, mx[0], mx[1], g_mix, *od_args, tm=tm_o)
            rc, vc, kkc, bonus_c, lwc, kdc, bdc, gcc, uc = _odd_in(hc, mc[0], mc[1], g_mix, *od_args, tm=C)
            s0 = jnp.zeros(_scan_state_shape(B), F32)
            yf_c, yb_c, s_c = _scan(rc, vc, kkc, lwc, kdc, bdc, s0, scan_masks, head_blk)
            yf_x, yb_x, _ = _scan(rx, vx, kkx, lwx, kdx, bdx, s_c, scan_masks, head_blk)
            ro_args = (g_ffn, rw_gn_g[j][None, :], rw_gn_b[j][None, :], rw_g2[j].astype(BF16), w_out, rw, rb)
            h, fx_x, ti_x, tg_x = _odd_out(yf_x, yb_x, bonus_x, gcx, ux, dft_x, cs64, h, mx[2], mx[3], mx[4],
                                           *ro_args, tm=tm_o)
            if not last:
                hc, fx_c, ti_c, tg_c = _odd_out(yf_c, yb_c, bonus_c, gcc, uc, dft_c, cs64, hc, mc[2], mc[3], mc[4],
                                                *ro_args, tm=C)

        n_x = B * S
        if last:
            fx_all = fx_x.reshape(n_x, D)
            ti_all = ti_x.transpose(0, 2, 1).reshape(n_x, TOP_K)
        else:
            fx_all = jnp.concatenate([fx_x.reshape(n_x, D), fx_c.reshape(B * C, D)], axis=0)
            ti_all = jnp.concatenate([ti_x.transpose(0, 2, 1).reshape(n_x, TOP_K),
                                      ti_c.transpose(0, 2, 1).reshape(B * C, TOP_K)], axis=0)
        dest, sorted_tok, vis_blk, vis_e, vis_lo, n_vis = _route(ti_all)
        xg = _take_rows(fx_all, sorted_tok)
        wg, wl = _split_gu(exp_w_gu, i)
        bg = exp_b_gu[i][:, None, 0::2]
        bl = exp_b_gu[i][:, None, 1::2]
        bd = exp_b_dn[i][:, None, :]
        y = _moe_experts(vis_blk, vis_e, vis_lo, n_vis, xg, wg, wl, bg, bl, exp_w_dn, i, bd)
        tm_cx = 256 if S % 256 == 0 else S
        yg_x = _take_rows(y, dest[:n_x].T.reshape(TOP_K, B, S))
        h = _combine(yg_x, tg_x.transpose(0, 2, 1), h, mx[5], final_g[None, :], final=last, tm=tm_cx)
        if not last:
            yg_c = _take_rows(y, dest[n_x:].T.reshape(TOP_K, B, C))
            hc = _combine(yg_c, tg_c.transpose(0, 2, 1), hc, mc[5], final_g[None, :], final=False, tm=C)
    return h
```

```python
import functools
import math

import jax
import jax.numpy as jnp
import numpy as np
from jax import lax
from jax.experimental import pallas as pl
from jax.experimental.pallas import tpu as pltpu

F32 = jnp.float32
BF16 = jnp.bfloat16

GRID_W = 64
HEAD_DIM = 64
ROT_FREQS = HEAD_DIM // 4
ROPE_THETA = 10000.0
NORM_EPS = 1e-5
POOL_WINDOWS = (2, 4, 8, 16)
POOL_GROUP = 64
POOL_WIDTH = POOL_GROUP * len(POOL_WINDOWS)
POOL_HALO = max(POOL_WINDOWS) // 2
ATT_HEADS = 12
ATT_KV_HEADS = 3
ATT_GROUP = ATT_HEADS // ATT_KV_HEADS
ATT_WINDOW = 128
ATT_BLOCK = 128
ATT_Q = ATT_HEADS * HEAD_DIM
ATT_KV = ATT_KV_HEADS * HEAD_DIM
EVEN_IN = POOL_WIDTH + ATT_Q + 2 * ATT_KV
RWKV_HEADS = 12
RWKV_WIDTH = RWKV_HEADS * HEAD_DIM
DECAY_LORA = 64
ICLR_LORA = 64
GATE_LORA = 160
GN_EPS = 64e-5
LORA_LO = 3 * RWKV_WIDTH
STATE_HI = LORA_LO + 2 * DECAY_LORA + 2 * ICLR_LORA
RWKV_IN = STATE_HI + GATE_LORA
FNET_GROUPS = 4
FNET_GROUP = 64
FNET_WIDTH = FNET_GROUPS * FNET_GROUP
ODD_IN = RWKV_IN + FNET_WIDTH
N_EXPERTS = 32
TOP_K = 4
SWIGLU_LIMIT = 7.0
SWIGLU_ALPHA = 1.702

LANES = 128
SUBLANES = 8
SCAN_CHUNK = 64
SCAN_GROUP = 4
SPLIT_ROWS = 256
MOE_ROWS = 512
MASK_NEG = -1e30


def _cparams(n_axes, vmem_mb):
    return pltpu.CompilerParams(
        dimension_semantics=("arbitrary",) * n_axes,
        vmem_limit_bytes=vmem_mb * 1024 * 1024,
    )


def _dot(a, b):
    return jnp.dot(a, b, preferred_element_type=F32)


def _dot_nt(a, b):
    return lax.dot_general(a, b, (((1,), (1,)), ((), ())), preferred_element_type=F32)


def _dot_tn(a, b):
    return lax.dot_general(a, b, (((0,), (0,)), ((), ())), preferred_element_type=F32)


def _split(x):
    hi = x.astype(BF16)
    lo = (x - hi.astype(F32)).astype(BF16)
    return hi, lo


def _dot3(a, b):
    ah, al = _split(a)
    bh, bl = _split(b)
    return _dot(ah, bh) + (_dot(al, bh) + _dot(ah, bl))


def _dot_exact_lhs(a_exact, b):
    a16 = a_exact.astype(BF16)
    b1 = b.astype(BF16)
    r1 = b - b1.astype(F32)
    b2 = r1.astype(BF16)
    b3 = (r1 - b2.astype(F32)).astype(BF16)
    return _dot(a16, b1) + (_dot(a16, b2) + _dot(a16, b3))


def _modnorm(x, g, sh, sc):
    ms = jnp.mean(x * x, axis=-1, keepdims=True)
    xn = x * lax.rsqrt(ms + NORM_EPS) * g
    return xn * (1.0 + sc) + sh


def _sigmoid(x):
    return 1.0 / (1.0 + jnp.exp(-x))


def _head_sum(x):
    n = x.shape[1] // LANES
    lane = lax.broadcasted_iota(jnp.int32, (x.shape[0], LANES), 1)
    lo_mask = lane < HEAD_DIM
    parts = []
    for c in range(n):
        xc = x[:, c * LANES:(c + 1) * LANES]
        s_lo = jnp.sum(jnp.where(lo_mask, xc, 0.0), axis=-1, keepdims=True)
        s_hi = jnp.sum(jnp.where(lo_mask, 0.0, xc), axis=-1, keepdims=True)
        parts.append(jnp.where(lo_mask, s_lo, s_hi))
    return jnp.concatenate(parts, axis=1)


def _ada_kernel(c_ref, w_ref, b_ref, o_ref):
    x = c_ref[...]
    x = x * _sigmoid(x)
    o_ref[...] = _dot3(x, w_ref[...]) + b_ref[...]


def _ada_all(cvec, ada_w, ada_b):
    depth, d, n6 = ada_w.shape
    r = cvec.shape[0]
    tn = 1536 if n6 % 1536 == 0 else n6
    return pl.pallas_call(
        _ada_kernel,
        out_shape=jax.ShapeDtypeStruct((depth, r, n6), F32),
        grid=(depth, n6 // tn),
        in_specs=[
            pl.BlockSpec((r, d), lambda i, j: (0, 0)),
            pl.BlockSpec((None, d, tn), lambda i, j: (i, 0, j)),
            pl.BlockSpec((None, 1, tn), lambda i, j: (i, 0, j)),
        ],
        out_specs=pl.BlockSpec((None, r, tn), lambda i, j: (i, 0, j)),
        compiler_params=_cparams(2, 48),
        name="ada_mod",
    )(cvec, ada_w, ada_b.reshape(depth, 1, n6))


def _bmap(arr):
    if arr.shape[0] == 1:
        return lambda b: 0
    return lambda b: b


def _even_in_kernel(h_ref, sh_ref, sc_ref, g_ref, w_ref, cos_ref, sin_ref,
                    pool_ref, q_ref, kv_ref, *, rope):
    a = _modnorm(h_ref[...], g_ref[...], sh_ref[...], sc_ref[...]).astype(BF16)
    px = _dot(a, w_ref[...])
    pool_ref[...] = px[:, :POOL_WIDTH]
    n_chunks = (ATT_Q + 2 * ATT_KV) // LANES
    n_full = (ATT_Q + ATT_KV) // LANES
    outs = []
    if rope:
        lane = lax.broadcasted_iota(jnp.int32, (px.shape[0], LANES), 1)
        first = (lane % (2 * ROT_FREQS)) < ROT_FREQS
    for c in range(n_chunks):
        x = px[:, POOL_WIDTH + c * LANES:POOL_WIDTH + (c + 1) * LANES]
        if rope and c <= n_full:
            t0 = 0 if c < n_full else LANES
            cs = cos_ref[:, t0:t0 + LANES]
            sn = sin_ref[:, t0:t0 + LANES]
            rot = jnp.where(first, pltpu.roll(x, LANES - ROT_FREQS, 1), pltpu.roll(x, ROT_FREQS, 1))
            x = x * cs + rot * sn
        if c < ATT_Q // LANES:
            x = x * (HEAD_DIM ** -0.5)
        outs.append(x.astype(BF16))
    nq = ATT_Q // LANES
    q_ref[...] = jnp.concatenate(outs[:nq], axis=1)
    kv_ref[...] = jnp.concatenate(outs[nq:], axis=1)


def _even_in(h, sh, sc, g, w_bf, cos_t, sin_t, *, rope, tm):
    B, T, D = h.shape
    nt = T // tm
    bm = _bmap(sh)
    kern = functools.partial(_even_in_kernel, rope=rope)
    return pl.pallas_call(
        kern,
        out_shape=(
            jax.ShapeDtypeStruct((B, T, POOL_WIDTH), F32),
            jax.ShapeDtypeStruct((B, T, ATT_Q), BF16),
            jax.ShapeDtypeStruct((B, T, 2 * ATT_KV), BF16),
        ),
        grid=(nt, B),
        in_specs=[
            pl.BlockSpec((None, tm, D), lambda s, b: (b, s, 0)),
            pl.BlockSpec((None, 1, D), lambda s, b: (bm(b), 0, 0)),
            pl.BlockSpec((None, 1, D), lambda s, b: (bm(b), 0, 0)),
            pl.BlockSpec((1, D), lambda s, b: (0, 0)),
            pl.BlockSpec((D, EVEN_IN), lambda s, b: (0, 0)),
            pl.BlockSpec((tm, 2 * LANES), lambda s, b: (s, 0)),
            pl.BlockSpec((tm, 2 * LANES), lambda s, b: (s, 0)),
        ],
        out_specs=(
            pl.BlockSpec((None, tm, POOL_WIDTH), lambda s, b: (b, s, 0)),
            pl.BlockSpec((None, tm, ATT_Q), lambda s, b: (b, s, 0)),
            pl.BlockSpec((None, tm, 2 * ATT_KV), lambda s, b: (b, s, 0)),
        ),
        compiler_params=_cparams(2, 48),
        name="even_in",
    )(h, sh, sc, g, w_bf, cos_t, sin_t)


def _rope_tables(T):
    t = jnp.arange(T, dtype=jnp.int32)
    row = (t // GRID_W).astype(F32)
    col = (t % GRID_W).astype(F32)
    inv_freq = ROPE_THETA ** (-jnp.arange(ROT_FREQS, dtype=F32) / ROT_FREQS)
    ang_r = row[:, None] * inv_freq
    ang_c = col[:, None] * inv_freq
    cos_h = jnp.concatenate([jnp.cos(ang_r)] * 2 + [jnp.cos(ang_c)] * 2, axis=1)
    sin_h = jnp.concatenate([-jnp.sin(ang_r), jnp.sin(ang_r), -jnp.sin(ang_c), jnp.sin(ang_c)], axis=1)
    one = jnp.ones_like(cos_h)
    zero = jnp.zeros_like(sin_h)
    cos_t = jnp.concatenate([cos_h, cos_h, cos_h, one], axis=1)
    sin_t = jnp.concatenate([sin_h, sin_h, sin_h, zero], axis=1)
    return cos_t, sin_t


def _residual_and_router(h, y, gm, gf_norm, shf, scf, rw, rb, hn_ref, fx_ref, ti_ref, tg_ref):
    hn = h + gm * y
    hn_ref[...] = hn
    fx = _modnorm(hn, gf_norm, shf, scf)
    fx_ref[...] = fx.astype(BF16)
    fh, fl = _split(fx)
    wh, wl = _split(rw)
    logits = _dot_nt(wh, fh) + (_dot_nt(wl, fh) + _dot_nt(wh, fl)) + rb
    rows = logits.shape[1]
    eidx = lax.broadcasted_iota(jnp.int32, (N_EXPERTS, rows), 0).astype(F32)
    vals, idxs = [], []
    for _ in range(TOP_K):
        mx = jnp.max(logits, axis=0, keepdims=True)
        ix = jnp.min(jnp.where(logits == mx, eidx, float(N_EXPERTS)), axis=0, keepdims=True)
        vals.append(mx)
        idxs.append(ix)
        logits = jnp.where(eidx == ix, MASK_NEG, logits)
    es = [jnp.exp(v - vals[0]) for v in vals]
    den = es[0]
    for e in es[1:]:
        den = den + e
    tg_ref[...] = jnp.concatenate(es, axis=0) / den
    ti_ref[...] = jnp.concatenate(idxs, axis=0).astype(jnp.int32)


def _even_mix_kernel(*refs, local, tq, T):
    it = iter(refs)
    sink_ref = next(it)
    q_ref = next(it)
    if local:
        kvp_ref, kvc_ref, kvn_ref, bias_ref = next(it), next(it), next(it), next(it)
    ckv_ref = next(it)
    up_ref, uc_ref, un_ref = next(it), next(it), next(it)
    h_ref, gm_ref, shf_ref, scf_ref, gfn_ref = next(it), next(it), next(it), next(it), next(it)
    wout_ref, pw_ref, ps_ref, rw_ref, rb_ref = next(it), next(it), next(it), next(it), next(it)
    hn_ref, fx_ref, ti_ref, tg_ref = next(it), next(it), next(it), next(it)

    j = pl.program_id(1)
    nb = pl.num_programs(1)

    cols = ATT_GROUP * tq
    col_head = lax.broadcasted_iota(jnp.int32, (1, cols), 1) // tq
    q = q_ref[...]
    ckv = ckv_ref[...]
    if local:
        kvl = jnp.concatenate([kvp_ref[...], kvc_ref[...], kvn_ref[...]], axis=0)
        bias = bias_ref[...]
    att_t = []
    for g in range(ATT_KV_HEADS):
        qs = jnp.concatenate(
            [q[:, (g * ATT_GROUP + i) * HEAD_DIM:(g * ATT_GROUP + i + 1) * HEAD_DIM] for i in range(ATT_GROUP)],
            axis=0)
        kc = ckv[:, g * HEAD_DIM:(g + 1) * HEAD_DIM]
        vc = ckv[:, ATT_KV + g * HEAD_DIM:ATT_KV + (g + 1) * HEAD_DIM]
        s_ctx = _dot_nt(kc, qs)
        sink = jnp.zeros((1, cols), F32)
        for i in range(ATT_GROUP):
            sink = jnp.where(col_head == i, sink_ref[g * ATT_GROUP + i], sink)
        if local:
            kl = kvl[:, g * HEAD_DIM:(g + 1) * HEAD_DIM]
            vl = kvl[:, ATT_KV + g * HEAD_DIM:ATT_KV + (g + 1) * HEAD_DIM]
            s_loc = _dot_nt(kl, qs) + bias
            m = jnp.maximum(jnp.maximum(jnp.max(s_loc, axis=0, keepdims=True),
                                        jnp.max(s_ctx, axis=0, keepdims=True)), sink)
            p_loc = jnp.exp(s_loc - m)
            p_ctx = jnp.exp(s_ctx - m)
            den = (jnp.sum(p_loc, axis=0, keepdims=True) + jnp.sum(p_ctx, axis=0, keepdims=True)
                   + jnp.exp(sink - m))
            o_t = _dot_tn(vl, p_loc.astype(BF16)) + _dot_tn(vc, p_ctx.astype(BF16))
        else:
            m = jnp.maximum(jnp.max(s_ctx, axis=0, keepdims=True), sink)
            p_ctx = jnp.exp(s_ctx - m)
            den = jnp.sum(p_ctx, axis=0, keepdims=True) + jnp.exp(sink - m)
            o_t = _dot_tn(vc, p_ctx.astype(BF16))
        o_t = o_t / den
        for i in range(ATT_GROUP):
            att_t.append(o_t[:, i * tq:(i + 1) * tq])
    att_xt = jnp.concatenate(att_t, axis=0).astype(BF16)

    u = uc_ref[...]
    up = jnp.where(j > 0, up_ref[...], 0.0)
    un = jnp.where(j < nb - 1, un_ref[...], 0.0)
    e = jnp.concatenate([up, u, un], axis=0)
    n_e = tq + 2 * POOL_HALO

    def shifted(x, k):
        return pltpu.roll(x, k % n_e, 0)

    a1 = e + shifted(e, 1)
    a2 = shifted(a1, 1) + shifted(a1, -1)
    a3 = shifted(a2, 2) + shifted(a2, -2)
    a4 = shifted(a3, 4) + shifted(a3, -4)
    lane = lax.broadcasted_iota(jnp.int32, (n_e, POOL_WIDTH), 1)
    grp = lane // POOL_GROUP
    win = jnp.where(grp == 0, a1, jnp.where(grp == 1, a2, jnp.where(grp == 2, a3, a4)))
    win = win[POOL_HALO:POOL_HALO + tq]
    t_i = j * tq + lax.broadcasted_iota(jnp.int32, (tq, POOL_WIDTH), 0)
    half = jnp.left_shift(1, lax.broadcasted_iota(jnp.int32, (tq, POOL_WIDTH), 1) // POOL_GROUP)
    cnt = (jnp.minimum(t_i + half, T) - jnp.maximum(t_i - half, 0)).astype(F32)
    pooled = win / cnt - u
    pool_x = _dot(pooled.astype(BF16), pw_ref[...]) * ps_ref[...]

    y = _dot(pool_x.astype(BF16), wout_ref[:POOL_WIDTH, :]) + _dot_tn(att_xt, wout_ref[POOL_WIDTH:, :])
    _residual_and_router(h_ref[...], y, gm_ref[...], gfn_ref[...], shf_ref[...], scf_ref[...],
                         rw_ref[...], rb_ref[...], hn_ref, fx_ref, ti_ref, tg_ref)


def _band_bias(tq, nb):
    c = np.arange(3 * ATT_BLOCK)[:, None]
    qq = (np.arange(ATT_GROUP * tq) % tq)[None, :]
    k_rel = c - ATT_BLOCK
    out = []
    for first, last in ((True, nb == 1), (False, False), (nb == 1, True)):
        ok = np.abs(qq - k_rel) <= ATT_WINDOW
        if first:
            ok = ok & (k_rel >= 0)
        if last:
            ok = ok & (k_rel < tq)
        out.append(np.where(ok, 0.0, MASK_NEG))
    return jnp.asarray(np.stack(out), F32)


def _even_mix(sink, q, kv, ckv, pool_u, h, gm, shf, scf, gfn, wout, pw_bd, pscale, rw, rb, *, local):
    B, T, D = h.shape
    n_ctx = ckv.shape[1]
    tq = ATT_BLOCK if local else T
    nb = T // tq
    r8 = tq // SUBLANES
    n8 = T // SUBLANES
    bm = _bmap(gm)
    kern = functools.partial(_even_mix_kernel, local=local, tq=tq, T=T)
    vec = lambda: pl.BlockSpec((None, 1, D), lambda b, j: (bm(b), 0, 0))
    full = lambda a: pl.BlockSpec(a.shape, lambda b, j: (0,) * a.ndim)
    in_specs = [pl.BlockSpec(memory_space=pltpu.SMEM),
                pl.BlockSpec((None, tq, ATT_Q), lambda b, j: (b, j, 0))]
    args = [sink, q]
    if local:
        in_specs += [
            pl.BlockSpec((None, ATT_BLOCK, 2 * ATT_KV), lambda b, j: (b, jnp.maximum(j - 1, 0), 0)),
            pl.BlockSpec((None, ATT_BLOCK, 2 * ATT_KV), lambda b, j: (b, j, 0)),
            pl.BlockSpec((None, ATT_BLOCK, 2 * ATT_KV), lambda b, j: (b, jnp.minimum(j + 1, nb - 1), 0)),
        ]
        in_specs.append(pl.BlockSpec((None, 3 * ATT_BLOCK, ATT_GROUP * tq),
                                     lambda b, j: (jnp.where(j == 0, 0, jnp.where(j == nb - 1, 2, 1)), 0, 0)))
        args += [kv, kv, kv, _band_bias(tq, nb)]
    in_specs += [
        pl.BlockSpec((None, n_ctx, 2 * ATT_KV), lambda b, j: (b, 0, 0)),
        pl.BlockSpec((None, SUBLANES, POOL_WIDTH), lambda b, j: (b, jnp.maximum(j * r8 - 1, 0), 0)),
        pl.BlockSpec((None, tq, POOL_WIDTH), lambda b, j: (b, j, 0)),
        pl.BlockSpec((None, SUBLANES, POOL_WIDTH), lambda b, j: (b, jnp.minimum((j + 1) * r8, n8 - 1), 0)),
        pl.BlockSpec((None, tq, D), lambda b, j: (b, j, 0)),
        vec(), vec(), vec(), full(gfn), full(wout), full(pw_bd), full(pscale), full(rw), full(rb),
    ]
    args += [ckv, pool_u, pool_u, pool_u, h, gm, shf, scf, gfn, wout, pw_bd, pscale, rw, rb]
    return pl.pallas_call(
        kern,
        out_shape=(
            jax.ShapeDtypeStruct((B, T, D), F32),
            jax.ShapeDtypeStruct((B, T, D), BF16),
            jax.ShapeDtypeStruct((B, TOP_K, T), jnp.int32),
            jax.ShapeDtypeStruct((B, TOP_K, T), F32),
        ),
        grid=(B, nb),
        in_specs=in_specs,
        out_specs=(
            pl.BlockSpec((None, tq, D), lambda b, j: (b, j, 0)),
            pl.BlockSpec((None, tq, D), lambda b, j: (b, j, 0)),
            pl.BlockSpec((None, TOP_K, tq), lambda b, j: (b, 0, j)),
            pl.BlockSpec((None, TOP_K, tq), lambda b, j: (b, 0, j)),
        ),
        compiler_params=_cparams(2, 48),
        name="even_mix_local" if local else "even_mix_ctx",
    )(*args)


def _odd_in_kernel(h_ref, hp_ref, hn_ref, sh_ref, sc_ref, g_ref, w_ref, mu_ref, w0_ref, w2_ref,
                   a0_ref, a2_ref, kk_ref, ka_ref, rk_ref,
                   r_out, v_out, kkn_out, bonus_out, lw_out, kd_out, bd_out, gc_out, fn_out, *, tm):
    s = pl.program_id(1)
    ns = pl.num_programs(1)
    g, sh, sc = g_ref[...], sh_ref[...], sc_ref[...]
    w = w_ref[...]
    a = _modnorm(h_ref[...], g, sh, sc).astype(BF16)
    px = _dot(a, w)
    fn_out[...] = px[:, RWKV_IN:]
    main = px[:, :RWKV_IN]
    ap = _modnorm(hp_ref[...], g, sh, sc).astype(BF16)
    an = _modnorm(hn_ref[...], g, sh, sc).astype(BF16)
    halo = _dot(jnp.concatenate([ap, an], axis=0), w[:, :RWKV_IN])
    prev_row = jnp.where(s > 0, halo[SUBLANES - 1:SUBLANES], 0.0)
    next_row = jnp.where(s < ns - 1, halo[SUBLANES:SUBLANES + 1], 0.0)
    row = lax.broadcasted_iota(jnp.int32, (tm, 1), 0)
    prev = jnp.where(row == 0, prev_row, pltpu.roll(main, 1, 0))
    nxt = jnp.where(row == tm - 1, next_row, pltpu.roll(main, tm - 1, 0))
    mu = mu_ref[...]
    fs = main + mu[0:1] * (prev - main) + mu[1:2] * (nxt - main)

    W = RWKV_WIDTH
    r = fs[:, :W]
    k = fs[:, W:2 * W]
    v = fs[:, 2 * W:3 * W]
    lora = fs[:, LORA_LO:STATE_HI]
    gc_out[...] = fs[:, STATE_HI:RWKV_IN]
    r_out[...] = r
    v_out[...] = v

    kx = k * kk_ref[...]
    nrm = jnp.sqrt(_head_sum(kx * kx))
    kkn = kx / jnp.maximum(nrm, 1e-12)
    kkn_out[...] = kkn
    ka = ka_ref[...]
    ksum = None
    for d in range(2):
        wd = lora[:, d * DECAY_LORA:(d + 1) * DECAY_LORA]
        o_a = 2 * DECAY_LORA
        ad = lora[:, o_a + d * ICLR_LORA:o_a + (d + 1) * ICLR_LORA]
        xw = _dot(jnp.tanh(wd).astype(BF16), w2_ref[d]) + w0_ref[d:d + 1]
        z = -xw
        softplus = jnp.maximum(z, 0.0) + jnp.log(1.0 + jnp.exp(-jnp.abs(z)))
        w_log = -softplus - 0.5
        lw_out[d] = -jnp.exp(w_log)
        xa = _dot(ad.astype(BF16), a2_ref[d]) + a0_ref[d:d + 1]
        a_d = _sigmoid(xa)
        k_d = k * (1.0 + (a_d - 1.0) * ka)
        kd_out[d] = k_d
        bd_out[d] = kkn * a_d
        ksum = k_d if ksum is None else ksum + k_d
    coef = _head_sum(r * ksum * rk_ref[...])
    bonus_out[...] = coef * v


def _odd_in(h, sh, sc, g, w_bf, mu, w0, w2_bf, a0, a2_bf, k_k, k_a, r_k, *, tm):
    B, T, D = h.shape
    ns = T // tm
    r8 = tm // SUBLANES
    n8 = T // SUBLANES
    bm = _bmap(sh)
    W = RWKV_WIDTH
    full = lambda a: pl.BlockSpec(a.shape, lambda b, s: (0,) * a.ndim)
    tok = lambda n: pl.BlockSpec((None, tm, n), lambda b, s: (b, s, 0))
    tok2 = lambda n: pl.BlockSpec((2, None, tm, n), lambda b, s: (0, b, s, 0))
    kern = functools.partial(_odd_in_kernel, tm=tm)
    return pl.pallas_call(
        kern,
        out_shape=(
            jax.ShapeDtypeStruct((B, T, W), F32),
            jax.ShapeDtypeStruct((B, T, W), F32),
            jax.ShapeDtypeStruct((B, T, W), F32),
            jax.ShapeDtypeStruct((B, T, W), F32),
            jax.ShapeDtypeStruct((2, B, T, W), F32),
            jax.ShapeDtypeStruct((2, B, T, W), F32),
            jax.ShapeDtypeStruct((2, B, T, W), F32),
            jax.ShapeDtypeStruct((B, T, GATE_LORA), F32),
            jax.ShapeDtypeStruct((B, T, FNET_WIDTH), F32),
        ),
        grid=(B, ns),
        in_specs=[
            pl.BlockSpec((None, tm, D), lambda b, s: (b, s, 0)),
            pl.BlockSpec((None, SUBLANES, D), lambda b, s: (b, jnp.maximum(s * r8 - 1, 0), 0)),
            pl.BlockSpec((None, SUBLANES, D), lambda b, s: (b, jnp.minimum((s + 1) * r8, n8 - 1), 0)),
            pl.BlockSpec((None, 1, D), lambda b, s: (bm(b), 0, 0)),
            pl.BlockSpec((None, 1, D), lambda b, s: (bm(b), 0, 0)),
            full(g), full(w_bf), full(mu), full(w0), full(w2_bf), full(a0), full(a2_bf),
            full(k_k), full(k_a), full(r_k),
        ],
        out_specs=(tok(W), tok(W), tok(W), tok(W), tok2(W), tok2(W), tok2(W), tok(GATE_LORA), tok(FNET_WIDTH)),
        compiler_params=_cparams(2, 60),
        name="odd_in",
    )(h, h, h, sh, sc, g, w_bf, mu, w0, w2_bf, a0, a2_bf, k_k, k_a, r_k)


SCAN_LEVELS = tuple(2 ** k for k in range(int(math.log2(SCAN_CHUNK))))
MASK_BEFORE, MASK_UPTO, MASK_LEVEL0 = 0, 1, 2


def _scan_masks():
    R = SCAN_GROUP * SCAN_CHUNK
    idx = np.arange(R)
    same = (idx[:, None] // SCAN_CHUNK) == (idx[None, :] // SCAN_CHUNK)
    t = (idx % SCAN_CHUNK)[:, None]
    s = (idx % SCAN_CHUNK)[None, :]
    out = []
    for sign in (1, -1):
        order = (t - s) * sign
        ms = [same & (order > 0), same & (order >= 0)]
        for m in SCAN_LEVELS:
            ms.append(same & (t // (2 * m) == s // (2 * m)) & ((((t // m) % 2) - ((s // m) % 2)) * sign == 1))
        out.append(np.stack(ms))
    return jnp.asarray(np.stack(out), F32), jnp.asarray(same, F32)


def _scan_kernel(rf_ref, vf_ref, kkf_ref, rb_ref, vb_ref, kkb_ref, lwf_ref, kdf_ref, bdf_ref,
                 lwb_ref, kdb_ref, bdb_ref, mask_ref, hb_ref, s0_ref, yf_ref, yb_ref, sT_ref, s_scr, *, L):
    c = pl.program_id(1)
    nc = pl.num_programs(1)
    R = SCAN_GROUP * L
    GW = SCAN_GROUP * HEAD_DIM

    @pl.when(c == 0)
    def _():
        s_scr[...] = s0_ref[...]

    head_blk = hb_ref[...]

    def wide(x):
        return (jnp.concatenate([x] * SCAN_GROUP, axis=0) * head_blk).astype(BF16)

    NG = RWKV_HEADS // SCAN_GROUP
    dirs = ((0, (rf_ref, vf_ref, kkf_ref, lwf_ref, kdf_ref, bdf_ref)),
            (1, (rb_ref, vb_ref, kkb_ref, lwb_ref, kdb_ref, bdb_ref)))
    wides = {k: [] for k in ("kt", "rt", "kh", "bh", "v", "ke", "be")}
    g_tots = []
    for d, (r_ref, v_ref, kk_ref, lw_ref, kd_ref, bd_ref) in dirs:
        lw = lw_ref[...]
        c_in = _dot_exact_lhs(mask_ref[d, MASK_UPTO][:L, :L], lw)
        c_ex = c_in - lw
        tot = jnp.sum(lw, axis=0, keepdims=True)
        kd = kd_ref[...]
        bd = bd_ref[...]
        inv = jnp.exp(-c_in)
        end = jnp.exp(tot - c_in)
        g_tot = jnp.exp(tot)
        cols = dict(
            kt=kk_ref[...] * jnp.exp(c_ex), rt=r_ref[...] * jnp.exp(c_in), kh=kd * inv, bh=bd * inv,
            v=v_ref[...], ke=kd * end, be=bd * end)
        for g in range(NG):
            sl = slice(g * GW, (g + 1) * GW)
            for k, a in cols.items():
                wides[k].append(wide(a[:, sl]))
            g_tots.append(jnp.broadcast_to(g_tot[:, sl], (GW, GW)))
    w = {k: jnp.stack(a) for k, a in wides.items()}
    g_tot_all = jnp.stack(g_tots)

    def masked(x, idx):
        return jnp.concatenate([x[:NG] * mask_ref[0, idx][None], x[NG:] * mask_ref[1, idx][None]], axis=0)

    bmm = lambda a, b: jnp.einsum("bij,bjk->bik", a, b, preferred_element_type=F32)
    bmm_nt = lambda a, b: jnp.einsum("bik,bjk->bij", a, b, preferred_element_type=F32)
    bmm_tn = lambda a, b: jnp.einsum("bki,bkj->bij", a, b, preferred_element_type=F32)

    lhs = jnp.concatenate([w["kt"], w["rt"]], axis=1)
    s4 = bmm_nt(lhs, jnp.concatenate([w["kh"], w["bh"]], axis=1))
    a_kk = masked(s4[:, :R, :R], MASK_BEFORE).astype(BF16)
    m = masked(s4[:, :R, R:], MASK_BEFORE)
    a_rk = masked(s4[:, R:, :R], MASK_UPTO).astype(BF16)
    a_rb = masked(s4[:, R:, R:], MASK_UPTO).astype(BF16)
    eye = (mask_ref[0, MASK_UPTO] - mask_ref[0, MASK_BEFORE])[None]
    dinv = eye - masked(m, MASK_LEVEL0)
    for li in range(1, len(SCAN_LEVELS)):
        d16 = dinv.astype(BF16)
        e16 = masked(m, MASK_LEVEL0 + li).astype(BF16)
        dinv = dinv - bmm(bmm(d16, e16).astype(BF16), d16)
    s_all = s_scr[...].reshape(2 * NG, GW, GW)
    ks = bmm_nt(lhs, s_all.astype(BF16))
    u16 = bmm(dinv.astype(BF16), (ks[:, :R] + bmm(a_kk, w["v"])).astype(BF16)).astype(BF16)
    yw = ks[:, R:] + bmm(a_rk, w["v"]) - bmm(a_rb, u16)
    y_all = yw[:, 0:L]
    for hh in range(1, SCAN_GROUP):
        y_all = y_all + yw[:, hh * L:(hh + 1) * L]
    yf_ref[...] = jnp.concatenate([y_all[g] for g in range(NG)], axis=1)
    yb_ref[...] = jnp.concatenate([y_all[NG + g] for g in range(NG)], axis=1)
    vu = jnp.concatenate([w["v"], -u16], axis=1)
    ke = jnp.concatenate([w["ke"], w["be"]], axis=1)
    s_scr[...] = (s_all * g_tot_all + bmm_tn(vu, ke)).reshape(2, NG, GW, GW)

    @pl.when(c == nc - 1)
    def _():
        sT_ref[...] = s_scr[...]


def _scan_state_shape(B):
    gw = SCAN_GROUP * HEAD_DIM
    return (2, B, RWKV_HEADS // SCAN_GROUP, gw, gw)


def _scan(r, v, kk, lw, kd, bd, s0, masks, head_blk):
    B, T, W = r.shape
    L = SCAN_CHUNK
    assert L == HEAD_DIM and T % L == 0
    nc = T // L
    st_shape = _scan_state_shape(B)
    fwd = pl.BlockSpec((None, L, W), lambda b, c: (b, c, 0))
    bwd = pl.BlockSpec((None, L, W), lambda b, c: (b, nc - 1 - c, 0))
    fwd_d = pl.BlockSpec((None, None, L, W), lambda b, c: (0, b, c, 0))
    bwd_d = pl.BlockSpec((None, None, L, W), lambda b, c: (1, b, nc - 1 - c, 0))
    state = pl.BlockSpec((2, None) + st_shape[2:], lambda b, c: (0, b, 0, 0, 0))
    full = lambda a: pl.BlockSpec(a.shape, lambda b, c: (0,) * a.ndim)
    kern = functools.partial(_scan_kernel, L=L)
    return pl.pallas_call(
        kern,
        out_shape=(
            jax.ShapeDtypeStruct((B, T, W), F32),
            jax.ShapeDtypeStruct((B, T, W), F32),
            jax.ShapeDtypeStruct(st_shape, F32),
        ),
        grid=(B, nc),
        in_specs=[fwd, fwd, fwd, bwd, bwd, bwd, fwd_d, fwd_d, fwd_d, bwd_d, bwd_d, bwd_d,
                  full(masks), full(head_blk), state],
        out_specs=(fwd, bwd, state),
        scratch_shapes=[pltpu.VMEM((2,) + st_shape[2:], F32)],
        compiler_params=_cparams(2, 56),
        name="rwkv_scan",
    )(r, v, kk, r, v, kk, lw, kd, bd, lw, kd, bd, masks, head_blk, s0)


def _odd_out_kernel(yf_ref, yb_ref, bonus_ref, gc_ref, u_ref, dft_ref, cs_ref, h_ref, gm_ref, shf_ref, scf_ref,
                    gfn_ref, gng_ref, gnb_ref, g2_ref, wout_ref, rw_ref, rb_ref,
                    hn_ref, fx_ref, ti_ref, tg_ref, ucs_scr, *, T):
    s = pl.program_id(1)

    @pl.when(s == 0)
    def _():
        t = _dot(u_ref[...].astype(BF16), cs_ref[...])
        ucs_scr[0:T, :] = t[:, :FNET_WIDTH].astype(BF16)
        ucs_scr[T:2 * T, :] = t[:, FNET_WIDTH:].astype(BF16)

    f = _dot(dft_ref[...], ucs_scr[...])
    y = yf_ref[...] + yb_ref[...]
    mean = _head_sum(y) * (1.0 / HEAD_DIM)
    yc = y - mean
    var = _head_sum(yc * yc) * (1.0 / HEAD_DIM)
    yn = yc * lax.rsqrt(var + GN_EPS) * gng_ref[...] + gnb_ref[...]
    out = yn + bonus_ref[...]
    gate = _dot(_sigmoid(gc_ref[...]).astype(BF16), g2_ref[...])
    o = out * gate
    mix = jnp.concatenate([o, f], axis=1).astype(BF16)
    ymix = _dot(mix, wout_ref[...])
    _residual_and_router(h_ref[...], ymix, gm_ref[...], gfn_ref[...], shf_ref[...], scf_ref[...],
                         rw_ref[...], rb_ref[...], hn_ref, fx_ref, ti_ref, tg_ref)


def _odd_out(yf, yb, bonus, gc, u, dft, cs64, h, gm, shf, scf, gfn, gn_g, gn_b, g2_bf, wout, rw, rb, *, tm):
    B, T, D = h.shape
    ns = T // tm
    W = RWKV_WIDTH
    bm = _bmap(gm)
    vec = lambda: pl.BlockSpec((None, 1, D), lambda b, s: (bm(b), 0, 0))
    full = lambda a: pl.BlockSpec(a.shape, lambda b, s: (0,) * a.ndim)
    kern = functools.partial(_odd_out_kernel, T=T)
    return pl.pallas_call(
        kern,
        out_shape=(
            jax.ShapeDtypeStruct((B, T, D), F32),
            jax.ShapeDtypeStruct((B, T, D), BF16),
            jax.ShapeDtypeStruct((B, TOP_K, T), jnp.int32),
            jax.ShapeDtypeStruct((B, TOP_K, T), F32),
        ),
        grid=(B, ns),
        in_specs=[
            pl.BlockSpec((None, tm, W), lambda b, s: (b, s, 0)),
            pl.BlockSpec((None, tm, W), lambda b, s: (b, s, 0)),
            pl.BlockSpec((None, tm, W), lambda b, s: (b, s, 0)),
            pl.BlockSpec((None, tm, GATE_LORA), lambda b, s: (b, s, 0)),
            pl.BlockSpec((None, T, FNET_WIDTH), lambda b, s: (b, 0, 0)),
            pl.BlockSpec((tm, 2 * T), lambda b, s: (s, 0)),
            full(cs64),
            pl.BlockSpec((None, tm, D), lambda b, s: (b, s, 0)),
            vec(), vec(), vec(), full(gfn), full(gn_g), full(gn_b), full(g2_bf), full(wout), full(rw), full(rb),
        ],
        out_specs=(
            pl.BlockSpec((None, tm, D), lambda b, s: (b, s, 0)),
            pl.BlockSpec((None, tm, D), lambda b, s: (b, s, 0)),
            pl.BlockSpec((None, TOP_K, tm), lambda b, s: (b, 0, s)),
            pl.BlockSpec((None, TOP_K, tm), lambda b, s: (b, 0, s)),
        ),
        scratch_shapes=[pltpu.VMEM((2 * T, FNET_WIDTH), BF16)],
        compiler_params=_cparams(2, 48),
        name="odd_out",
    )(yf, yb, bonus, gc, u, dft, cs64, h, gm, shf, scf, gfn, gn_g, gn_b, g2_bf, wout, rw, rb)


def _dft_tables(T):
    t = jnp.arange(T, dtype=jnp.int32)
    ang = ((t[:, None] * t[None, :]) % T).astype(F32) * (2.0 * math.pi / T)
    scale = 1.0 / math.sqrt(T * FNET_GROUP)
    return (jnp.concatenate([jnp.cos(ang), -jnp.sin(ang)], axis=1) * scale).astype(BF16)


def _channel_dft():
    c = jnp.arange(FNET_GROUP, dtype=jnp.int32)
    ang = ((c[:, None] * c[None, :]) % FNET_GROUP).astype(F32) * (2.0 * math.pi / FNET_GROUP)
    eye = jnp.eye(FNET_GROUPS, dtype=F32)
    return jnp.concatenate([jnp.kron(eye, jnp.cos(ang)), jnp.kron(eye, jnp.sin(ang))], axis=1).astype(BF16)


def _split_gu_kernel(w_ref, wg_ref, wl_ref, t_scr):
    n_slab = w_ref.shape[0] // LANES
    de = w_ref.shape[1] // 2
    for s in range(n_slab):
        t_scr[s] = w_ref[s * LANES:(s + 1) * LANES, :].T
    ev = [t_scr[s, pl.ds(0, de, stride=2), :].T for s in range(n_slab)]
    od = [t_scr[s, pl.ds(1, de, stride=2), :].T for s in range(n_slab)]
    wg_ref[...] = jnp.concatenate(ev, axis=0).astype(BF16)
    wl_ref[...] = jnp.concatenate(od, axis=0).astype(BF16)


def _split_gu(w_gu_all, layer):
    depth, E, D, de2 = w_gu_all.shape
    de = de2 // 2
    rows = E * D
    tr = SPLIT_ROWS if rows % SPLIT_ROWS == 0 else LANES
    nb = rows // tr
    wg, wl = pl.pallas_call(
        _split_gu_kernel,
        out_shape=(jax.ShapeDtypeStruct((rows, de), BF16), jax.ShapeDtypeStruct((rows, de), BF16)),
        grid=(nb,),
        in_specs=[pl.BlockSpec((tr, de2), lambda i: (layer * nb + i, 0))],
        out_specs=(pl.BlockSpec((tr, de), lambda i: (i, 0)), pl.BlockSpec((tr, de), lambda i: (i, 0))),
        scratch_shapes=[pltpu.VMEM((tr // LANES, de2, LANES), F32)],
        compiler_params=_cparams(1, 48),
        name="split_gu",
    )(w_gu_all.reshape(depth * rows, de2))
    return wg.reshape(E, D, de), wl.reshape(E, D, de)


def _moe_kernel(vb_ref, ve_ref, vlo_ref, nv_ref, x_ref, wg_ref, wl_ref, bg_ref, bl_ref, wd_ref, bd_ref, y_ref):
    v = pl.program_id(0)

    @pl.when(v < nv_ref[0])
    def _():
        x = x_ref[...]
        glu = jnp.minimum(_dot(x, wg_ref[...]) + bg_ref[...], SWIGLU_LIMIT)
        lin = jnp.clip(_dot(x, wl_ref[...]) + bl_ref[...], -SWIGLU_LIMIT, SWIGLU_LIMIT)
        act = glu * _sigmoid(SWIGLU_ALPHA * glu) * (lin + 1.0)
        y = _dot(act.astype(BF16), wd_ref[...].astype(BF16)) + bd_ref[...]
        lo = vlo_ref[v]

        @pl.when(lo == 0)
        def _():
            y_ref[...] = y.astype(BF16)

        @pl.when(lo > 0)
        def _():
            row = lax.broadcasted_iota(jnp.int32, y.shape, 0)
            y_ref[...] = jnp.where(row >= lo, y, y_ref[...].astype(F32)).astype(BF16)


def _moe_experts(vis_blk, vis_e, vis_lo, n_vis, xg, wg, wl, bg, bl, wd_all, layer, bd):
    n_rows, D = xg.shape
    E, _, DE = wg.shape
    n_visits = vis_blk.shape[0]
    wmap = lambda v, vb, ve, vlo, nv: (ve[v], 0, 0)
    dmap = lambda v, vb, ve, vlo, nv: (layer, ve[v], 0, 0)
    xmap = lambda v, vb, ve, vlo, nv: (vb[v], 0)
    grid_spec = pltpu.PrefetchScalarGridSpec(
        num_scalar_prefetch=4,
        grid=(n_visits,),
        in_specs=[
            pl.BlockSpec((MOE_ROWS, D), xmap),
            pl.BlockSpec((None, D, DE), wmap),
            pl.BlockSpec((None, D, DE), wmap),
            pl.BlockSpec((None, 1, DE), wmap),
            pl.BlockSpec((None, 1, DE), wmap),
            pl.BlockSpec((None, None, DE, D), dmap),
            pl.BlockSpec((None, 1, D), wmap),
        ],
        out_specs=pl.BlockSpec((MOE_ROWS, D), xmap),
    )
    return pl.pallas_call(
        _moe_kernel,
        out_shape=jax.ShapeDtypeStruct((n_rows, D), BF16),
        grid_spec=grid_spec,
        compiler_params=_cparams(1, 56),
        name="moe_experts",
    )(vis_blk, vis_e, vis_lo, n_vis, xg, wg, wl, bg, bl, wd_all, bd)


def _route(top_idx):
    n_tok = top_idx.shape[0]
    n_assign = n_tok * TOP_K
    assert n_assign % MOE_ROWS == 0
    n_blocks = n_assign // MOE_ROWS
    flat_e = top_idx.reshape(-1).astype(jnp.int32)
    iota = jnp.arange(n_assign, dtype=jnp.int32)
    sorted_e, order = lax.sort((flat_e, iota), num_keys=1)
    _, dest = lax.sort((order, iota), num_keys=1)
    sorted_tok = order // TOP_K
    experts = jnp.arange(N_EXPERTS + 1, dtype=jnp.int32)
    bounds = jnp.sum((sorted_e[None, :] < experts[:, None]).astype(jnp.int32), axis=1)
    starts, ends = bounds[:-1], bounds[1:]
    first_blk = starts // MOE_ROWS
    n_vis_e = jnp.where(ends > starts, (ends - 1) // MOE_ROWS - first_blk + 1, 0)
    vis_end = jnp.cumsum(n_vis_e)
    vis_start = vis_end - n_vis_e
    n_vis = vis_end[-1]
    n_visits = n_blocks + N_EXPERTS - 1
    v = jnp.minimum(jnp.arange(n_visits, dtype=jnp.int32), n_vis - 1)
    vis_e = jnp.sum((vis_end[None, :] <= v[:, None]).astype(jnp.int32), axis=1)
    vis_blk = first_blk[vis_e] + (v - vis_start[vis_e])
    vis_lo = jnp.maximum(starts[vis_e] - vis_blk * MOE_ROWS, 0)
    return (dest.reshape(n_tok, TOP_K), sorted_tok, vis_blk.astype(jnp.int32), vis_e.astype(jnp.int32),
            vis_lo.astype(jnp.int32), n_vis.astype(jnp.int32).reshape(1))


def _combine_kernel(yg_ref, gate_ref, h_ref, gf_ref, fg_ref, o_ref, *, final):
    gate = gate_ref[...]
    acc = yg_ref[0].astype(F32) * gate[:, 0:1]
    for k in range(1, TOP_K):
        acc = acc + yg_ref[k].astype(F32) * gate[:, k:k + 1]
    hn = h_ref[...] + gf_ref[...] * acc
    if final:
        ms = jnp.mean(hn * hn, axis=-1, keepdims=True)
        hn = hn * lax.rsqrt(ms + NORM_EPS) * fg_ref[...]
    o_ref[...] = hn


def _combine(yg, gates, h, gf, final_g, *, final, tm):
    B, T, D = h.shape
    bm = _bmap(gf)
    kern = functools.partial(_combine_kernel, final=final)
    return pl.pallas_call(
        kern,
        out_shape=jax.ShapeDtypeStruct((B, T, D), F32),
        grid=(B, T // tm),
        in_specs=[
            pl.BlockSpec((TOP_K, None, tm, D), lambda b, s: (0, b, s, 0)),
            pl.BlockSpec((None, tm, TOP_K), lambda b, s: (b, s, 0)),
            pl.BlockSpec((None, tm, D), lambda b, s: (b, s, 0)),
            pl.BlockSpec((None, 1, D), lambda b, s: (bm(b), 0, 0)),
            pl.BlockSpec((1, D), lambda b, s: (0, 0)),
        ],
        out_specs=pl.BlockSpec((None, tm, D), lambda b, s: (b, s, 0)),
        compiler_params=_cparams(2, 48),
        name="moe_combine",
    )(yg, gates, h, gf, final_g)


def _take_rows(a, idx):
    return a.at[idx].get(mode="promise_in_bounds")


def _block_diag(w):
    G, c, d = w.shape
    eye = jnp.eye(G, dtype=w.dtype)
    return (eye[:, None, :, None] * w[:, :, None, :]).reshape(G * c, G * d)


def kernel(x, c, ctx, c_ctx, ada_w, ada_b, norm_mix_g, norm_ffn_g, ev_w_in, ev_w_out, pool_w, pool_scale,
           att_sink, od_w_in, od_w_out, rw_mu, rw_w0, rw_w2, rw_a0, rw_a2, rw_g2, rw_k_k, rw_k_a, rw_r_k,
           rw_gn_g, rw_gn_b, router_w, router_b, exp_w_gu, exp_b_gu, exp_w_dn, exp_b_dn, final_g):
    B, S, D = x.shape
    C = ctx.shape[1]
    depth = ada_w.shape[0]
    DE = exp_w_dn.shape[2]

    n_mod = -(-(B + 1) // SUBLANES) * SUBLANES
    cvec = jnp.concatenate([c, c_ctx[None, :], jnp.zeros((n_mod - B - 1, D), F32)], axis=0)
    mods = _ada_all(cvec, ada_w, ada_b)

    cos_t, sin_t = _rope_tables(S)
    dft_x = dft_c = cs64 = scan_masks = head_blk = None
    if depth > 1:
        dft_x, dft_c, cs64 = _dft_tables(S), _dft_tables(C), _channel_dft()
        scan_masks, head_blk = _scan_masks()

    tm_x = 512 if S % 512 == 0 else S
    tm_o = 256 if S % 256 == 0 else S

    h, hc = x, ctx
    for i in range(depth):
        last = i == depth - 1
        j = i // 2
        m6 = mods[i].reshape(n_mod, 6, D)
        mx = [m6[:B, k][:, None, :] for k in range(6)]
        mc = [m6[B:B + 1, k][:, None, :] for k in range(6)]
        g_mix = norm_mix_g[i][None, :]
        g_ffn = norm_ffn_g[i][None, :]
        rw = router_w[i].T
        rb = router_b[i][:, None]

        if i % 2 == 0:
            assert not last, "an even final layer is not part of this block"
            w_in = ev_w_in[j].astype(BF16)
            w_out = ev_w_out[j].astype(BF16)
            pw_bd = _block_diag(pool_w[j]).astype(BF16)
            ps = pool_scale[j][None, :]
            sink = att_sink[j]
            pool_ux, qx, kvx = _even_in(h, mx[0], mx[1], g_mix, w_in, cos_t, sin_t, rope=True, tm=tm_x)
            pool_uc, qc, kvc = _even_in(hc, mc[0], mc[1], g_mix, w_in, cos_t[:C], sin_t[:C], rope=False, tm=C)
            h, fx_x, ti_x, tg_x = _even_mix(sink, qx, kvx, kvc, pool_ux, h, mx[2], mx[3], mx[4], g_ffn, w_out,
                                      pw_bd, ps, rw, rb, local=True)
            hc, fx_c, ti_c, tg_c = _even_mix(sink, qc, None, kvc, pool_uc, hc, mc[2], mc[3], mc[4], g_ffn, w_out,
                                       pw_bd, ps, rw, rb, local=False)
        else:
            w_in = od_w_in[j].astype(BF16)
            w_out = od_w_out[j].astype(BF16)
            od_args = (w_in, rw_mu[j], rw_w0[j], rw_w2[j].astype(BF16), rw_a0[j], rw_a2[j].astype(BF16),
                       rw_k_k[j][None, :], rw_k_a[j][None, :], rw_r_k[j].reshape(1, RWKV_WIDTH))
            rx, vx, kkx, bonus_x, lwx, kdx, bdx, gcx, ux = _odd_in(h, mx[0], mx[1], g_mix, *od_args, tm=tm_o)
            rc, vc, kkc, bonus_c, lwc, kdc, bdc, gcc, uc = _odd_in(hc, mc[0], mc[1], g_mix, *od_args, tm=C)
            s0 = jnp.zeros(_scan_state_shape(B), F32)
            yf_c, yb_c, s_c = _scan(rc, vc, kkc, lwc, kdc, bdc, s0, scan_masks, head_blk)
            yf_x, yb_x, _ = _scan(rx, vx, kkx, lwx, kdx, bdx, s_c, scan_masks, head_blk)
            ro_args = (g_ffn, rw_gn_g[j][None, :], rw_gn_b[j][None, :], rw_g2[j].astype(BF16), w_out, rw, rb)
            h, fx_x, ti_x, tg_x = _odd_out(yf_x, yb_x, bonus_x, gcx, ux, dft_x, cs64, h, mx[2], mx[3], mx[4],
                                           *ro_args, tm=tm_o)
            if not last:
                hc, fx_c, ti_c, tg_c = _odd_out(yf_c, yb_c, bonus_c, gcc, uc, dft_c, cs64, hc, mc[2], mc[3], mc[4],
                                                *ro_args, tm=C)

        n_x = B * S
        if last:
            fx_all = fx_x.reshape(n_x, D)
            ti_all = ti_x.transpose(0, 2, 1).reshape(n_x, TOP_K)
        else:
            fx_all = jnp.concatenate([fx_x.reshape(n_x, D), fx_c.reshape(B * C, D)], axis=0)
            ti_all = jnp.concatenate([ti_x.transpose(0, 2, 1).reshape(n_x, TOP_K),
                                      ti_c.transpose(0, 2, 1).reshape(B * C, TOP_K)], axis=0)
        dest, sorted_tok, vis_blk, vis_e, vis_lo, n_vis = _route(ti_all)
        xg = _take_rows(fx_all, sorted_tok)
        wg, wl = _split_gu(exp_w_gu, i)
        bg = exp_b_gu[i][:, None, 0::2]
        bl = exp_b_gu[i][:, None, 1::2]
        bd = exp_b_dn[i][:, None, :]
        y = _moe_experts(vis_blk, vis_e, vis_lo, n_vis, xg, wg, wl, bg, bl, exp_w_dn, i, bd)
        tm_cx = 256 if S % 256 == 0 else S
        yg_x = _take_rows(y, dest[:n_x].T.reshape(TOP_K, B, S))
        h = _combine(yg_x, tg_x.transpose(0, 2, 1), h, mx[5], final_g[None, :], final=last, tm=tm_cx)
        if not last:
            yg_c = _take_rows(y, dest[n_x:].T.reshape(TOP_K, B, C))
            hc = _combine(yg_c, tg_c.transpose(0, 2, 1), hc, mc[5], final_g[None, :], final=False, tm=C)
    return h
```

```python
import functools
import math

import jax
import jax.numpy as jnp
import numpy as np
from jax import lax
from jax.experimental import pallas as pl
from jax.experimental.pallas import tpu as pltpu

F32 = jnp.float32
BF16 = jnp.bfloat16

GRID_W = 64
HEAD_DIM = 64
ROT_FREQS = HEAD_DIM // 4
ROPE_THETA = 10000.0
NORM_EPS = 1e-5
POOL_WINDOWS = (2, 4, 8, 16)
POOL_GROUP = 64
POOL_WIDTH = POOL_GROUP * len(POOL_WINDOWS)
POOL_HALO = max(POOL_WINDOWS) // 2
ATT_HEADS = 12
ATT_KV_HEADS = 3
ATT_GROUP = ATT_HEADS // ATT_KV_HEADS
ATT_WINDOW = 128
ATT_BLOCK = 128
ATT_Q = ATT_HEADS * HEAD_DIM
ATT_KV = ATT_KV_HEADS * HEAD_DIM
EVEN_IN = POOL_WIDTH + ATT_Q + 2 * ATT_KV
RWKV_HEADS = 12
RWKV_WIDTH = RWKV_HEADS * HEAD_DIM
DECAY_LORA = 64
ICLR_LORA = 64
GATE_LORA = 160
GN_EPS = 64e-5
LORA_LO = 3 * RWKV_WIDTH
STATE_HI = LORA_LO + 2 * DECAY_LORA + 2 * ICLR_LORA
RWKV_IN = STATE_HI + GATE_LORA
FNET_GROUPS = 4
FNET_GROUP = 64
FNET_WIDTH = FNET_GROUPS * FNET_GROUP
ODD_IN = RWKV_IN + FNET_WIDTH
N_EXPERTS = 32
TOP_K = 4
SWIGLU_LIMIT = 7.0
SWIGLU_ALPHA = 1.702

LANES = 128
SUBLANES = 8
SCAN_CHUNK = 64
SCAN_GROUP = 4
SPLIT_ROWS = 512
MOE_ROWS = 512
MASK_NEG = -1e30


def _cparams(n_axes, vmem_mb):
    return pltpu.CompilerParams(
        dimension_semantics=("arbitrary",) * n_axes,
        vmem_limit_bytes=vmem_mb * 1024 * 1024,
    )


def _dot(a, b):
    return jnp.dot(a, b, preferred_element_type=F32)


def _dot_nt(a, b):
    return lax.dot_general(a, b, (((1,), (1,)), ((), ())), preferred_element_type=F32)


def _dot_tn(a, b):
    return lax.dot_general(a, b, (((0,), (0,)), ((), ())), preferred_element_type=F32)


def _split(x):
    hi = x.astype(BF16)
    lo = (x - hi.astype(F32)).astype(BF16)
    return hi, lo


def _dot3(a, b):
    ah, al = _split(a)
    bh, bl = _split(b)
    return _dot(ah, bh) + (_dot(al, bh) + _dot(ah, bl))


def _dot_exact_lhs(a_exact, b):
    a16 = a_exact.astype(BF16)
    b1 = b.astype(BF16)
    r1 = b - b1.astype(F32)
    b2 = r1.astype(BF16)
    b3 = (r1 - b2.astype(F32)).astype(BF16)
    return _dot(a16, b1) + (_dot(a16, b2) + _dot(a16, b3))


def _modnorm(x, g, sh, sc):
    ms = jnp.mean(x * x, axis=-1, keepdims=True)
    xn = x * lax.rsqrt(ms + NORM_EPS) * g
    return xn * (1.0 + sc) + sh


def _sigmoid(x):
    return 1.0 / (1.0 + jnp.exp(-x))


def _head_sum(x):
    n = x.shape[1] // LANES
    lane = lax.broadcasted_iota(jnp.int32, (x.shape[0], LANES), 1)
    lo_mask = lane < HEAD_DIM
    parts = []
    for c in range(n):
        xc = x[:, c * LANES:(c + 1) * LANES]
        s_lo = jnp.sum(jnp.where(lo_mask, xc, 0.0), axis=-1, keepdims=True)
        s_hi = jnp.sum(jnp.where(lo_mask, 0.0, xc), axis=-1, keepdims=True)
        parts.append(jnp.where(lo_mask, s_lo, s_hi))
    return jnp.concatenate(parts, axis=1)


def _ada_kernel(c_ref, w_ref, b_ref, o_ref):
    x = c_ref[...]
    x = x * _sigmoid(x)
    o_ref[...] = _dot3(x, w_ref[...]) + b_ref[...]


def _ada_all(cvec, ada_w, ada_b):
    depth, d, n6 = ada_w.shape
    r = cvec.shape[0]
    tn = 1536 if n6 % 1536 == 0 else n6
    return pl.pallas_call(
        _ada_kernel,
        out_shape=jax.ShapeDtypeStruct((depth, r, n6), F32),
        grid=(depth, n6 // tn),
        in_specs=[
            pl.BlockSpec((r, d), lambda i, j: (0, 0)),
            pl.BlockSpec((None, d, tn), lambda i, j: (i, 0, j)),
            pl.BlockSpec((None, 1, tn), lambda i, j: (i, 0, j)),
        ],
        out_specs=pl.BlockSpec((None, r, tn), lambda i, j: (i, 0, j)),
        compiler_params=_cparams(2, 48),
        name="ada_mod",
    )(cvec, ada_w, ada_b.reshape(depth, 1, n6))


def _bmap(arr):
    if arr.shape[0] == 1:
        return lambda b: 0
    return lambda b: b


def _even_in_kernel(h_ref, sh_ref, sc_ref, g_ref, w_ref, cos_ref, sin_ref,
                    pool_ref, q_ref, kv_ref, *, rope):
    a = _modnorm(h_ref[...], g_ref[...], sh_ref[...], sc_ref[...]).astype(BF16)
    px = _dot(a, w_ref[...])
    pool_ref[...] = px[:, :POOL_WIDTH]
    n_chunks = (ATT_Q + 2 * ATT_KV) // LANES
    n_full = (ATT_Q + ATT_KV) // LANES
    outs = []
    if rope:
        lane = lax.broadcasted_iota(jnp.int32, (px.shape[0], LANES), 1)
        first = (lane % (2 * ROT_FREQS)) < ROT_FREQS
    for c in range(n_chunks):
        x = px[:, POOL_WIDTH + c * LANES:POOL_WIDTH + (c + 1) * LANES]
        if rope and c <= n_full:
            t0 = 0 if c < n_full else LANES
            cs = cos_ref[:, t0:t0 + LANES]
            sn = sin_ref[:, t0:t0 + LANES]
            rot = jnp.where(first, pltpu.roll(x, LANES - ROT_FREQS, 1), pltpu.roll(x, ROT_FREQS, 1))
            x = x * cs + rot * sn
        if c < ATT_Q // LANES:
            x = x * (HEAD_DIM ** -0.5)
        outs.append(x.astype(BF16))
    nq = ATT_Q // LANES
    q_ref[...] = jnp.concatenate(outs[:nq], axis=1)
    kv_ref[...] = jnp.concatenate(outs[nq:], axis=1)


def _even_in(h, sh, sc, g, w_bf, cos_t, sin_t, *, rope, tm):
    B, T, D = h.shape
    nt = T // tm
    bm = _bmap(sh)
    kern = functools.partial(_even_in_kernel, rope=rope)
    return pl.pallas_call(
        kern,
        out_shape=(
            jax.ShapeDtypeStruct((B, T, POOL_WIDTH), F32),
            jax.ShapeDtypeStruct((B, T, ATT_Q), BF16),
            jax.ShapeDtypeStruct((B, T, 2 * ATT_KV), BF16),
        ),
        grid=(nt, B),
        in_specs=[
            pl.BlockSpec((None, tm, D), lambda s, b: (b, s, 0)),
            pl.BlockSpec((None, 1, D), lambda s, b: (bm(b), 0, 0)),
            pl.BlockSpec((None, 1, D), lambda s, b: (bm(b), 0, 0)),
            pl.BlockSpec((1, D), lambda s, b: (0, 0)),
            pl.BlockSpec((D, EVEN_IN), lambda s, b: (0, 0)),
            pl.BlockSpec((tm, 2 * LANES), lambda s, b: (s, 0)),
            pl.BlockSpec((tm, 2 * LANES), lambda s, b: (s, 0)),
        ],
        out_specs=(
            pl.BlockSpec((None, tm, POOL_WIDTH), lambda s, b: (b, s, 0)),
            pl.BlockSpec((None, tm, ATT_Q), lambda s, b: (b, s, 0)),
            pl.BlockSpec((None, tm, 2 * ATT_KV), lambda s, b: (b, s, 0)),
        ),
        compiler_params=_cparams(2, 48),
        name="even_in",
    )(h, sh, sc, g, w_bf, cos_t, sin_t)


def _rope_tables(T):
    t = jnp.arange(T, dtype=jnp.int32)
    row = (t // GRID_W).astype(F32)
    col = (t % GRID_W).astype(F32)
    inv_freq = ROPE_THETA ** (-jnp.arange(ROT_FREQS, dtype=F32) / ROT_FREQS)
    ang_r = row[:, None] * inv_freq
    ang_c = col[:, None] * inv_freq
    cos_h = jnp.concatenate([jnp.cos(ang_r)] * 2 + [jnp.cos(ang_c)] * 2, axis=1)
    sin_h = jnp.concatenate([-jnp.sin(ang_r), jnp.sin(ang_r), -jnp.sin(ang_c), jnp.sin(ang_c)], axis=1)
    one = jnp.ones_like(cos_h)
    zero = jnp.zeros_like(sin_h)
    cos_t = jnp.concatenate([cos_h, cos_h, cos_h, one], axis=1)
    sin_t = jnp.concatenate([sin_h, sin_h, sin_h, zero], axis=1)
    return cos_t, sin_t


def _strict_upper(n):
    i = np.arange(n)
    return jnp.asarray(i[:, None] < i[None, :], BF16)


def _residual_and_router(h, y, gm, gf_norm, shf, scf, rw, rb, tri, first_step, cnt_in_ref,
                         hn_ref, fx_ref, ti_ref, tg_ref, rk_ref, cnt_ref, cnt_scr):
    hn = h + gm * y
    hn_ref[...] = hn
    fx = _modnorm(hn, gf_norm, shf, scf)
    fx_ref[...] = fx.astype(BF16)
    fh, fl = _split(fx)
    wh, wl = _split(rw)
    logits = _dot_nt(wh, fh) + (_dot_nt(wl, fh) + _dot_nt(wh, fl)) + rb
    rows = logits.shape[1]
    eidx = lax.broadcasted_iota(jnp.int32, (N_EXPERTS, rows), 0).astype(F32)
    vals, idxs = [], []
    for _ in range(TOP_K):
        mx = jnp.max(logits, axis=0, keepdims=True)
        ix = jnp.min(jnp.where(logits == mx, eidx, float(N_EXPERTS)), axis=0, keepdims=True)
        vals.append(mx)
        idxs.append(ix)
        logits = jnp.where(eidx == ix, MASK_NEG, logits)
    es = [jnp.exp(v - vals[0]) for v in vals]
    den = es[0]
    for e in es[1:]:
        den = den + e
    tg_ref[...] = jnp.concatenate(es, axis=0) / den
    ti_ref[...] = jnp.concatenate(idxs, axis=0).astype(jnp.int32)

    @pl.when(first_step)
    def _():
        cnt_scr[...] = cnt_in_ref[...]

    hits = [jnp.where(eidx == ix, 1.0, 0.0) for ix in idxs]
    onehot = hits[0]
    for hit in hits[1:]:
        onehot = onehot + hit
    before = cnt_scr[:, 0:1] + _dot(onehot.astype(BF16), tri)
    ranks = [jnp.sum(hit * before, axis=0, keepdims=True) for hit in hits]
    rk_ref[...] = jnp.concatenate(ranks, axis=0).astype(jnp.int32)
    total = cnt_scr[...] + jnp.sum(onehot, axis=1, keepdims=True)
    cnt_scr[...] = total
    cnt_ref[...] = total


def _even_mix_kernel(*refs, local, tq, T):
    it = iter(refs)
    sink_ref = next(it)
    q_ref = next(it)
    if local:
        kvp_ref, kvc_ref, kvn_ref, bias_ref = next(it), next(it), next(it), next(it)
    ckv_ref = next(it)
    up_ref, uc_ref, un_ref = next(it), next(it), next(it)
    h_ref, gm_ref, shf_ref, scf_ref, gfn_ref = next(it), next(it), next(it), next(it), next(it)
    wout_ref, pw_ref, ps_ref, rw_ref, rb_ref = next(it), next(it), next(it), next(it), next(it)
    tri_ref, cnt_in_ref = next(it), next(it)
    hn_ref, fx_ref, ti_ref, tg_ref, rk_ref, cnt_ref = next(it), next(it), next(it), next(it), next(it), next(it)
    cnt_scr = next(it)

    j = pl.program_id(1)
    nb = pl.num_programs(1)
    first_step = (pl.program_id(0) == 0) & (j == 0)

    cols = ATT_GROUP * tq
    col_head = lax.broadcasted_iota(jnp.int32, (1, cols), 1) // tq
    q = q_ref[...]
    ckv = ckv_ref[...]
    if local:
        kvl = jnp.concatenate([kvp_ref[...], kvc_ref[...], kvn_ref[...]], axis=0)
        bias = bias_ref[...]
    att_t = []
    for g in range(ATT_KV_HEADS):
        qs = jnp.concatenate(
            [q[:, (g * ATT_GROUP + i) * HEAD_DIM:(g * ATT_GROUP + i + 1) * HEAD_DIM] for i in range(ATT_GROUP)],
            axis=0)
        kc = ckv[:, g * HEAD_DIM:(g + 1) * HEAD_DIM]
        vc = ckv[:, ATT_KV + g * HEAD_DIM:ATT_KV + (g + 1) * HEAD_DIM]
        s_ctx = _dot_nt(kc, qs)
        sink = jnp.zeros((1, cols), F32)
        for i in range(ATT_GROUP):
            sink = jnp.where(col_head == i, sink_ref[g * ATT_GROUP + i], sink)
        if local:
            kl = kvl[:, g * HEAD_DIM:(g + 1) * HEAD_DIM]
            vl = kvl[:, ATT_KV + g * HEAD_DIM:ATT_KV + (g + 1) * HEAD_DIM]
            s_loc = _dot_nt(kl, qs) + bias
            m = jnp.maximum(jnp.maximum(jnp.max(s_loc, axis=0, keepdims=True),
                                        jnp.max(s_ctx, axis=0, keepdims=True)), sink)
            p_loc = jnp.exp(s_loc - m)
            p_ctx = jnp.exp(s_ctx - m)
            den = (jnp.sum(p_loc, axis=0, keepdims=True) + jnp.sum(p_ctx, axis=0, keepdims=True)
                   + jnp.exp(sink - m))
            o_t = _dot_tn(vl, p_loc.astype(BF16)) + _dot_tn(vc, p_ctx.astype(BF16))
        else:
            m = jnp.maximum(jnp.max(s_ctx, axis=0, keepdims=True), sink)
            p_ctx = jnp.exp(s_ctx - m)
            den = jnp.sum(p_ctx, axis=0, keepdims=True) + jnp.exp(sink - m)
            o_t = _dot_tn(vc, p_ctx.astype(BF16))
        o_t = o_t / den
        for i in range(ATT_GROUP):
            att_t.append(o_t[:, i * tq:(i + 1) * tq])
    att_xt = jnp.concatenate(att_t, axis=0).astype(BF16)

    u = uc_ref[...]
    up = jnp.where(j > 0, up_ref[...], 0.0)
    un = jnp.where(j < nb - 1, un_ref[...], 0.0)
    e = jnp.concatenate([up, u, un], axis=0)
    n_e = tq + 2 * POOL_HALO

    def shifted(x, k):
        return pltpu.roll(x, k % n_e, 0)

    a1 = e + shifted(e, 1)
    a2 = shifted(a1, 1) + shifted(a1, -1)
    a3 = shifted(a2, 2) + shifted(a2, -2)
    a4 = shifted(a3, 4) + shifted(a3, -4)
    lane = lax.broadcasted_iota(jnp.int32, (n_e, POOL_WIDTH), 1)
    grp = lane // POOL_GROUP
    win = jnp.where(grp == 0, a1, jnp.where(grp == 1, a2, jnp.where(grp == 2, a3, a4)))
    win = win[POOL_HALO:POOL_HALO + tq]
    t_i = j * tq + lax.broadcasted_iota(jnp.int32, (tq, POOL_WIDTH), 0)
    half = jnp.left_shift(1, lax.broadcasted_iota(jnp.int32, (tq, POOL_WIDTH), 1) // POOL_GROUP)
    cnt = (jnp.minimum(t_i + half, T) - jnp.maximum(t_i - half, 0)).astype(F32)
    pooled = win / cnt - u
    pool_x = _dot(pooled.astype(BF16), pw_ref[...]) * ps_ref[...]

    y = _dot(pool_x.astype(BF16), wout_ref[:POOL_WIDTH, :]) + _dot_tn(att_xt, wout_ref[POOL_WIDTH:, :])
    _residual_and_router(h_ref[...], y, gm_ref[...], gfn_ref[...], shf_ref[...], scf_ref[...],
                         rw_ref[...], rb_ref[...], tri_ref[...], first_step, cnt_in_ref,
                         hn_ref, fx_ref, ti_ref, tg_ref, rk_ref, cnt_ref, cnt_scr)


def _router_out_shapes(B, T, D):
    return (
        jax.ShapeDtypeStruct((B, T, D), F32),
        jax.ShapeDtypeStruct((B, T, D), BF16),
        jax.ShapeDtypeStruct((B, TOP_K, T), jnp.int32),
        jax.ShapeDtypeStruct((B, TOP_K, T), F32),
        jax.ShapeDtypeStruct((B, TOP_K, T), jnp.int32),
        jax.ShapeDtypeStruct((N_EXPERTS, LANES), F32),
    )


def _router_out_specs(tm, D):
    return (
        pl.BlockSpec((None, tm, D), lambda b, s: (b, s, 0)),
        pl.BlockSpec((None, tm, D), lambda b, s: (b, s, 0)),
        pl.BlockSpec((None, TOP_K, tm), lambda b, s: (b, 0, s)),
        pl.BlockSpec((None, TOP_K, tm), lambda b, s: (b, 0, s)),
        pl.BlockSpec((None, TOP_K, tm), lambda b, s: (b, 0, s)),
        pl.BlockSpec((N_EXPERTS, LANES), lambda b, s: (0, 0)),
    )


def _band_bias(tq, nb):
    c = np.arange(3 * ATT_BLOCK)[:, None]
    qq = (np.arange(ATT_GROUP * tq) % tq)[None, :]
    k_rel = c - ATT_BLOCK
    out = []
    for first, last in ((True, nb == 1), (False, False), (nb == 1, True)):
        ok = np.abs(qq - k_rel) <= ATT_WINDOW
        if first:
            ok = ok & (k_rel >= 0)
        if last:
            ok = ok & (k_rel < tq)
        out.append(np.where(ok, 0.0, MASK_NEG))
    return jnp.asarray(np.stack(out), F32)


def _even_mix(sink, q, kv, ckv, pool_u, h, gm, shf, scf, gfn, wout, pw_bd, pscale, rw, rb, cnt_in, *, local):
    B, T, D = h.shape
    n_ctx = ckv.shape[1]
    tq = ATT_BLOCK if local else T
    nb = T // tq
    r8 = tq // SUBLANES
    n8 = T // SUBLANES
    bm = _bmap(gm)
    kern = functools.partial(_even_mix_kernel, local=local, tq=tq, T=T)
    vec = lambda: pl.BlockSpec((None, 1, D), lambda b, j: (bm(b), 0, 0))
    full = lambda a: pl.BlockSpec(a.shape, lambda b, j: (0,) * a.ndim)
    in_specs = [pl.BlockSpec(memory_space=pltpu.SMEM),
                pl.BlockSpec((None, tq, ATT_Q), lambda b, j: (b, j, 0))]
    args = [sink, q]
    if local:
        in_specs += [
            pl.BlockSpec((None, ATT_BLOCK, 2 * ATT_KV), lambda b, j: (b, jnp.maximum(j - 1, 0), 0)),
            pl.BlockSpec((None, ATT_BLOCK, 2 * ATT_KV), lambda b, j: (b, j, 0)),
            pl.BlockSpec((None, ATT_BLOCK, 2 * ATT_KV), lambda b, j: (b, jnp.minimum(j + 1, nb - 1), 0)),
        ]
        in_specs.append(pl.BlockSpec((None, 3 * ATT_BLOCK, ATT_GROUP * tq),
                                     lambda b, j: (jnp.where(j == 0, 0, jnp.where(j == nb - 1, 2, 1)), 0, 0)))
        args += [kv, kv, kv, _band_bias(tq, nb)]
    in_specs += [
        pl.BlockSpec((None, n_ctx, 2 * ATT_KV), lambda b, j: (b, 0, 0)),
        pl.BlockSpec((None, SUBLANES, POOL_WIDTH), lambda b, j: (b, jnp.maximum(j * r8 - 1, 0), 0)),
        pl.BlockSpec((None, tq, POOL_WIDTH), lambda b, j: (b, j, 0)),
        pl.BlockSpec((None, SUBLANES, POOL_WIDTH), lambda b, j: (b, jnp.minimum((j + 1) * r8, n8 - 1), 0)),
        pl.BlockSpec((None, tq, D), lambda b, j: (b, j, 0)),
        vec(), vec(), vec(), full(gfn), full(wout), full(pw_bd), full(pscale), full(rw), full(rb),
    ]
    tri = _strict_upper(tq)
    in_specs += [full(tri), full(cnt_in)]
    args += [ckv, pool_u, pool_u, pool_u, h, gm, shf, scf, gfn, wout, pw_bd, pscale, rw, rb, tri, cnt_in]
    return pl.pallas_call(
        kern,
        out_shape=_router_out_shapes(B, T, D),
        grid=(B, nb),
        in_specs=in_specs,
        out_specs=_router_out_specs(tq, D),
        scratch_shapes=[pltpu.VMEM((N_EXPERTS, LANES), F32)],
        compiler_params=_cparams(2, 48),
        name="even_mix_local" if local else "even_mix_ctx",
    )(*args)


def _odd_in_kernel(h_ref, hp_ref, hn_ref, sh_ref, sc_ref, g_ref, w_ref, mu_ref, w0_ref, w2_ref,
                   a0_ref, a2_ref, kk_ref, ka_ref, rk_ref,
                   r_out, v_out, kkn_out, bonus_out, lw_out, kd_out, bd_out, gc_out, fn_out, *, tm):
    s = pl.program_id(1)
    ns = pl.num_programs(1)
    g, sh, sc = g_ref[...], sh_ref[...], sc_ref[...]
    w = w_ref[...]
    a = _modnorm(h_ref[...], g, sh, sc).astype(BF16)
    px = _dot(a, w)
    fn_out[...] = px[:, RWKV_IN:]
    main = px[:, :RWKV_IN]
    ap = _modnorm(hp_ref[...], g, sh, sc).astype(BF16)
    an = _modnorm(hn_ref[...], g, sh, sc).astype(BF16)
    halo = _dot(jnp.concatenate([ap, an], axis=0), w[:, :RWKV_IN])
    prev_row = jnp.where(s > 0, halo[SUBLANES - 1:SUBLANES], 0.0)
    next_row = jnp.where(s < ns - 1, halo[SUBLANES:SUBLANES + 1], 0.0)
    row = lax.broadcasted_iota(jnp.int32, (tm, 1), 0)
    prev = jnp.where(row == 0, prev_row, pltpu.roll(main, 1, 0))
    nxt = jnp.where(row == tm - 1, next_row, pltpu.roll(main, tm - 1, 0))
    mu = mu_ref[...]
    fs = main + mu[0:1] * (prev - main) + mu[1:2] * (nxt - main)

    W = RWKV_WIDTH
    r = fs[:, :W]
    k = fs[:, W:2 * W]
    v = fs[:, 2 * W:3 * W]
    lora = fs[:, LORA_LO:STATE_HI]
    gc_out[...] = fs[:, STATE_HI:RWKV_IN]
    r_out[...] = r
    v_out[...] = v

    kx = k * kk_ref[...]
    nrm = jnp.sqrt(_head_sum(kx * kx))
    kkn = kx / jnp.maximum(nrm, 1e-12)
    kkn_out[...] = kkn
    ka = ka_ref[...]
    ksum = None
    for d in range(2):
        wd = lora[:, d * DECAY_LORA:(d + 1) * DECAY_LORA]
        o_a = 2 * DECAY_LORA
        ad = lora[:, o_a + d * ICLR_LORA:o_a + (d + 1) * ICLR_LORA]
        xw = _dot(jnp.tanh(wd).astype(BF16), w2_ref[d]) + w0_ref[d:d + 1]
        z = -xw
        softplus = jnp.maximum(z, 0.0) + jnp.log(1.0 + jnp.exp(-jnp.abs(z)))
        w_log = -softplus - 0.5
        lw_out[d] = -jnp.exp(w_log)
        xa = _dot(ad.astype(BF16), a2_ref[d]) + a0_ref[d:d + 1]
        a_d = _sigmoid(xa)
        k_d = k * (1.0 + (a_d - 1.0) * ka)
        kd_out[d] = k_d
        bd_out[d] = kkn * a_d
        ksum = k_d if ksum is None else ksum + k_d
    coef = _head_sum(r * ksum * rk_ref[...])
    bonus_out[...] = coef * v


def _odd_in(h, sh, sc, g, w_bf, mu, w0, w2_bf, a0, a2_bf, k_k, k_a, r_k, *, tm):
    B, T, D = h.shape
    ns = T // tm
    r8 = tm // SUBLANES
    n8 = T // SUBLANES
    bm = _bmap(sh)
    W = RWKV_WIDTH
    full = lambda a: pl.BlockSpec(a.shape, lambda b, s: (0,) * a.ndim)
    tok = lambda n: pl.BlockSpec((None, tm, n), lambda b, s: (b, s, 0))
    tok2 = lambda n: pl.BlockSpec((2, None, tm, n), lambda b, s: (0, b, s, 0))
    kern = functools.partial(_odd_in_kernel, tm=tm)
    return pl.pallas_call(
        kern,
        out_shape=(
            jax.ShapeDtypeStruct((B, T, W), F32),
            jax.ShapeDtypeStruct((B, T, W), F32),
            jax.ShapeDtypeStruct((B, T, W), F32),
            jax.ShapeDtypeStruct((B, T, W), F32),
            jax.ShapeDtypeStruct((2, B, T, W), F32),
            jax.ShapeDtypeStruct((2, B, T, W), F32),
            jax.ShapeDtypeStruct((2, B, T, W), F32),
            jax.ShapeDtypeStruct((B, T, GATE_LORA), F32),
            jax.ShapeDtypeStruct((B, T, FNET_WIDTH), F32),
        ),
        grid=(B, ns),
        in_specs=[
            pl.BlockSpec((None, tm, D), lambda b, s: (b, s, 0)),
            pl.BlockSpec((None, SUBLANES, D), lambda b, s: (b, jnp.maximum(s * r8 - 1, 0), 0)),
            pl.BlockSpec((None, SUBLANES, D), lambda b, s: (b, jnp.minimum((s + 1) * r8, n8 - 1), 0)),
            pl.BlockSpec((None, 1, D), lambda b, s: (bm(b), 0, 0)),
            pl.BlockSpec((None, 1, D), lambda b, s: (bm(b), 0, 0)),
            full(g), full(w_bf), full(mu), full(w0), full(w2_bf), full(a0), full(a2_bf),
            full(k_k), full(k_a), full(r_k),
        ],
        out_specs=(tok(W), tok(W), tok(W), tok(W), tok2(W), tok2(W), tok2(W), tok(GATE_LORA), tok(FNET_WIDTH)),
        compiler_params=_cparams(2, 60),
        name="odd_in",
    )(h, h, h, sh, sc, g, w_bf, mu, w0, w2_bf, a0, a2_bf, k_k, k_a, r_k)


SCAN_LEVELS = tuple(2 ** k for k in range(int(math.log2(SCAN_CHUNK))))
MASK_BEFORE, MASK_UPTO, MASK_LEVEL0 = 0, 1, 2


def _scan_masks():
    R = SCAN_GROUP * SCAN_CHUNK
    idx = np.arange(R)
    same = (idx[:, None] // SCAN_CHUNK) == (idx[None, :] // SCAN_CHUNK)
    t = (idx % SCAN_CHUNK)[:, None]
    s = (idx % SCAN_CHUNK)[None, :]
    out = []
    for sign in (1, -1):
        order = (t - s) * sign
        ms = [same & (order > 0), same & (order >= 0)]
        for m in SCAN_LEVELS:
            ms.append(same & (t // (2 * m) == s // (2 * m)) & ((((t // m) % 2) - ((s // m) % 2)) * sign == 1))
        out.append(np.stack(ms))
    return jnp.asarray(np.stack(out), F32), jnp.asarray(same, F32)


def _scan_kernel(rf_ref, vf_ref, kkf_ref, rb_ref, vb_ref, kkb_ref, lwf_ref, kdf_ref, bdf_ref,
                 lwb_ref, kdb_ref, bdb_ref, mask_ref, hb_ref, s0_ref, yf_ref, yb_ref, sT_ref, s_scr, *, L):
    c = pl.program_id(1)
    nc = pl.num_programs(1)
    R = SCAN_GROUP * L
    GW = SCAN_GROUP * HEAD_DIM

    @pl.when(c == 0)
    def _():
        s_scr[...] = s0_ref[...]

    head_blk = hb_ref[...]

    def wide(x):
        return (jnp.concatenate([x] * SCAN_GROUP, axis=0) * head_blk).astype(BF16)

    NG = RWKV_HEADS // SCAN_GROUP
    dirs = ((0, (rf_ref, vf_ref, kkf_ref, lwf_ref, kdf_ref, bdf_ref)),
            (1, (rb_ref, vb_ref, kkb_ref, lwb_ref, kdb_ref, bdb_ref)))
    wides = {k: [] for k in ("kt", "rt", "kh", "bh", "v", "ke", "be")}
    g_tots = []
    for d, (r_ref, v_ref, kk_ref, lw_ref, kd_ref, bd_ref) in dirs:
        lw = lw_ref[...]
        c_in = _dot_exact_lhs(mask_ref[d, MASK_UPTO][:L, :L], lw)
        c_ex = c_in - lw
        tot = jnp.sum(lw, axis=0, keepdims=True)
        kd = kd_ref[...]
        bd = bd_ref[...]
        inv = jnp.exp(-c_in)
        end = jnp.exp(tot - c_in)
        g_tot = jnp.exp(tot)
        cols = dict(
            kt=kk_ref[...] * jnp.exp(c_ex), rt=r_ref[...] * jnp.exp(c_in), kh=kd * inv, bh=bd * inv,
            v=v_ref[...], ke=kd * end, be=bd * end)
        for g in range(NG):
            sl = slice(g * GW, (g + 1) * GW)
            for k, a in cols.items():
                wides[k].append(wide(a[:, sl]))
            g_tots.append(jnp.broadcast_to(g_tot[:, sl], (GW, GW)))
    w = {k: jnp.stack(a) for k, a in wides.items()}
    g_tot_all = jnp.stack(g_tots)

    def masked(x, idx):
        return jnp.concatenate([x[:NG] * mask_ref[0, idx][None], x[NG:] * mask_ref[1, idx][None]], axis=0)

    bmm = lambda a, b: jnp.einsum("bij,bjk->bik", a, b, preferred_element_type=F32)
    bmm_nt = lambda a, b: jnp.einsum("bik,bjk->bij", a, b, preferred_element_type=F32)
    bmm_tn = lambda a, b: jnp.einsum("bki,bkj->bij", a, b, preferred_element_type=F32)

    lhs = jnp.concatenate([w["kt"], w["rt"]], axis=1)
    s4 = bmm_nt(lhs, jnp.concatenate([w["kh"], w["bh"]], axis=1))
    a_kk = masked(s4[:, :R, :R], MASK_BEFORE).astype(BF16)
    m = masked(s4[:, :R, R:], MASK_BEFORE)
    a_rk = masked(s4[:, R:, :R], MASK_UPTO).astype(BF16)
    a_rb = masked(s4[:, R:, R:], MASK_UPTO).astype(BF16)
    eye = (mask_ref[0, MASK_UPTO] - mask_ref[0, MASK_BEFORE])[None]
    dinv = eye - masked(m, MASK_LEVEL0)
    for li in range(1, len(SCAN_LEVELS)):
        d16 = dinv.astype(BF16)
        e16 = masked(m, MASK_LEVEL0 + li).astype(BF16)
        dinv = dinv - bmm(bmm(d16, e16).astype(BF16), d16)
    s_all = s_scr[...].reshape(2 * NG, GW, GW)
    ks = bmm_nt(lhs, s_all.astype(BF16))
    u16 = bmm(dinv.astype(BF16), (ks[:, :R] + bmm(a_kk, w["v"])).astype(BF16)).astype(BF16)
    yw = ks[:, R:] + bmm(a_rk, w["v"]) - bmm(a_rb, u16)
    y_all = yw[:, 0:L]
    for hh in range(1, SCAN_GROUP):
        y_all = y_all + yw[:, hh * L:(hh + 1) * L]
    yf_ref[...] = jnp.concatenate([y_all[g] for g in range(NG)], axis=1)
    yb_ref[...] = jnp.concatenate([y_all[NG + g] for g in range(NG)], axis=1)
    vu = jnp.concatenate([w["v"], -u16], axis=1)
    ke = jnp.concatenate([w["ke"], w["be"]], axis=1)
    s_scr[...] = (s_all * g_tot_all + bmm_tn(vu, ke)).reshape(2, NG, GW, GW)

    @pl.when(c == nc - 1)
    def _():
        sT_ref[...] = s_scr[...]


def _scan_state_shape(B):
    gw = SCAN_GROUP * HEAD_DIM
    return (2, B, RWKV_HEADS // SCAN_GROUP, gw, gw)


def _scan(r, v, kk, lw, kd, bd, s0, masks, head_blk):
    B, T, W = r.shape
    L = SCAN_CHUNK
    assert L == HEAD_DIM and T % L == 0
    nc = T // L
    st_shape = _scan_state_shape(B)
    fwd = pl.BlockSpec((None, L, W), lambda b, c: (b, c, 0))
    bwd = pl.BlockSpec((None, L, W), lambda b, c: (b, nc - 1 - c, 0))
    fwd_d = pl.BlockSpec((None, None, L, W), lambda b, c: (0, b, c, 0))
    bwd_d = pl.BlockSpec((None, None, L, W), lambda b, c: (1, b, nc - 1 - c, 0))
    state = pl.BlockSpec((2, None) + st_shape[2:], lambda b, c: (0, b, 0, 0, 0))
    full = lambda a: pl.BlockSpec(a.shape, lambda b, c: (0,) * a.ndim)
    kern = functools.partial(_scan_kernel, L=L)
    return pl.pallas_call(
        kern,
        out_shape=(
            jax.ShapeDtypeStruct((B, T, W), F32),
            jax.ShapeDtypeStruct((B, T, W), F32),
            jax.ShapeDtypeStruct(st_shape, F32),
        ),
        grid=(B, nc),
        in_specs=[fwd, fwd, fwd, bwd, bwd, bwd, fwd_d, fwd_d, fwd_d, bwd_d, bwd_d, bwd_d,
                  full(masks), full(head_blk), state],
        out_specs=(fwd, bwd, state),
        scratch_shapes=[pltpu.VMEM((2,) + st_shape[2:], F32)],
        compiler_params=_cparams(2, 56),
        name="rwkv_scan",
    )(r, v, kk, r, v, kk, lw, kd, bd, lw, kd, bd, masks, head_blk, s0)


def _odd_out_kernel(yf_ref, yb_ref, bonus_ref, gc_ref, u_ref, dft_ref, cs_ref, h_ref, gm_ref, shf_ref, scf_ref,
                    gfn_ref, gng_ref, gnb_ref, g2_ref, wout_ref, rw_ref, rb_ref, tri_ref, cnt_in_ref,
                    hn_ref, fx_ref, ti_ref, tg_ref, rk_ref, cnt_ref, ucs_scr, cnt_scr, *, T):
    s = pl.program_id(1)
    first_step = (pl.program_id(0) == 0) & (s == 0)

    @pl.when(s == 0)
    def _():
        t = _dot(u_ref[...].astype(BF16), cs_ref[...])
        ucs_scr[0:T, :] = t[:, :FNET_WIDTH].astype(BF16)
        ucs_scr[T:2 * T, :] = t[:, FNET_WIDTH:].astype(BF16)

    f = _dot(dft_ref[...], ucs_scr[...])
    y = yf_ref[...] + yb_ref[...]
    mean = _head_sum(y) * (1.0 / HEAD_DIM)
    yc = y - mean
    var = _head_sum(yc * yc) * (1.0 / HEAD_DIM)
    yn = yc * lax.rsqrt(var + GN_EPS) * gng_ref[...] + gnb_ref[...]
    out = yn + bonus_ref[...]
    gate = _dot(_sigmoid(gc_ref[...]).astype(BF16), g2_ref[...])
    o = out * gate
    mix = jnp.concatenate([o, f], axis=1).astype(BF16)
    ymix = _dot(mix, wout_ref[...])
    _residual_and_router(h_ref[...], ymix, gm_ref[...], gfn_ref[...], shf_ref[...], scf_ref[...],
                         rw_ref[...], rb_ref[...], tri_ref[...], first_step, cnt_in_ref,
                         hn_ref, fx_ref, ti_ref, tg_ref, rk_ref, cnt_ref, cnt_scr)


def _odd_out(yf, yb, bonus, gc, u, dft, cs64, h, gm, shf, scf, gfn, gn_g, gn_b, g2_bf, wout, rw, rb, cnt_in, *, tm):
    B, T, D = h.shape
    ns = T // tm
    W = RWKV_WIDTH
    bm = _bmap(gm)
    vec = lambda: pl.BlockSpec((None, 1, D), lambda b, s: (bm(b), 0, 0))
    full = lambda a: pl.BlockSpec(a.shape, lambda b, s: (0,) * a.ndim)
    kern = functools.partial(_odd_out_kernel, T=T)
    tri = _strict_upper(tm)
    return pl.pallas_call(
        kern,
        out_shape=_router_out_shapes(B, T, D),
        grid=(B, ns),
        in_specs=[
            pl.BlockSpec((None, tm, W), lambda b, s: (b, s, 0)),
            pl.BlockSpec((None, tm, W), lambda b, s: (b, s, 0)),
            pl.BlockSpec((None, tm, W), lambda b, s: (b, s, 0)),
            pl.BlockSpec((None, tm, GATE_LORA), lambda b, s: (b, s, 0)),
            pl.BlockSpec((None, T, FNET_WIDTH), lambda b, s: (b, 0, 0)),
            pl.BlockSpec((tm, 2 * T), lambda b, s: (s, 0)),
            full(cs64),
            pl.BlockSpec((None, tm, D), lambda b, s: (b, s, 0)),
            vec(), vec(), vec(), full(gfn), full(gn_g), full(gn_b), full(g2_bf), full(wout), full(rw), full(rb),
            full(tri), full(cnt_in),
        ],
        out_specs=_router_out_specs(tm, D),
        scratch_shapes=[pltpu.VMEM((2 * T, FNET_WIDTH), BF16), pltpu.VMEM((N_EXPERTS, LANES), F32)],
        compiler_params=_cparams(2, 48),
        name="odd_out",
    )(yf, yb, bonus, gc, u, dft, cs64, h, gm, shf, scf, gfn, gn_g, gn_b, g2_bf, wout, rw, rb, tri, cnt_in)


def _dft_tables(T):
    t = jnp.arange(T, dtype=jnp.int32)
    ang = ((t[:, None] * t[None, :]) % T).astype(F32) * (2.0 * math.pi / T)
    scale = 1.0 / math.sqrt(T * FNET_GROUP)
    return (jnp.concatenate([jnp.cos(ang), -jnp.sin(ang)], axis=1) * scale).astype(BF16)


def _channel_dft():
    c = jnp.arange(FNET_GROUP, dtype=jnp.int32)
    ang = ((c[:, None] * c[None, :]) % FNET_GROUP).astype(F32) * (2.0 * math.pi / FNET_GROUP)
    eye = jnp.eye(FNET_GROUPS, dtype=F32)
    return jnp.concatenate([jnp.kron(eye, jnp.cos(ang)), jnp.kron(eye, jnp.sin(ang))], axis=1).astype(BF16)


def _split_gu_kernel(w_ref, wg_ref, wl_ref, t_scr):
    n_slab = w_ref.shape[0] // LANES
    de = w_ref.shape[1] // 2
    for s in range(n_slab):
        t_scr[s] = w_ref[s * LANES:(s + 1) * LANES, :].T
    ev = [t_scr[s, pl.ds(0, de, stride=2), :].T for s in range(n_slab)]
    od = [t_scr[s, pl.ds(1, de, stride=2), :].T for s in range(n_slab)]
    wg_ref[...] = jnp.concatenate(ev, axis=0).astype(BF16)
    wl_ref[...] = jnp.concatenate(od, axis=0).astype(BF16)


def _split_gu(w_gu_all, layer):
    depth, E, D, de2 = w_gu_all.shape
    de = de2 // 2
    rows = E * D
    tr = SPLIT_ROWS if rows % SPLIT_ROWS == 0 else LANES
    nb = rows // tr
    wg, wl = pl.pallas_call(
        _split_gu_kernel,
        out_shape=(jax.ShapeDtypeStruct((rows, de), BF16), jax.ShapeDtypeStruct((rows, de), BF16)),
        grid=(nb,),
        in_specs=[pl.BlockSpec((tr, de2), lambda i: (layer * nb + i, 0))],
        out_specs=(pl.BlockSpec((tr, de), lambda i: (i, 0)), pl.BlockSpec((tr, de), lambda i: (i, 0))),
        scratch_shapes=[pltpu.VMEM((tr // LANES, de2, LANES), F32)],
        compiler_params=_cparams(1, 48),
        name="split_gu",
    )(w_gu_all.reshape(depth * rows, de2))
    return wg.reshape(E, D, de), wl.reshape(E, D, de)


def _moe_kernel(vb_ref, ve_ref, vlo_ref, nv_ref, x_ref, wg_ref, wl_ref, bg_ref, bl_ref, wd_ref, bd_ref, y_ref):
    v = pl.program_id(0)

    @pl.when(v < nv_ref[0])
    def _():
        x = x_ref[...]
        glu = jnp.minimum(_dot(x, wg_ref[...]) + bg_ref[...], SWIGLU_LIMIT)
        lin = jnp.clip(_dot(x, wl_ref[...]) + bl_ref[...], -SWIGLU_LIMIT, SWIGLU_LIMIT)
        act = glu * _sigmoid(SWIGLU_ALPHA * glu) * (lin + 1.0)
        y = _dot(act.astype(BF16), wd_ref[...].astype(BF16)) + bd_ref[...]
        lo = vlo_ref[v]

        @pl.when(lo == 0)
        def _():
            y_ref[...] = y.astype(BF16)

        @pl.when(lo > 0)
        def _():
            row = lax.broadcasted_iota(jnp.int32, y.shape, 0)
            y_ref[...] = jnp.where(row >= lo, y, y_ref[...].astype(F32)).astype(BF16)


def _moe_experts(vis_blk, vis_e, vis_lo, n_vis, xg, wg, wl, bg, bl, wd_all, layer, bd):
    n_rows, D = xg.shape
    E, _, DE = wg.shape
    n_visits = vis_blk.shape[0]
    wmap = lambda v, vb, ve, vlo, nv: (ve[v], 0, 0)
    dmap = lambda v, vb, ve, vlo, nv: (layer, ve[v], 0, 0)
    xmap = lambda v, vb, ve, vlo, nv: (vb[v], 0)
    grid_spec = pltpu.PrefetchScalarGridSpec(
        num_scalar_prefetch=4,
        grid=(n_visits,),
        in_specs=[
            pl.BlockSpec((MOE_ROWS, D), xmap),
            pl.BlockSpec((None, D, DE), wmap),
            pl.BlockSpec((None, D, DE), wmap),
            pl.BlockSpec((None, 1, DE), wmap),
            pl.BlockSpec((None, 1, DE), wmap),
            pl.BlockSpec((None, None, DE, D), dmap),
            pl.BlockSpec((None, 1, D), wmap),
        ],
        out_specs=pl.BlockSpec((MOE_ROWS, D), xmap),
    )
    return pl.pallas_call(
        _moe_kernel,
        out_shape=jax.ShapeDtypeStruct((n_rows, D), BF16),
        grid_spec=grid_spec,
        compiler_params=_cparams(1, 56),
        name="moe_experts",
    )(vis_blk, vis_e, vis_lo, n_vis, xg, wg, wl, bg, bl, wd_all, bd)


def _expert_lookup(table, idx):
    experts = jnp.arange(N_EXPERTS, dtype=jnp.int32)
    return jnp.sum(jnp.where(idx[..., None] == experts, table, 0), axis=-1)


def _route(top_idx, counts):
    n_tok = top_idx.shape[0]
    n_assign = n_tok * TOP_K
    assert n_assign % MOE_ROWS == 0
    n_blocks = n_assign // MOE_ROWS
    flat_e = top_idx.reshape(-1).astype(jnp.int32)
    iota = jnp.arange(n_assign, dtype=jnp.int32)
    _, order = lax.sort((flat_e, iota), num_keys=1)
    sorted_tok = order // TOP_K
    ends = jnp.cumsum(counts)
    starts = ends - counts
    first_blk = starts // MOE_ROWS
    n_vis_e = jnp.where(ends > starts, (ends - 1) // MOE_ROWS - first_blk + 1, 0)
    vis_end = jnp.cumsum(n_vis_e)
    vis_start = vis_end - n_vis_e
    n_vis = vis_end[-1]
    n_visits = n_blocks + N_EXPERTS - 1
    v = jnp.minimum(jnp.arange(n_visits, dtype=jnp.int32), n_vis - 1)
    vis_e = jnp.sum((vis_end[None, :] <= v[:, None]).astype(jnp.int32), axis=1)
    vis_blk = first_blk[vis_e] + (v - vis_start[vis_e])
    vis_lo = jnp.maximum(starts[vis_e] - vis_blk * MOE_ROWS, 0)
    return (starts, sorted_tok, vis_blk.astype(jnp.int32), vis_e.astype(jnp.int32),
            vis_lo.astype(jnp.int32), n_vis.astype(jnp.int32).reshape(1))


def _combine_kernel(yg_ref, gate_ref, h_ref, gf_ref, fg_ref, o_ref, *, final):
    gate = gate_ref[...]
    acc = yg_ref[0].astype(F32) * gate[:, 0:1]
    for k in range(1, TOP_K):
        acc = acc + yg_ref[k].astype(F32) * gate[:, k:k + 1]
    hn = h_ref[...] + gf_ref[...] * acc
    if final:
        ms = jnp.mean(hn * hn, axis=-1, keepdims=True)
        hn = hn * lax.rsqrt(ms + NORM_EPS) * fg_ref[...]
    o_ref[...] = hn


def _combine(yg, gates, h, gf, final_g, *, final, tm):
    B, T, D = h.shape
    bm = _bmap(gf)
    kern = functools.partial(_combine_kernel, final=final)
    return pl.pallas_call(
        kern,
        out_shape=jax.ShapeDtypeStruct((B, T, D), F32),
        grid=(B, T // tm),
        in_specs=[
            pl.BlockSpec((TOP_K, None, tm, D), lambda b, s: (0, b, s, 0)),
            pl.BlockSpec((None, tm, TOP_K), lambda b, s: (b, s, 0)),
            pl.BlockSpec((None, tm, D), lambda b, s: (b, s, 0)),
            pl.BlockSpec((None, 1, D), lambda b, s: (bm(b), 0, 0)),
            pl.BlockSpec((1, D), lambda b, s: (0, 0)),
        ],
        out_specs=pl.BlockSpec((None, tm, D), lambda b, s: (b, s, 0)),
        compiler_params=_cparams(2, 48),
        name="moe_combine",
    )(yg, gates, h, gf, final_g)


def _take_rows(a, idx):
    return a.at[idx].get(mode="promise_in_bounds")


def _block_diag(w):
    G, c, d = w.shape
    eye = jnp.eye(G, dtype=w.dtype)
    return (eye[:, None, :, None] * w[:, :, None, :]).reshape(G * c, G * d)


def kernel(x, c, ctx, c_ctx, ada_w, ada_b, norm_mix_g, norm_ffn_g, ev_w_in, ev_w_out, pool_w, pool_scale,
           att_sink, od_w_in, od_w_out, rw_mu, rw_w0, rw_w2, rw_a0, rw_a2, rw_g2, rw_k_k, rw_k_a, rw_r_k,
           rw_gn_g, rw_gn_b, router_w, router_b, exp_w_gu, exp_b_gu, exp_w_dn, exp_b_dn, final_g):
    B, S, D = x.shape
    C = ctx.shape[1]
    depth = ada_w.shape[0]
    DE = exp_w_dn.shape[2]

    n_mod = -(-(B + 1) // SUBLANES) * SUBLANES
    cvec = jnp.concatenate([c, c_ctx[None, :], jnp.zeros((n_mod - B - 1, D), F32)], axis=0)
    mods = _ada_all(cvec, ada_w, ada_b)

    cos_t, sin_t = _rope_tables(S)
    dft_x = dft_c = cs64 = scan_masks = head_blk = None
    if depth > 1:
        dft_x, dft_c, cs64 = _dft_tables(S), _dft_tables(C), _channel_dft()
        scan_masks, head_blk = _scan_masks()

    cnt0 = jnp.zeros((N_EXPERTS, LANES), F32)
    tm_x = 512 if S % 512 == 0 else S
    tm_o = 256 if S % 256 == 0 else S

    h, hc = x, ctx
    for i in range(depth):
        last = i == depth - 1
        j = i // 2
        m6 = mods[i].reshape(n_mod, 6, D)
        mx = [m6[:B, k][:, None, :] for k in range(6)]
        mc = [m6[B:B + 1, k][:, None, :] for k in range(6)]
        g_mix = norm_mix_g[i][None, :]
        g_ffn = norm_ffn_g[i][None, :]
        rw = router_w[i].T
        rb = router_b[i][:, None]

        if i % 2 == 0:
            assert not last, "an even final layer is not part of this block"
            w_in = ev_w_in[j].astype(BF16)
            w_out = ev_w_out[j].astype(BF16)
            pw_bd = _block_diag(pool_w[j]).astype(BF16)
            ps = pool_scale[j][None, :]
            sink = att_sink[j]
            pool_ux, qx, kvx = _even_in(h, mx[0], mx[1], g_mix, w_in, cos_t, sin_t, rope=True, tm=tm_x)
            pool_uc, qc, kvc = _even_in(hc, mc[0], mc[1], g_mix, w_in, cos_t[:C], sin_t[:C], rope=False, tm=C)
            h, fx_x, ti_x, tg_x, rk_x, cnt = _even_mix(sink, qx, kvx, kvc, pool_ux, h, mx[2], mx[3], mx[4], g_ffn,
                                                       w_out, pw_bd, ps, rw, rb, cnt0, local=True)
            hc, fx_c, ti_c, tg_c, rk_c, cnt = _even_mix(sink, qc, None, kvc, pool_uc, hc, mc[2], mc[3], mc[4], g_ffn,
                                                        w_out, pw_bd, ps, rw, rb, cnt, local=False)
        else:
            w_in = od_w_in[j].astype(BF16)
            w_out = od_w_out[j].astype(BF16)
            od_args = (w_in, rw_mu[j], rw_w0[j], rw_w2[j].astype(BF16), rw_a0[j], rw_a2[j].astype(BF16),
                       rw_k_k[j][None, :], rw_k_a[j][None, :], rw_r_k[j].reshape(1, RWKV_WIDTH))
            rx, vx, kkx, bonus_x, lwx, kdx, bdx, gcx, ux = _odd_in(h, mx[0], mx[1], g_mix, *od_args, tm=tm_o)
            rc, vc, kkc, bonus_c, lwc, kdc, bdc, gcc, uc = _odd_in(hc, mc[0], mc[1], g_mix, *od_args, tm=C)
            s0 = jnp.zeros(_scan_state_shape(B), F32)
            yf_c, yb_c, s_c = _scan(rc, vc, kkc, lwc, kdc, bdc, s0, scan_masks, head_blk)
            yf_x, yb_x, _ = _scan(rx, vx, kkx, lwx, kdx, bdx, s_c, scan_masks, head_blk)
            ro_args = (g_ffn, rw_gn_g[j][None, :], rw_gn_b[j][None, :], rw_g2[j].astype(BF16), w_out, rw, rb)
            h, fx_x, ti_x, tg_x, rk_x, cnt = _odd_out(yf_x, yb_x, bonus_x, gcx, ux, dft_x, cs64, h, mx[2], mx[3], mx[4],
                                                      *ro_args, cnt0, tm=tm_o)
            if not last:
                hc, fx_c, ti_c, tg_c, rk_c, cnt = _odd_out(yf_c, yb_c, bonus_c, gcc, uc, dft_c, cs64, hc, mc[2], mc[3],
                                                           mc[4], *ro_args, cnt, tm=C)

        n_x = B * S
        if last:
            fx_all = fx_x.reshape(n_x, D)
            ti_all = ti_x.transpose(0, 2, 1).reshape(n_x, TOP_K)
        else:
            fx_all = jnp.concatenate([fx_x.reshape(n_x, D), fx_c.reshape(B * C, D)], axis=0)
            ti_all = jnp.concatenate([ti_x.transpose(0, 2, 1).reshape(n_x, TOP_K),
                                      ti_c.transpose(0, 2, 1).reshape(B * C, TOP_K)], axis=0)
        counts = cnt[:, 0].astype(jnp.int32)
        starts, sorted_tok, vis_blk, vis_e, vis_lo, n_vis = _route(ti_all, counts)
        xg = _take_rows(fx_all, sorted_tok)
        wg, wl = _split_gu(exp_w_gu, i)
        bg = exp_b_gu[i][:, None, 0::2]
        bl = exp_b_gu[i][:, None, 1::2]
        bd = exp_b_dn[i][:, None, :]
        y = _moe_experts(vis_blk, vis_e, vis_lo, n_vis, xg, wg, wl, bg, bl, exp_w_dn, i, bd)
        tm_cx = 256 if S % 256 == 0 else S
        yg_x = _take_rows(y, (_expert_lookup(starts, ti_x) + rk_x).transpose(1, 0, 2))
        h = _combine(yg_x, tg_x.transpose(0, 2, 1), h, mx[5], final_g[None, :], final=last, tm=tm_cx)
        if not last:
            yg_c = _take_rows(y, (_expert_lookup(starts, ti_c) + rk_c).transpose(1, 0, 2))
            hc = _combine(yg_c, tg_c.transpose(0, 2, 1), hc, mc[5], final_g[None, :], final=False, tm=C)
    return h
```

```python
import functools
import math

import jax
import jax.numpy as jnp
import numpy as np
from jax import lax
from jax.experimental import pallas as pl
from jax.experimental.pallas import tpu as pltpu

F32 = jnp.float32
BF16 = jnp.bfloat16

GRID_W = 64
HEAD_DIM = 64
ROT_FREQS = HEAD_DIM // 4
ROPE_THETA = 10000.0
NORM_EPS = 1e-5
POOL_WINDOWS = (2, 4, 8, 16)
POOL_GROUP = 64
POOL_WIDTH = POOL_GROUP * len(POOL_WINDOWS)
POOL_HALO = max(POOL_WINDOWS) // 2
ATT_HEADS = 12
ATT_KV_HEADS = 3
ATT_GROUP = ATT_HEADS // ATT_KV_HEADS
ATT_WINDOW = 128
ATT_BLOCK = 128
ATT_Q = ATT_HEADS * HEAD_DIM
ATT_KV = ATT_KV_HEADS * HEAD_DIM
EVEN_IN = POOL_WIDTH + ATT_Q + 2 * ATT_KV
RWKV_HEADS = 12
RWKV_WIDTH = RWKV_HEADS * HEAD_DIM
DECAY_LORA = 64
ICLR_LORA = 64
GATE_LORA = 160
GN_EPS = 64e-5
LORA_LO = 3 * RWKV_WIDTH
STATE_HI = LORA_LO + 2 * DECAY_LORA + 2 * ICLR_LORA
RWKV_IN = STATE_HI + GATE_LORA
FNET_GROUPS = 4
FNET_GROUP = 64
FNET_WIDTH = FNET_GROUPS * FNET_GROUP
ODD_IN = RWKV_IN + FNET_WIDTH
N_EXPERTS = 32
TOP_K = 4
SWIGLU_LIMIT = 7.0
SWIGLU_ALPHA = 1.702

LANES = 128
SUBLANES = 8
SCAN_CHUNK = 64
SCAN_GROUP = 4
SPLIT_ROWS = 512
MOE_ROWS = 512
MASK_NEG = -1e30


def _cparams(n_axes, vmem_mb):
    return pltpu.CompilerParams(
        dimension_semantics=("arbitrary",) * n_axes,
        vmem_limit_bytes=vmem_mb * 1024 * 1024,
    )


def _dot(a, b):
    return jnp.dot(a, b, preferred_element_type=F32)


def _dot_nt(a, b):
    return lax.dot_general(a, b, (((1,), (1,)), ((), ())), preferred_element_type=F32)


def _dot_tn(a, b):
    return lax.dot_general(a, b, (((0,), (0,)), ((), ())), preferred_element_type=F32)


def _split(x):
    hi = x.astype(BF16)
    lo = (x - hi.astype(F32)).astype(BF16)
    return hi, lo


def _dot3(a, b):
    ah, al = _split(a)
    bh, bl = _split(b)
    return _dot(ah, bh) + (_dot(al, bh) + _dot(ah, bl))


def _dot_exact_lhs(a_exact, b):
    a16 = a_exact.astype(BF16)
    b1 = b.astype(BF16)
    r1 = b - b1.astype(F32)
    b2 = r1.astype(BF16)
    b3 = (r1 - b2.astype(F32)).astype(BF16)
    return _dot(a16, b1) + (_dot(a16, b2) + _dot(a16, b3))


def _modnorm(x, g, sh, sc):
    ms = jnp.mean(x * x, axis=-1, keepdims=True)
    xn = x * lax.rsqrt(ms + NORM_EPS) * g
    return xn * (1.0 + sc) + sh


def _sigmoid(x):
    return 1.0 / (1.0 + jnp.exp(-x))


def _head_sum(x):
    n = x.shape[1] // LANES
    lane = lax.broadcasted_iota(jnp.int32, (x.shape[0], LANES), 1)
    lo_mask = lane < HEAD_DIM
    parts = []
    for c in range(n):
        xc = x[:, c * LANES:(c + 1) * LANES]
        s_lo = jnp.sum(jnp.where(lo_mask, xc, 0.0), axis=-1, keepdims=True)
        s_hi = jnp.sum(jnp.where(lo_mask, 0.0, xc), axis=-1, keepdims=True)
        parts.append(jnp.where(lo_mask, s_lo, s_hi))
    return jnp.concatenate(parts, axis=1)


def _ada_kernel(c_ref, w_ref, b_ref, o_ref):
    x = c_ref[...]
    x = x * _sigmoid(x)
    o_ref[...] = _dot3(x, w_ref[...]) + b_ref[...]


def _ada_all(cvec, ada_w, ada_b):
    depth, d, n6 = ada_w.shape
    r = cvec.shape[0]
    tn = 1536 if n6 % 1536 == 0 else n6
    return pl.pallas_call(
        _ada_kernel,
        out_shape=jax.ShapeDtypeStruct((depth, r, n6), F32),
        grid=(depth, n6 // tn),
        in_specs=[
            pl.BlockSpec((r, d), lambda i, j: (0, 0)),
            pl.BlockSpec((None, d, tn), lambda i, j: (i, 0, j)),
            pl.BlockSpec((None, 1, tn), lambda i, j: (i, 0, j)),
        ],
        out_specs=pl.BlockSpec((None, r, tn), lambda i, j: (i, 0, j)),
        compiler_params=_cparams(2, 48),
        name="ada_mod",
    )(cvec, ada_w, ada_b.reshape(depth, 1, n6))


def _bmap(arr):
    if arr.shape[0] == 1:
        return lambda b: 0
    return lambda b: b


def _even_in_kernel(h_ref, sh_ref, sc_ref, g_ref, w_ref, cos_ref, sin_ref,
                    pool_ref, q_ref, kv_ref, *, rope):
    a = _modnorm(h_ref[...], g_ref[...], sh_ref[...], sc_ref[...]).astype(BF16)
    px = _dot(a, w_ref[...])
    pool_ref[...] = px[:, :POOL_WIDTH]
    n_chunks = (ATT_Q + 2 * ATT_KV) // LANES
    n_full = (ATT_Q + ATT_KV) // LANES
    outs = []
    if rope:
        lane = lax.broadcasted_iota(jnp.int32, (px.shape[0], LANES), 1)
        first = (lane % (2 * ROT_FREQS)) < ROT_FREQS
    for c in range(n_chunks):
        x = px[:, POOL_WIDTH + c * LANES:POOL_WIDTH + (c + 1) * LANES]
        if rope and c <= n_full:
            t0 = 0 if c < n_full else LANES
            cs = cos_ref[:, t0:t0 + LANES]
            sn = sin_ref[:, t0:t0 + LANES]
            rot = jnp.where(first, pltpu.roll(x, LANES - ROT_FREQS, 1), pltpu.roll(x, ROT_FREQS, 1))
            x = x * cs + rot * sn
        if c < ATT_Q // LANES:
            x = x * (HEAD_DIM ** -0.5)
        outs.append(x.astype(BF16))
    nq = ATT_Q // LANES
    q_ref[...] = jnp.concatenate(outs[:nq], axis=1)
    kv_ref[...] = jnp.concatenate(outs[nq:], axis=1)


def _even_in(h, sh, sc, g, w_bf, cos_t, sin_t, *, rope, tm):
    B, T, D = h.shape
    nt = T // tm
    bm = _bmap(sh)
    kern = functools.partial(_even_in_kernel, rope=rope)
    return pl.pallas_call(
        kern,
        out_shape=(
            jax.ShapeDtypeStruct((B, T, POOL_WIDTH), F32),
            jax.ShapeDtypeStruct((B, T, ATT_Q), BF16),
            jax.ShapeDtypeStruct((B, T, 2 * ATT_KV), BF16),
        ),
        grid=(nt, B),
        in_specs=[
            pl.BlockSpec((None, tm, D), lambda s, b: (b, s, 0)),
            pl.BlockSpec((None, 1, D), lambda s, b: (bm(b), 0, 0)),
            pl.BlockSpec((None, 1, D), lambda s, b: (bm(b), 0, 0)),
            pl.BlockSpec((1, D), lambda s, b: (0, 0)),
            pl.BlockSpec((D, EVEN_IN), lambda s, b: (0, 0)),
            pl.BlockSpec((tm, 2 * LANES), lambda s, b: (s, 0)),
            pl.BlockSpec((tm, 2 * LANES), lambda s, b: (s, 0)),
        ],
        out_specs=(
            pl.BlockSpec((None, tm, POOL_WIDTH), lambda s, b: (b, s, 0)),
            pl.BlockSpec((None, tm, ATT_Q), lambda s, b: (b, s, 0)),
            pl.BlockSpec((None, tm, 2 * ATT_KV), lambda s, b: (b, s, 0)),
        ),
        compiler_params=_cparams(2, 48),
        name="even_in",
    )(h, sh, sc, g, w_bf, cos_t, sin_t)


def _rope_tables(T):
    t = jnp.arange(T, dtype=jnp.int32)
    row = (t // GRID_W).astype(F32)
    col = (t % GRID_W).astype(F32)
    inv_freq = ROPE_THETA ** (-jnp.arange(ROT_FREQS, dtype=F32) / ROT_FREQS)
    ang_r = row[:, None] * inv_freq
    ang_c = col[:, None] * inv_freq
    cos_h = jnp.concatenate([jnp.cos(ang_r)] * 2 + [jnp.cos(ang_c)] * 2, axis=1)
    sin_h = jnp.concatenate([-jnp.sin(ang_r), jnp.sin(ang_r), -jnp.sin(ang_c), jnp.sin(ang_c)], axis=1)
    one = jnp.ones_like(cos_h)
    zero = jnp.zeros_like(sin_h)
    cos_t = jnp.concatenate([cos_h, cos_h, cos_h, one], axis=1)
    sin_t = jnp.concatenate([sin_h, sin_h, sin_h, zero], axis=1)
    return cos_t, sin_t


def _strict_upper(n):
    i = np.arange(n)
    return jnp.asarray(i[:, None] < i[None, :], BF16)


def _residual_and_router(h, y, gm, gf_norm, shf, scf, rw, rb, tri, first_step, cnt_in_ref,
                         hn_ref, fx_ref, ti_ref, tg_ref, rk_ref, cnt_ref, cnt_scr):
    hn = h + gm * y
    hn_ref[...] = hn
    fx = _modnorm(hn, gf_norm, shf, scf)
    fx_ref[...] = fx.astype(BF16)
    fh, fl = _split(fx)
    wh, wl = _split(rw)
    logits = _dot_nt(wh, fh) + (_dot_nt(wl, fh) + _dot_nt(wh, fl)) + rb
    rows = logits.shape[1]
    eidx = lax.broadcasted_iota(jnp.int32, (N_EXPERTS, rows), 0).astype(F32)
    vals, idxs = [], []
    for _ in range(TOP_K):
        mx = jnp.max(logits, axis=0, keepdims=True)
        ix = jnp.min(jnp.where(logits == mx, eidx, float(N_EXPERTS)), axis=0, keepdims=True)
        vals.append(mx)
        idxs.append(ix)
        logits = jnp.where(eidx == ix, MASK_NEG, logits)
    es = [jnp.exp(v - vals[0]) for v in vals]
    den = es[0]
    for e in es[1:]:
        den = den + e
    tg_ref[...] = jnp.concatenate(es, axis=0) / den
    ti_ref[...] = jnp.concatenate(idxs, axis=0).astype(jnp.int32)

    @pl.when(first_step)
    def _():
        cnt_scr[...] = cnt_in_ref[...]

    hits = [jnp.where(eidx == ix, 1.0, 0.0) for ix in idxs]
    onehot = hits[0]
    for hit in hits[1:]:
        onehot = onehot + hit
    before = cnt_scr[:, 0:1] + _dot(onehot.astype(BF16), tri)
    ranks = [jnp.sum(hit * before, axis=0, keepdims=True) for hit in hits]
    rk_ref[...] = jnp.concatenate(ranks, axis=0).astype(jnp.int32)
    total = cnt_scr[...] + jnp.sum(onehot, axis=1, keepdims=True)
    cnt_scr[...] = total
    cnt_ref[...] = total


def _even_mix_kernel(*refs, local, tq, T):
    it = iter(refs)
    sink_ref = next(it)
    q_ref = next(it)
    n_sub = tq // ATT_BLOCK if local else 1
    if local:
        kv_refs = [next(it) for _ in range(n_sub + 2)]
        bias_refs = [next(it) for _ in range(n_sub)]
    ckv_ref = next(it)
    up_ref, uc_ref, un_ref = next(it), next(it), next(it)
    h_ref, gm_ref, shf_ref, scf_ref, gfn_ref = next(it), next(it), next(it), next(it), next(it)
    wout_ref, pw_ref, ps_ref, rw_ref, rb_ref = next(it), next(it), next(it), next(it), next(it)
    tri_ref, cnt_in_ref = next(it), next(it)
    hn_ref, fx_ref, ti_ref, tg_ref, rk_ref, cnt_ref = next(it), next(it), next(it), next(it), next(it), next(it)
    cnt_scr = next(it)

    j = pl.program_id(1)
    nb = pl.num_programs(1)
    first_step = (pl.program_id(0) == 0) & (j == 0)

    tb = tq // n_sub
    cols = ATT_GROUP * tb
    col_head = lax.broadcasted_iota(jnp.int32, (1, cols), 1) // tb
    ckv = ckv_ref[...]
    att_subs = []
    for sb in range(n_sub):
        q = q_ref[sb * tb:(sb + 1) * tb, :]
        if local:
            kvl = jnp.concatenate([kv_refs[sb + k][...] for k in range(3)], axis=0)
            bias = bias_refs[sb][...]
        att_t = []
        for g in range(ATT_KV_HEADS):
            qs = jnp.concatenate(
                [q[:, (g * ATT_GROUP + i) * HEAD_DIM:(g * ATT_GROUP + i + 1) * HEAD_DIM] for i in range(ATT_GROUP)],
                axis=0)
            kc = ckv[:, g * HEAD_DIM:(g + 1) * HEAD_DIM]
            vc = ckv[:, ATT_KV + g * HEAD_DIM:ATT_KV + (g + 1) * HEAD_DIM]
            s_ctx = _dot_nt(kc, qs)
            sink = jnp.zeros((1, cols), F32)
            for i in range(ATT_GROUP):
                sink = jnp.where(col_head == i, sink_ref[g * ATT_GROUP + i], sink)
            if local:
                kl = kvl[:, g * HEAD_DIM:(g + 1) * HEAD_DIM]
                vl = kvl[:, ATT_KV + g * HEAD_DIM:ATT_KV + (g + 1) * HEAD_DIM]
                s_loc = _dot_nt(kl, qs) + bias
                m = jnp.maximum(jnp.maximum(jnp.max(s_loc, axis=0, keepdims=True),
                                            jnp.max(s_ctx, axis=0, keepdims=True)), sink)
                p_loc = jnp.exp(s_loc - m)
                p_ctx = jnp.exp(s_ctx - m)
                den = (jnp.sum(p_loc, axis=0, keepdims=True) + jnp.sum(p_ctx, axis=0, keepdims=True)
                       + jnp.exp(sink - m))
                o_t = _dot_tn(vl, p_loc.astype(BF16)) + _dot_tn(vc, p_ctx.astype(BF16))
            else:
                m = jnp.maximum(jnp.max(s_ctx, axis=0, keepdims=True), sink)
                p_ctx = jnp.exp(s_ctx - m)
                den = jnp.sum(p_ctx, axis=0, keepdims=True) + jnp.exp(sink - m)
                o_t = _dot_tn(vc, p_ctx.astype(BF16))
            o_t = o_t / den
            for i in range(ATT_GROUP):
                att_t.append(o_t[:, i * tb:(i + 1) * tb])
        att_subs.append(jnp.concatenate(att_t, axis=0))
    att_xt = jnp.concatenate(att_subs, axis=1).astype(BF16)

    u = uc_ref[...]
    up = jnp.where(j > 0, up_ref[...], 0.0)
    un = jnp.where(j < nb - 1, un_ref[...], 0.0)
    e = jnp.concatenate([up, u, un], axis=0)
    n_e = tq + 2 * POOL_HALO

    def shifted(x, k):
        return pltpu.roll(x, k % n_e, 0)

    a1 = e + shifted(e, 1)
    a2 = shifted(a1, 1) + shifted(a1, -1)
    a3 = shifted(a2, 2) + shifted(a2, -2)
    a4 = shifted(a3, 4) + shifted(a3, -4)
    lane = lax.broadcasted_iota(jnp.int32, (n_e, POOL_WIDTH), 1)
    grp = lane // POOL_GROUP
    win = jnp.where(grp == 0, a1, jnp.where(grp == 1, a2, jnp.where(grp == 2, a3, a4)))
    win = win[POOL_HALO:POOL_HALO + tq]
    t_i = j * tq + lax.broadcasted_iota(jnp.int32, (tq, POOL_WIDTH), 0)
    half = jnp.left_shift(1, lax.broadcasted_iota(jnp.int32, (tq, POOL_WIDTH), 1) // POOL_GROUP)
    cnt = (jnp.minimum(t_i + half, T) - jnp.maximum(t_i - half, 0)).astype(F32)
    pooled = win / cnt - u
    pool_x = _dot(pooled.astype(BF16), pw_ref[...]) * ps_ref[...]

    y = _dot(pool_x.astype(BF16), wout_ref[:POOL_WIDTH, :]) + _dot_tn(att_xt, wout_ref[POOL_WIDTH:, :])
    _residual_and_router(h_ref[...], y, gm_ref[...], gfn_ref[...], shf_ref[...], scf_ref[...],
                         rw_ref[...], rb_ref[...], tri_ref[...], first_step, cnt_in_ref,
                         hn_ref, fx_ref, ti_ref, tg_ref, rk_ref, cnt_ref, cnt_scr)


def _router_out_shapes(B, T, D):
    return (
        jax.ShapeDtypeStruct((B, T, D), F32),
        jax.ShapeDtypeStruct((B, T, D), BF16),
        jax.ShapeDtypeStruct((B, TOP_K, T), jnp.int32),
        jax.ShapeDtypeStruct((B, TOP_K, T), F32),
        jax.ShapeDtypeStruct((B, TOP_K, T), jnp.int32),
        jax.ShapeDtypeStruct((N_EXPERTS, LANES), F32),
    )


def _router_out_specs(tm, D):
    return (
        pl.BlockSpec((None, tm, D), lambda b, s: (b, s, 0)),
        pl.BlockSpec((None, tm, D), lambda b, s: (b, s, 0)),
        pl.BlockSpec((None, TOP_K, tm), lambda b, s: (b, 0, s)),
        pl.BlockSpec((None, TOP_K, tm), lambda b, s: (b, 0, s)),
        pl.BlockSpec((None, TOP_K, tm), lambda b, s: (b, 0, s)),
        pl.BlockSpec((N_EXPERTS, LANES), lambda b, s: (0, 0)),
    )


def _band_bias(tq, nb):
    c = np.arange(3 * ATT_BLOCK)[:, None]
    qq = (np.arange(ATT_GROUP * tq) % tq)[None, :]
    k_rel = c - ATT_BLOCK
    out = []
    for first, last in ((True, nb == 1), (False, False), (nb == 1, True)):
        ok = np.abs(qq - k_rel) <= ATT_WINDOW
        if first:
            ok = ok & (k_rel >= 0)
        if last:
            ok = ok & (k_rel < tq)
        out.append(np.where(ok, 0.0, MASK_NEG))
    return jnp.asarray(np.stack(out), F32)


def _even_mix(sink, q, kv, ckv, pool_u, h, gm, shf, scf, gfn, wout, pw_bd, pscale, rw, rb, cnt_in, *, local):
    B, T, D = h.shape
    n_ctx = ckv.shape[1]
    n_sub = (2 if T % (2 * ATT_BLOCK) == 0 else 1) if local else 1
    tq = n_sub * ATT_BLOCK if local else T
    nb = T // tq
    nkb = T // ATT_BLOCK
    r8 = tq // SUBLANES
    n8 = T // SUBLANES
    bm = _bmap(gm)
    kern = functools.partial(_even_mix_kernel, local=local, tq=tq, T=T)
    vec = lambda: pl.BlockSpec((None, 1, D), lambda b, j: (bm(b), 0, 0))
    full = lambda a: pl.BlockSpec(a.shape, lambda b, j: (0,) * a.ndim)
    in_specs = [pl.BlockSpec(memory_space=pltpu.SMEM),
                pl.BlockSpec((None, tq, ATT_Q), lambda b, j: (b, j, 0))]
    args = [sink, q]
    if local:
        def kv_spec(k):
            return pl.BlockSpec((None, ATT_BLOCK, 2 * ATT_KV),
                                lambda b, j: (b, jnp.clip(j * n_sub - 1 + k, 0, nkb - 1), 0))

        def bias_spec(sb):
            def imap(b, j):
                jb = j * n_sub + sb
                return (jnp.where(jb == 0, 0, jnp.where(jb == nkb - 1, 2, 1)), 0, 0)
            return pl.BlockSpec((None, 3 * ATT_BLOCK, ATT_GROUP * ATT_BLOCK), imap)

        band = _band_bias(ATT_BLOCK, nkb)
        in_specs += [kv_spec(k) for k in range(n_sub + 2)] + [bias_spec(sb) for sb in range(n_sub)]
        args += [kv] * (n_sub + 2) + [band] * n_sub
    in_specs += [
        pl.BlockSpec((None, n_ctx, 2 * ATT_KV), lambda b, j: (b, 0, 0)),
        pl.BlockSpec((None, SUBLANES, POOL_WIDTH), lambda b, j: (b, jnp.maximum(j * r8 - 1, 0), 0)),
        pl.BlockSpec((None, tq, POOL_WIDTH), lambda b, j: (b, j, 0)),
        pl.BlockSpec((None, SUBLANES, POOL_WIDTH), lambda b, j: (b, jnp.minimum((j + 1) * r8, n8 - 1), 0)),
        pl.BlockSpec((None, tq, D), lambda b, j: (b, j, 0)),
        vec(), vec(), vec(), full(gfn), full(wout), full(pw_bd), full(pscale), full(rw), full(rb),
    ]
    tri = _strict_upper(tq)
    in_specs += [full(tri), full(cnt_in)]
    args += [ckv, pool_u, pool_u, pool_u, h, gm, shf, scf, gfn, wout, pw_bd, pscale, rw, rb, tri, cnt_in]
    return pl.pallas_call(
        kern,
        out_shape=_router_out_shapes(B, T, D),
        grid=(B, nb),
        in_specs=in_specs,
        out_specs=_router_out_specs(tq, D),
        scratch_shapes=[pltpu.VMEM((N_EXPERTS, LANES), F32)],
        compiler_params=_cparams(2, 48),
        name="even_mix_local" if local else "even_mix_ctx",
    )(*args)


def _odd_in_kernel(h_ref, hp_ref, hn_ref, sh_ref, sc_ref, g_ref, w_ref, mu_ref, w0_ref, w2_ref,
                   a0_ref, a2_ref, kk_ref, ka_ref, rk_ref,
                   r_out, v_out, kkn_out, bonus_out, lw_out, kd_out, bd_out, gc_out, fn_out, *, tm):
    s = pl.program_id(1)
    ns = pl.num_programs(1)
    g, sh, sc = g_ref[...], sh_ref[...], sc_ref[...]
    w = w_ref[...]
    a = _modnorm(h_ref[...], g, sh, sc).astype(BF16)
    px = _dot(a, w)
    fn_out[...] = px[:, RWKV_IN:]
    main = px[:, :RWKV_IN]
    ap = _modnorm(hp_ref[...], g, sh, sc).astype(BF16)
    an = _modnorm(hn_ref[...], g, sh, sc).astype(BF16)
    halo = _dot(jnp.concatenate([ap, an], axis=0), w[:, :RWKV_IN])
    prev_row = jnp.where(s > 0, halo[SUBLANES - 1:SUBLANES], 0.0)
    next_row = jnp.where(s < ns - 1, halo[SUBLANES:SUBLANES + 1], 0.0)
    row = lax.broadcasted_iota(jnp.int32, (tm, 1), 0)
    prev = jnp.where(row == 0, prev_row, pltpu.roll(main, 1, 0))
    nxt = jnp.where(row == tm - 1, next_row, pltpu.roll(main, tm - 1, 0))
    mu = mu_ref[...]
    fs = main + mu[0:1] * (prev - main) + mu[1:2] * (nxt - main)

    W = RWKV_WIDTH
    r = fs[:, :W]
    k = fs[:, W:2 * W]
    v = fs[:, 2 * W:3 * W]
    lora = fs[:, LORA_LO:STATE_HI]
    gc_out[...] = fs[:, STATE_HI:RWKV_IN]
    r_out[...] = r
    v_out[...] = v

    kx = k * kk_ref[...]
    nrm = jnp.sqrt(_head_sum(kx * kx))
    kkn = kx / jnp.maximum(nrm, 1e-12)
    kkn_out[...] = kkn
    ka = ka_ref[...]
    ksum = None
    for d in range(2):
        wd = lora[:, d * DECAY_LORA:(d + 1) * DECAY_LORA]
        o_a = 2 * DECAY_LORA
        ad = lora[:, o_a + d * ICLR_LORA:o_a + (d + 1) * ICLR_LORA]
        xw = _dot(jnp.tanh(wd).astype(BF16), w2_ref[d]) + w0_ref[d:d + 1]
        z = -xw
        softplus = jnp.maximum(z, 0.0) + jnp.log(1.0 + jnp.exp(-jnp.abs(z)))
        w_log = -softplus - 0.5
        lw_out[d] = -jnp.exp(w_log)
        xa = _dot(ad.astype(BF16), a2_ref[d]) + a0_ref[d:d + 1]
        a_d = _sigmoid(xa)
        k_d = k * (1.0 + (a_d - 1.0) * ka)
        kd_out[d] = k_d
        bd_out[d] = kkn * a_d
        ksum = k_d if ksum is None else ksum + k_d
    coef = _head_sum(r * ksum * rk_ref[...])
    bonus_out[...] = coef * v


def _odd_in(h, sh, sc, g, w_bf, mu, w0, w2_bf, a0, a2_bf, k_k, k_a, r_k, *, tm):
    B, T, D = h.shape
    ns = T // tm
    r8 = tm // SUBLANES
    n8 = T // SUBLANES
    bm = _bmap(sh)
    W = RWKV_WIDTH
    full = lambda a: pl.BlockSpec(a.shape, lambda b, s: (0,) * a.ndim)
    tok = lambda n: pl.BlockSpec((None, tm, n), lambda b, s: (b, s, 0))
    tok2 = lambda n: pl.BlockSpec((2, None, tm, n), lambda b, s: (0, b, s, 0))
    kern = functools.partial(_odd_in_kernel, tm=tm)
    return pl.pallas_call(
        kern,
        out_shape=(
            jax.ShapeDtypeStruct((B, T, W), F32),
            jax.ShapeDtypeStruct((B, T, W), F32),
            jax.ShapeDtypeStruct((B, T, W), F32),
            jax.ShapeDtypeStruct((B, T, W), F32),
            jax.ShapeDtypeStruct((2, B, T, W), F32),
            jax.ShapeDtypeStruct((2, B, T, W), F32),
            jax.ShapeDtypeStruct((2, B, T, W), F32),
            jax.ShapeDtypeStruct((B, T, GATE_LORA), F32),
            jax.ShapeDtypeStruct((B, T, FNET_WIDTH), F32),
        ),
        grid=(B, ns),
        in_specs=[
            pl.BlockSpec((None, tm, D), lambda b, s: (b, s, 0)),
            pl.BlockSpec((None, SUBLANES, D), lambda b, s: (b, jnp.maximum(s * r8 - 1, 0), 0)),
            pl.BlockSpec((None, SUBLANES, D), lambda b, s: (b, jnp.minimum((s + 1) * r8, n8 - 1), 0)),
            pl.BlockSpec((None, 1, D), lambda b, s: (bm(b), 0, 0)),
            pl.BlockSpec((None, 1, D), lambda b, s: (bm(b), 0, 0)),
            full(g), full(w_bf), full(mu), full(w0), full(w2_bf), full(a0), full(a2_bf),
            full(k_k), full(k_a), full(r_k),
        ],
        out_specs=(tok(W), tok(W), tok(W), tok(W), tok2(W), tok2(W), tok2(W), tok(GATE_LORA), tok(FNET_WIDTH)),
        compiler_params=_cparams(2, 60),
        name="odd_in",
    )(h, h, h, sh, sc, g, w_bf, mu, w0, w2_bf, a0, a2_bf, k_k, k_a, r_k)


SCAN_LEVELS = tuple(2 ** k for k in range(int(math.log2(SCAN_CHUNK))))
MASK_BEFORE, MASK_UPTO, MASK_LEVEL0 = 0, 1, 2


def _scan_masks():
    R = SCAN_GROUP * SCAN_CHUNK
    idx = np.arange(R)
    same = (idx[:, None] // SCAN_CHUNK) == (idx[None, :] // SCAN_CHUNK)
    t = (idx % SCAN_CHUNK)[:, None]
    s = (idx % SCAN_CHUNK)[None, :]
    out = []
    for sign in (1, -1):
        order = (t - s) * sign
        ms = [same & (order > 0), same & (order >= 0)]
        for m in SCAN_LEVELS:
            ms.append(same & (t // (2 * m) == s // (2 * m)) & ((((t // m) % 2) - ((s // m) % 2)) * sign == 1))
        out.append(np.stack(ms))
    return jnp.asarray(np.stack(out), F32), jnp.asarray(same, F32)


def _scan_kernel(rf_ref, vf_ref, kkf_ref, rb_ref, vb_ref, kkb_ref, lwf_ref, kdf_ref, bdf_ref,
                 lwb_ref, kdb_ref, bdb_ref, mask_ref, hb_ref, s0_ref, yf_ref, yb_ref, sT_ref, s_scr, *, L):
    c = pl.program_id(1)
    nc = pl.num_programs(1)
    R = SCAN_GROUP * L
    GW = SCAN_GROUP * HEAD_DIM

    @pl.when(c == 0)
    def _():
        s_scr[...] = s0_ref[...]

    head_blk = hb_ref[...]

    def wide(x):
        return (jnp.concatenate([x] * SCAN_GROUP, axis=0) * head_blk).astype(BF16)

    NG = RWKV_HEADS // SCAN_GROUP
    dirs = ((0, (rf_ref, vf_ref, kkf_ref, lwf_ref, kdf_ref, bdf_ref)),
            (1, (rb_ref, vb_ref, kkb_ref, lwb_ref, kdb_ref, bdb_ref)))
    wides = {k: [] for k in ("kt", "rt", "kh", "bh", "v", "ke", "be")}
    g_tots = []
    for d, (r_ref, v_ref, kk_ref, lw_ref, kd_ref, bd_ref) in dirs:
        lw = lw_ref[...]
        c_in = _dot_exact_lhs(mask_ref[d, MASK_UPTO][:L, :L], lw)
        c_ex = c_in - lw
        tot = jnp.sum(lw, axis=0, keepdims=True)
        kd = kd_ref[...]
        bd = bd_ref[...]
        inv = jnp.exp(-c_in)
        end = jnp.exp(tot - c_in)
        g_tot = jnp.exp(tot)
        cols = dict(
            kt=kk_ref[...] * jnp.exp(c_ex), rt=r_ref[...] * jnp.exp(c_in), kh=kd * inv, bh=bd * inv,
            v=v_ref[...], ke=kd * end, be=bd * end)
        for g in range(NG):
            sl = slice(g * GW, (g + 1) * GW)
            for k, a in cols.items():
                wides[k].append(wide(a[:, sl]))
            g_tots.append(jnp.broadcast_to(g_tot[:, sl], (GW, GW)))
    w = {k: jnp.stack(a) for k, a in wides.items()}
    g_tot_all = jnp.stack(g_tots)

    def masked(x, idx):
        return jnp.concatenate([x[:NG] * mask_ref[0, idx][None], x[NG:] * mask_ref[1, idx][None]], axis=0)

    bmm = lambda a, b: jnp.einsum("bij,bjk->bik", a, b, preferred_element_type=F32)
    bmm_nt = lambda a, b: jnp.einsum("bik,bjk->bij", a, b, preferred_element_type=F32)
    bmm_tn = lambda a, b: jnp.einsum("bki,bkj->bij", a, b, preferred_element_type=F32)

    lhs = jnp.concatenate([w["kt"], w["rt"]], axis=1)
    s4 = bmm_nt(lhs, jnp.concatenate([w["kh"], w["bh"]], axis=1))
    a_kk = masked(s4[:, :R, :R], MASK_BEFORE).astype(BF16)
    m = masked(s4[:, :R, R:], MASK_BEFORE)
    a_rk = masked(s4[:, R:, :R], MASK_UPTO).astype(BF16)
    a_rb = masked(s4[:, R:, R:], MASK_UPTO).astype(BF16)
    eye = (mask_ref[0, MASK_UPTO] - mask_ref[0, MASK_BEFORE])[None]
    dinv = eye - masked(m, MASK_LEVEL0)
    for li in range(1, len(SCAN_LEVELS)):
        d16 = dinv.astype(BF16)
        e16 = masked(m, MASK_LEVEL0 + li).astype(BF16)
        dinv = dinv - bmm(bmm(d16, e16).astype(BF16), d16)
    s_all = s_scr[...].reshape(2 * NG, GW, GW)
    ks = bmm_nt(lhs, s_all.astype(BF16))
    u16 = bmm(dinv.astype(BF16), (ks[:, :R] + bmm(a_kk, w["v"])).astype(BF16)).astype(BF16)
    yw = ks[:, R:] + bmm(a_rk, w["v"]) - bmm(a_rb, u16)
    y_all = yw[:, 0:L]
    for hh in range(1, SCAN_GROUP):
        y_all = y_all + yw[:, hh * L:(hh + 1) * L]
    yf_ref[...] = jnp.concatenate([y_all[g] for g in range(NG)], axis=1)
    yb_ref[...] = jnp.concatenate([y_all[NG + g] for g in range(NG)], axis=1)
    vu = jnp.concatenate([w["v"], -u16], axis=1)
    ke = jnp.concatenate([w["ke"], w["be"]], axis=1)
    s_scr[...] = (s_all * g_tot_all + bmm_tn(vu, ke)).reshape(2, NG, GW, GW)

    @pl.when(c == nc - 1)
    def _():
        sT_ref[...] = s_scr[...]


def _scan_state_shape(B):
    gw = SCAN_GROUP * HEAD_DIM
    return (2, B, RWKV_HEADS // SCAN_GROUP, gw, gw)


def _scan(r, v, kk, lw, kd, bd, s0, masks, head_blk):
    B, T, W = r.shape
    L = SCAN_CHUNK
    assert L == HEAD_DIM and T % L == 0
    nc = T // L
    st_shape = _scan_state_shape(B)
    fwd = pl.BlockSpec((None, L, W), lambda b, c: (b, c, 0))
    bwd = pl.BlockSpec((None, L, W), lambda b, c: (b, nc - 1 - c, 0))
    fwd_d = pl.BlockSpec((None, None, L, W), lambda b, c: (0, b, c, 0))
    bwd_d = pl.BlockSpec((None, None, L, W), lambda b, c: (1, b, nc - 1 - c, 0))
    state = pl.BlockSpec((2, None) + st_shape[2:], lambda b, c: (0, b, 0, 0, 0))
    full = lambda a: pl.BlockSpec(a.shape, lambda b, c: (0,) * a.ndim)
    kern = functools.partial(_scan_kernel, L=L)
    return pl.pallas_call(
        kern,
        out_shape=(
            jax.ShapeDtypeStruct((B, T, W), F32),
            jax.ShapeDtypeStruct((B, T, W), F32),
            jax.ShapeDtypeStruct(st_shape, F32),
        ),
        grid=(B, nc),
        in_specs=[fwd, fwd, fwd, bwd, bwd, bwd, fwd_d, fwd_d, fwd_d, bwd_d, bwd_d, bwd_d,
                  full(masks), full(head_blk), state],
        out_specs=(fwd, bwd, state),
        scratch_shapes=[pltpu.VMEM((2,) + st_shape[2:], F32)],
        compiler_params=_cparams(2, 56),
        name="rwkv_scan",
    )(r, v, kk, r, v, kk, lw, kd, bd, lw, kd, bd, masks, head_blk, s0)


def _odd_out_kernel(yf_ref, yb_ref, bonus_ref, gc_ref, u_ref, dft_ref, cs_ref, h_ref, gm_ref, shf_ref, scf_ref,
                    gfn_ref, gng_ref, gnb_ref, g2_ref, wout_ref, rw_ref, rb_ref, tri_ref, cnt_in_ref,
                    hn_ref, fx_ref, ti_ref, tg_ref, rk_ref, cnt_ref, ucs_scr, cnt_scr, *, T):
    s = pl.program_id(1)
    first_step = (pl.program_id(0) == 0) & (s == 0)

    @pl.when(s == 0)
    def _():
        t = _dot(u_ref[...].astype(BF16), cs_ref[...])
        ucs_scr[0:T, :] = t[:, :FNET_WIDTH].astype(BF16)
        ucs_scr[T:2 * T, :] = t[:, FNET_WIDTH:].astype(BF16)

    f = _dot(dft_ref[...], ucs_scr[...])
    y = yf_ref[...] + yb_ref[...]
    mean = _head_sum(y) * (1.0 / HEAD_DIM)
    yc = y - mean
    var = _head_sum(yc * yc) * (1.0 / HEAD_DIM)
    yn = yc * lax.rsqrt(var + GN_EPS) * gng_ref[...] + gnb_ref[...]
    out = yn + bonus_ref[...]
    gate = _dot(_sigmoid(gc_ref[...]).astype(BF16), g2_ref[...])
    o = out * gate
    mix = jnp.concatenate([o, f], axis=1).astype(BF16)
    ymix = _dot(mix, wout_ref[...])
    _residual_and_router(h_ref[...], ymix, gm_ref[...], gfn_ref[...], shf_ref[...], scf_ref[...],
                         rw_ref[...], rb_ref[...], tri_ref[...], first_step, cnt_in_ref,
                         hn_ref, fx_ref, ti_ref, tg_ref, rk_ref, cnt_ref, cnt_scr)


def _odd_out(yf, yb, bonus, gc, u, dft, cs64, h, gm, shf, scf, gfn, gn_g, gn_b, g2_bf, wout, rw, rb, cnt_in, *, tm):
    B, T, D = h.shape
    ns = T // tm
    W = RWKV_WIDTH
    bm = _bmap(gm)
    vec = lambda: pl.BlockSpec((None, 1, D), lambda b, s: (bm(b), 0, 0))
    full = lambda a: pl.BlockSpec(a.shape, lambda b, s: (0,) * a.ndim)
    kern = functools.partial(_odd_out_kernel, T=T)
    tri = _strict_upper(tm)
    return pl.pallas_call(
        kern,
        out_shape=_router_out_shapes(B, T, D),
        grid=(B, ns),
        in_specs=[
            pl.BlockSpec((None, tm, W), lambda b, s: (b, s, 0)),
            pl.BlockSpec((None, tm, W), lambda b, s: (b, s, 0)),
            pl.BlockSpec((None, tm, W), lambda b, s: (b, s, 0)),
            pl.BlockSpec((None, tm, GATE_LORA), lambda b, s: (b, s, 0)),
            pl.BlockSpec((None, T, FNET_WIDTH), lambda b, s: (b, 0, 0)),
            pl.BlockSpec((tm, 2 * T), lambda b, s: (s, 0)),
            full(cs64),
            pl.BlockSpec((None, tm, D), lambda b, s: (b, s, 0)),
            vec(), vec(), vec(), full(gfn), full(gn_g), full(gn_b), full(g2_bf), full(wout), full(rw), full(rb),
            full(tri), full(cnt_in),
        ],
        out_specs=_router_out_specs(tm, D),
        scratch_shapes=[pltpu.VMEM((2 * T, FNET_WIDTH), BF16), pltpu.VMEM((N_EXPERTS, LANES), F32)],
        compiler_params=_cparams(2, 48),
        name="odd_out",
    )(yf, yb, bonus, gc, u, dft, cs64, h, gm, shf, scf, gfn, gn_g, gn_b, g2_bf, wout, rw, rb, tri, cnt_in)


def _dft_tables(T):
    t = jnp.arange(T, dtype=jnp.int32)
    ang = ((t[:, None] * t[None, :]) % T).astype(F32) * (2.0 * math.pi / T)
    scale = 1.0 / math.sqrt(T * FNET_GROUP)
    return (jnp.concatenate([jnp.cos(ang), -jnp.sin(ang)], axis=1) * scale).astype(BF16)


def _channel_dft():
    c = jnp.arange(FNET_GROUP, dtype=jnp.int32)
    ang = ((c[:, None] * c[None, :]) % FNET_GROUP).astype(F32) * (2.0 * math.pi / FNET_GROUP)
    eye = jnp.eye(FNET_GROUPS, dtype=F32)
    return jnp.concatenate([jnp.kron(eye, jnp.cos(ang)), jnp.kron(eye, jnp.sin(ang))], axis=1).astype(BF16)


def _split_gu_kernel(w_ref, wg_ref, wl_ref, t_scr):
    n_slab = w_ref.shape[0] // LANES
    de = w_ref.shape[1] // 2
    for s in range(n_slab):
        t_scr[s] = w_ref[s * LANES:(s + 1) * LANES, :].T
    ev = [t_scr[s, pl.ds(0, de, stride=2), :].T for s in range(n_slab)]
    od = [t_scr[s, pl.ds(1, de, stride=2), :].T for s in range(n_slab)]
    wg_ref[...] = jnp.concatenate(ev, axis=0).astype(BF16)
    wl_ref[...] = jnp.concatenate(od, axis=0).astype(BF16)


def _split_gu(w_gu_all, layer):
    depth, E, D, de2 = w_gu_all.shape
    de = de2 // 2
    rows = E * D
    tr = SPLIT_ROWS if rows % SPLIT_ROWS == 0 else LANES
    nb = rows // tr
    wg, wl = pl.pallas_call(
        _split_gu_kernel,
        out_shape=(jax.ShapeDtypeStruct((rows, de), BF16), jax.ShapeDtypeStruct((rows, de), BF16)),
        grid=(nb,),
        in_specs=[pl.BlockSpec((tr, de2), lambda i: (layer * nb + i, 0))],
        out_specs=(pl.BlockSpec((tr, de), lambda i: (i, 0)), pl.BlockSpec((tr, de), lambda i: (i, 0))),
        scratch_shapes=[pltpu.VMEM((tr // LANES, de2, LANES), F32)],
        compiler_params=_cparams(1, 48),
        name="split_gu",
    )(w_gu_all.reshape(depth * rows, de2))
    return wg.reshape(E, D, de), wl.reshape(E, D, de)


def _moe_kernel(vb_ref, ve_ref, vlo_ref, nv_ref, x_ref, wg_ref, wl_ref, bg_ref, bl_ref, wd_ref, bd_ref, y_ref):
    v = pl.program_id(0)

    @pl.when(v < nv_ref[0])
    def _():
        x = x_ref[...]
        glu = jnp.minimum(_dot(x, wg_ref[...]) + bg_ref[...], SWIGLU_LIMIT)
        lin = jnp.clip(_dot(x, wl_ref[...]) + bl_ref[...], -SWIGLU_LIMIT, SWIGLU_LIMIT)
        act = glu * _sigmoid(SWIGLU_ALPHA * glu) * (lin + 1.0)
        y = _dot(act.astype(BF16), wd_ref[...].astype(BF16)) + bd_ref[...]
        lo = vlo_ref[v]

        @pl.when(lo == 0)
        def _():
            y_ref[...] = y.astype(BF16)

        @pl.when(lo > 0)
        def _():
            row = lax.broadcasted_iota(jnp.int32, y.shape, 0)
            y_ref[...] = jnp.where(row >= lo, y, y_ref[...].astype(F32)).astype(BF16)


def _moe_experts(vis_blk, vis_e, vis_lo, n_vis, xg, wg, wl, bg, bl, wd_all, layer, bd):
    n_rows, D = xg.shape
    E, _, DE = wg.shape
    n_visits = vis_blk.shape[0]
    wmap = lambda v, vb, ve, vlo, nv: (ve[v], 0, 0)
    dmap = lambda v, vb, ve, vlo, nv: (layer, ve[v], 0, 0)
    xmap = lambda v, vb, ve, vlo, nv: (vb[v], 0)
    grid_spec = pltpu.PrefetchScalarGridSpec(
        num_scalar_prefetch=4,
        grid=(n_visits,),
        in_specs=[
            pl.BlockSpec((MOE_ROWS, D), xmap),
            pl.BlockSpec((None, D, DE), wmap),
            pl.BlockSpec((None, D, DE), wmap),
            pl.BlockSpec((None, 1, DE), wmap),
            pl.BlockSpec((None, 1, DE), wmap),
            pl.BlockSpec((None, None, DE, D), dmap),
            pl.BlockSpec((None, 1, D), wmap),
        ],
        out_specs=pl.BlockSpec((MOE_ROWS, D), xmap),
    )
    return pl.pallas_call(
        _moe_kernel,
        out_shape=jax.ShapeDtypeStruct((n_rows, D), BF16),
        grid_spec=grid_spec,
        compiler_params=_cparams(1, 56),
        name="moe_experts",
    )(vis_blk, vis_e, vis_lo, n_vis, xg, wg, wl, bg, bl, wd_all, bd)


def _expert_lookup(table, idx):
    experts = jnp.arange(N_EXPERTS, dtype=jnp.int32)
    return jnp.sum(jnp.where(idx[..., None] == experts, table, 0), axis=-1)


def _route(top_idx, counts):
    n_tok = top_idx.shape[0]
    n_assign = n_tok * TOP_K
    assert n_assign % MOE_ROWS == 0
    n_blocks = n_assign // MOE_ROWS
    flat_e = top_idx.reshape(-1).astype(jnp.int32)
    iota = jnp.arange(n_assign, dtype=jnp.int32)
    _, order = lax.sort((flat_e, iota), num_keys=1)
    sorted_tok = order // TOP_K
    ends = jnp.cumsum(counts)
    starts = ends - counts
    first_blk = starts // MOE_ROWS
    n_vis_e = jnp.where(ends > starts, (ends - 1) // MOE_ROWS - first_blk + 1, 0)
    vis_end = jnp.cumsum(n_vis_e)
    vis_start = vis_end - n_vis_e
    n_vis = vis_end[-1]
    n_visits = n_blocks + N_EXPERTS - 1
    v = jnp.minimum(jnp.arange(n_visits, dtype=jnp.int32), n_vis - 1)
    vis_e = jnp.sum((vis_end[None, :] <= v[:, None]).astype(jnp.int32), axis=1)
    vis_blk = first_blk[vis_e] + (v - vis_start[vis_e])
    vis_lo = jnp.maximum(starts[vis_e] - vis_blk * MOE_ROWS, 0)
    return (starts, sorted_tok, vis_blk.astype(jnp.int32), vis_e.astype(jnp.int32),
            vis_lo.astype(jnp.int32), n_vis.astype(jnp.int32).reshape(1))


def _combine_kernel(yg_ref, gate_ref, h_ref, gf_ref, fg_ref, o_ref, *, final):
    gate = gate_ref[...]
    acc = yg_ref[0].astype(F32) * gate[:, 0:1]
    for k in range(1, TOP_K):
        acc = acc + yg_ref[k].astype(F32) * gate[:, k:k + 1]
    hn = h_ref[...] + gf_ref[...] * acc
    if final:
        ms = jnp.mean(hn * hn, axis=-1, keepdims=True)
        hn = hn * lax.rsqrt(ms + NORM_EPS) * fg_ref[...]
    o_ref[...] = hn


def _combine(yg, gates, h, gf, final_g, *, final, tm):
    B, T, D = h.shape
    bm = _bmap(gf)
    kern = functools.partial(_combine_kernel, final=final)
    return pl.pallas_call(
        kern,
        out_shape=jax.ShapeDtypeStruct((B, T, D), F32),
        grid=(B, T // tm),
        in_specs=[
            pl.BlockSpec((TOP_K, None, tm, D), lambda b, s: (0, b, s, 0)),
            pl.BlockSpec((None, tm, TOP_K), lambda b, s: (b, s, 0)),
            pl.BlockSpec((None, tm, D), lambda b, s: (b, s, 0)),
            pl.BlockSpec((None, 1, D), lambda b, s: (bm(b), 0, 0)),
            pl.BlockSpec((1, D), lambda b, s: (0, 0)),
        ],
        out_specs=pl.BlockSpec((None, tm, D), lambda b, s: (b, s, 0)),
        compiler_params=_cparams(2, 48),
        name="moe_combine",
    )(yg, gates, h, gf, final_g)


def _take_rows(a, idx):
    return a.at[idx].get(mode="promise_in_bounds")


def _block_diag(w):
    G, c, d = w.shape
    eye = jnp.eye(G, dtype=w.dtype)
    return (eye[:, None, :, None] * w[:, :, None, :]).reshape(G * c, G * d)


def kernel(x, c, ctx, c_ctx, ada_w, ada_b, norm_mix_g, norm_ffn_g, ev_w_in, ev_w_out, pool_w, pool_scale,
           att_sink, od_w_in, od_w_out, rw_mu, rw_w0, rw_w2, rw_a0, rw_a2, rw_g2, rw_k_k, rw_k_a, rw_r_k,
           rw_gn_g, rw_gn_b, router_w, router_b, exp_w_gu, exp_b_gu, exp_w_dn, exp_b_dn, final_g):
    B, S, D = x.shape
    C = ctx.shape[1]
    depth = ada_w.shape[0]
    DE = exp_w_dn.shape[2]

    n_mod = -(-(B + 1) // SUBLANES) * SUBLANES
    cvec = jnp.concatenate([c, c_ctx[None, :], jnp.zeros((n_mod - B - 1, D), F32)], axis=0)
    mods = _ada_all(cvec, ada_w, ada_b)

    cos_t, sin_t = _rope_tables(S)
    dft_x = dft_c = cs64 = scan_masks = head_blk = None
    if depth > 1:
        dft_x, dft_c, cs64 = _dft_tables(S), _dft_tables(C), _channel_dft()
        scan_masks, head_blk = _scan_masks()

    cnt0 = jnp.zeros((N_EXPERTS, LANES), F32)
    tm_x = 512 if S % 512 == 0 else S
    tm_o = 256 if S % 256 == 0 else S

    h, hc = x, ctx
    for i in range(depth):
        last = i == depth - 1
        j = i // 2
        m6 = mods[i].reshape(n_mod, 6, D)
        mx = [m6[:B, k][:, None, :] for k in range(6)]
        mc = [m6[B:B + 1, k][:, None, :] for k in range(6)]
        g_mix = norm_mix_g[i][None, :]
        g_ffn = norm_ffn_g[i][None, :]
        rw = router_w[i].T
        rb = router_b[i][:, None]

        if i % 2 == 0:
            assert not last, "an even final layer is not part of this block"
            w_in = ev_w_in[j].astype(BF16)
            w_out = ev_w_out[j].astype(BF16)
            pw_bd = _block_diag(pool_w[j]).astype(BF16)
            ps = pool_scale[j][None, :]
            sink = att_sink[j]
            pool_ux, qx, kvx = _even_in(h, mx[0], mx[1], g_mix, w_in, cos_t, sin_t, rope=True, tm=tm_x)
            pool_uc, qc, kvc = _even_in(hc, mc[0], mc[1], g_mix, w_in, cos_t[:C], sin_t[:C], rope=False, tm=C)
            h, fx_x, ti_x, tg_x, rk_x, cnt = _even_mix(sink, qx, kvx, kvc, pool_ux, h, mx[2], mx[3], mx[4], g_ffn,
                                                       w_out, pw_bd, ps, rw, rb, cnt0, local=True)
            hc, fx_c, ti_c, tg_c, rk_c, cnt = _even_mix(sink, qc, None, kvc, pool_uc, hc, mc[2], mc[3], mc[4], g_ffn,
                                                        w_out, pw_bd, ps, rw, rb, cnt, local=False)
        else:
            w_in = od_w_in[j].astype(BF16)
            w_out = od_w_out[j].astype(BF16)
            od_args = (w_in, rw_mu[j], rw_w0[j], rw_w2[j].astype(BF16), rw_a0[j], rw_a2[j].astype(BF16),
                       rw_k_k[j][None, :], rw_k_a[j][None, :], rw_r_k[j].reshape(1, RWKV_WIDTH))
            rx, vx, kkx, bonus_x, lwx, kdx, bdx, gcx, ux = _odd_in(h, mx[0], mx[1], g_mix, *od_args, tm=tm_o)
            rc, vc, kkc, bonus_c, lwc, kdc, bdc, gcc, uc = _odd_in(hc, mc[0], mc[1], g_mix, *od_args, tm=C)
            s0 = jnp.zeros(_scan_state_shape(B), F32)
            yf_c, yb_c, s_c = _scan(rc, vc, kkc, lwc, kdc, bdc, s0, scan_masks, head_blk)
            yf_x, yb_x, _ = _scan(rx, vx, kkx, lwx, kdx, bdx, s_c, scan_masks, head_blk)
            ro_args = (g_ffn, rw_gn_g[j][None, :], rw_gn_b[j][None, :], rw_g2[j].astype(BF16), w_out, rw, rb)
            h, fx_x, ti_x, tg_x, rk_x, cnt = _odd_out(yf_x, yb_x, bonus_x, gcx, ux, dft_x, cs64, h, mx[2], mx[3], mx[4],
                                                      *ro_args, cnt0, tm=tm_x)
            if not last:
                hc, fx_c, ti_c, tg_c, rk_c, cnt = _odd_out(yf_c, yb_c, bonus_c, gcc, uc, dft_c, cs64, hc, mc[2], mc[3],
                                                           mc[4], *ro_args, cnt, tm=C)

        n_x = B * S
        if last:
            fx_all = fx_x.reshape(n_x, D)
            ti_all = ti_x.transpose(0, 2, 1).reshape(n_x, TOP_K)
        else:
            fx_all = jnp.concatenate([fx_x.reshape(n_x, D), fx_c.reshape(B * C, D)], axis=0)
            ti_all = jnp.concatenate([ti_x.transpose(0, 2, 1).reshape(n_x, TOP_K),
                                      ti_c.transpose(0, 2, 1).reshape(B * C, TOP_K)], axis=0)
        counts = cnt[:, 0].astype(jnp.int32)
        starts, sorted_tok, vis_blk, vis_e, vis_lo, n_vis = _route(ti_all, counts)
        xg = _take_rows(fx_all, sorted_tok)
        wg, wl = _split_gu(exp_w_gu, i)
        bg = exp_b_gu[i][:, None, 0::2]
        bl = exp_b_gu[i][:, None, 1::2]
        bd = exp_b_dn[i][:, None, :]
        y = _moe_experts(vis_blk, vis_e, vis_lo, n_vis, xg, wg, wl, bg, bl, exp_w_dn, i, bd)
        tm_cx = 256 if S % 256 == 0 else S
        yg_x = _take_rows(y, (_expert_lookup(starts, ti_x) + rk_x).transpose(1, 0, 2))
        h = _combine(yg_x, tg_x.transpose(0, 2, 1), h, mx[5], final_g[None, :], final=last, tm=tm_cx)
        if not last:
            yg_c = _take_rows(y, (_expert_lookup(starts, ti_c) + rk_c).transpose(1, 0, 2))
            hc = _combine(yg_c, tg_c.transpose(0, 2, 1), hc, mc[5], final_g[None, :], final=False, tm=C)
    return h
```

```python
import functools
import math

import jax
import jax.numpy as jnp
import numpy as np
from jax import lax
from jax.experimental import pallas as pl
from jax.experimental.pallas import tpu as pltpu

F32 = jnp.float32
BF16 = jnp.bfloat16

GRID_W = 64
HEAD_DIM = 64
ROT_FREQS = HEAD_DIM // 4
ROPE_THETA = 10000.0
NORM_EPS = 1e-5
POOL_WINDOWS = (2, 4, 8, 16)
POOL_GROUP = 64
POOL_WIDTH = POOL_GROUP * len(POOL_WINDOWS)
POOL_HALO = max(POOL_WINDOWS) // 2
ATT_HEADS = 12
ATT_KV_HEADS = 3
ATT_GROUP = ATT_HEADS // ATT_KV_HEADS
ATT_WINDOW = 128
ATT_BLOCK = 128
ATT_Q = ATT_HEADS * HEAD_DIM
ATT_KV = ATT_KV_HEADS * HEAD_DIM
EVEN_IN = POOL_WIDTH + ATT_Q + 2 * ATT_KV
RWKV_HEADS = 12
RWKV_WIDTH = RWKV_HEADS * HEAD_DIM
DECAY_LORA = 64
ICLR_LORA = 64
GATE_LORA = 160
GN_EPS = 64e-5
LORA_LO = 3 * RWKV_WIDTH
STATE_HI = LORA_LO + 2 * DECAY_LORA + 2 * ICLR_LORA
RWKV_IN = STATE_HI + GATE_LORA
FNET_GROUPS = 4
FNET_GROUP = 64
FNET_WIDTH = FNET_GROUPS * FNET_GROUP
ODD_IN = RWKV_IN + FNET_WIDTH
N_EXPERTS = 32
TOP_K = 4
SWIGLU_LIMIT = 7.0
SWIGLU_ALPHA = 1.702

LANES = 128
SUBLANES = 8
SCAN_CHUNK = 64
SCAN_GROUP = 4
SPLIT_ROWS = 512
MOE_ROWS = 512
MASK_NEG = -1e30


def _cparams(n_axes, vmem_mb):
    return pltpu.CompilerParams(
        dimension_semantics=("arbitrary",) * n_axes,
        vmem_limit_bytes=vmem_mb * 1024 * 1024,
    )


def _dot(a, b):
    return jnp.dot(a, b, preferred_element_type=F32)


def _dot_nt(a, b):
    return lax.dot_general(a, b, (((1,), (1,)), ((), ())), preferred_element_type=F32)


def _dot_tn(a, b):
    return lax.dot_general(a, b, (((0,), (0,)), ((), ())), preferred_element_type=F32)


def _split(x):
    hi = x.astype(BF16)
    lo = (x - hi.astype(F32)).astype(BF16)
    return hi, lo


def _dot3(a, b):
    ah, al = _split(a)
    bh, bl = _split(b)
    return _dot(ah, bh) + (_dot(al, bh) + _dot(ah, bl))


def _dot_exact_lhs(a_exact, b):
    a16 = a_exact.astype(BF16)
    b1 = b.astype(BF16)
    r1 = b - b1.astype(F32)
    b2 = r1.astype(BF16)
    b3 = (r1 - b2.astype(F32)).astype(BF16)
    return _dot(a16, b1) + (_dot(a16, b2) + _dot(a16, b3))


def _modnorm(x, g, sh, sc):
    ms = jnp.mean(x * x, axis=-1, keepdims=True)
    xn = x * lax.rsqrt(ms + NORM_EPS) * g
    return xn * (1.0 + sc) + sh


def _sigmoid(x):
    return 1.0 / (1.0 + jnp.exp(-x))


def _head_sum(x):
    n = x.shape[1] // LANES
    lane = lax.broadcasted_iota(jnp.int32, (x.shape[0], LANES), 1)
    lo_mask = lane < HEAD_DIM
    parts = []
    for c in range(n):
        xc = x[:, c * LANES:(c + 1) * LANES]
        s_lo = jnp.sum(jnp.where(lo_mask, xc, 0.0), axis=-1, keepdims=True)
        s_hi = jnp.sum(jnp.where(lo_mask, 0.0, xc), axis=-1, keepdims=True)
        parts.append(jnp.where(lo_mask, s_lo, s_hi))
    return jnp.concatenate(parts, axis=1)


def _ada_kernel(c_ref, w_ref, b_ref, o_ref):
    x = c_ref[...]
    x = x * _sigmoid(x)
    o_ref[...] = _dot3(x, w_ref[...]) + b_ref[...]


def _ada_all(cvec, ada_w, ada_b):
    depth, d, n6 = ada_w.shape
    r = cvec.shape[0]
    tn = 1536 if n6 % 1536 == 0 else n6
    return pl.pallas_call(
        _ada_kernel,
        out_shape=jax.ShapeDtypeStruct((depth, r, n6), F32),
        grid=(depth, n6 // tn),
        in_specs=[
            pl.BlockSpec((r, d), lambda i, j: (0, 0)),
            pl.BlockSpec((None, d, tn), lambda i, j: (i, 0, j)),
            pl.BlockSpec((None, 1, tn), lambda i, j: (i, 0, j)),
        ],
        out_specs=pl.BlockSpec((None, r, tn), lambda i, j: (i, 0, j)),
        compiler_params=_cparams(2, 48),
        name="ada_mod",
    )(cvec, ada_w, ada_b.reshape(depth, 1, n6))


def _bmap(arr):
    if arr.shape[0] == 1:
        return lambda b: 0
    return lambda b: b


def _even_in_kernel(h_ref, sh_ref, sc_ref, g_ref, w_ref, cos_ref, sin_ref,
                    pool_ref, q_ref, kv_ref, *, rope):
    a = _modnorm(h_ref[...], g_ref[...], sh_ref[...], sc_ref[...]).astype(BF16)
    px = _dot(a, w_ref[...])
    pool_ref[...] = px[:, :POOL_WIDTH]
    n_chunks = (ATT_Q + 2 * ATT_KV) // LANES
    n_full = (ATT_Q + ATT_KV) // LANES
    outs = []
    if rope:
        lane = lax.broadcasted_iota(jnp.int32, (px.shape[0], LANES), 1)
        first = (lane % (2 * ROT_FREQS)) < ROT_FREQS
    for c in range(n_chunks):
        x = px[:, POOL_WIDTH + c * LANES:POOL_WIDTH + (c + 1) * LANES]
        if rope and c <= n_full:
            t0 = 0 if c < n_full else LANES
            cs = cos_ref[:, t0:t0 + LANES]
            sn = sin_ref[:, t0:t0 + LANES]
            rot = jnp.where(first, pltpu.roll(x, LANES - ROT_FREQS, 1), pltpu.roll(x, ROT_FREQS, 1))
            x = x * cs + rot * sn
        if c < ATT_Q // LANES:
            x = x * (HEAD_DIM ** -0.5)
        outs.append(x.astype(BF16))
    nq = ATT_Q // LANES
    q_ref[...] = jnp.concatenate(outs[:nq], axis=1)
    kv_ref[...] = jnp.concatenate(outs[nq:], axis=1)


def _even_in(h, sh, sc, g, w_bf, cos_t, sin_t, *, rope, tm):
    B, T, D = h.shape
    nt = T // tm
    bm = _bmap(sh)
    kern = functools.partial(_even_in_kernel, rope=rope)
    return pl.pallas_call(
        kern,
        out_shape=(
            jax.ShapeDtypeStruct((B, T, POOL_WIDTH), F32),
            jax.ShapeDtypeStruct((B, T, ATT_Q), BF16),
            jax.ShapeDtypeStruct((B, T, 2 * ATT_KV), BF16),
        ),
        grid=(nt, B),
        in_specs=[
            pl.BlockSpec((None, tm, D), lambda s, b: (b, s, 0)),
            pl.BlockSpec((None, 1, D), lambda s, b: (bm(b), 0, 0)),
            pl.BlockSpec((None, 1, D), lambda s, b: (bm(b), 0, 0)),
            pl.BlockSpec((1, D), lambda s, b: (0, 0)),
            pl.BlockSpec((D, EVEN_IN), lambda s, b: (0, 0)),
            pl.BlockSpec((tm, 2 * LANES), lambda s, b: (s, 0)),
            pl.BlockSpec((tm, 2 * LANES), lambda s, b: (s, 0)),
        ],
        out_specs=(
            pl.BlockSpec((None, tm, POOL_WIDTH), lambda s, b: (b, s, 0)),
            pl.BlockSpec((None, tm, ATT_Q), lambda s, b: (b, s, 0)),
            pl.BlockSpec((None, tm, 2 * ATT_KV), lambda s, b: (b, s, 0)),
        ),
        compiler_params=_cparams(2, 48),
        name="even_in",
    )(h, sh, sc, g, w_bf, cos_t, sin_t)


def _rope_tables(T):
    t = jnp.arange(T, dtype=jnp.int32)
    row = (t // GRID_W).astype(F32)
    col = (t % GRID_W).astype(F32)
    inv_freq = ROPE_THETA ** (-jnp.arange(ROT_FREQS, dtype=F32) / ROT_FREQS)
    ang_r = row[:, None] * inv_freq
    ang_c = col[:, None] * inv_freq
    cos_h = jnp.concatenate([jnp.cos(ang_r)] * 2 + [jnp.cos(ang_c)] * 2, axis=1)
    sin_h = jnp.concatenate([-jnp.sin(ang_r), jnp.sin(ang_r), -jnp.sin(ang_c), jnp.sin(ang_c)], axis=1)
    one = jnp.ones_like(cos_h)
    zero = jnp.zeros_like(sin_h)
    cos_t = jnp.concatenate([cos_h, cos_h, cos_h, one], axis=1)
    sin_t = jnp.concatenate([sin_h, sin_h, sin_h, zero], axis=1)
    return cos_t, sin_t


def _strict_upper(n):
    i = np.arange(n)
    return jnp.asarray(i[:, None] < i[None, :], BF16)


def _residual_and_router(h, y, gm, gf_norm, shf, scf, rw, rb, tri, first_step, cnt_in_ref,
                         hn_ref, fx_ref, ti_ref, tg_ref, rk_ref, cnt_ref, cnt_scr):
    hn = h + gm * y
    hn_ref[...] = hn
    fx = _modnorm(hn, gf_norm, shf, scf)
    fx_ref[...] = fx.astype(BF16)
    fh, fl = _split(fx)
    wh, wl = _split(rw)
    logits = _dot_nt(wh, fh) + (_dot_nt(wl, fh) + _dot_nt(wh, fl)) + rb
    rows = logits.shape[1]
    eidx = lax.broadcasted_iota(jnp.int32, (N_EXPERTS, rows), 0).astype(F32)
    vals, idxs = [], []
    for _ in range(TOP_K):
        mx = jnp.max(logits, axis=0, keepdims=True)
        ix = jnp.min(jnp.where(logits == mx, eidx, float(N_EXPERTS)), axis=0, keepdims=True)
        vals.append(mx)
        idxs.append(ix)
        logits = jnp.where(eidx == ix, MASK_NEG, logits)
    es = [jnp.exp(v - vals[0]) for v in vals]
    den = es[0]
    for e in es[1:]:
        den = den + e
    tg_ref[...] = jnp.concatenate(es, axis=0) / den
    ti_ref[...] = jnp.concatenate(idxs, axis=0).astype(jnp.int32)

    @pl.when(first_step)
    def _():
        cnt_scr[...] = cnt_in_ref[...]

    hits = [jnp.where(eidx == ix, 1.0, 0.0) for ix in idxs]
    onehot = hits[0]
    for hit in hits[1:]:
        onehot = onehot + hit
    before = cnt_scr[:, 0:1] + _dot(onehot.astype(BF16), tri)
    ranks = [jnp.sum(hit * before, axis=0, keepdims=True) for hit in hits]
    rk_ref[...] = jnp.concatenate(ranks, axis=0).astype(jnp.int32)
    total = cnt_scr[...] + jnp.sum(onehot, axis=1, keepdims=True)
    cnt_scr[...] = total
    cnt_ref[...] = total


def _even_mix_kernel(*refs, local, tq, T):
    it = iter(refs)
    sink_ref = next(it)
    q_ref = next(it)
    n_sub = tq // ATT_BLOCK if local else 1
    if local:
        kv_refs = [next(it) for _ in range(n_sub + 2)]
        bias_refs = [next(it) for _ in range(n_sub)]
    ckv_ref = next(it)
    up_ref, uc_ref, un_ref = next(it), next(it), next(it)
    h_ref, gm_ref, shf_ref, scf_ref, gfn_ref = next(it), next(it), next(it), next(it), next(it)
    wout_ref, pw_ref, ps_ref, rw_ref, rb_ref = next(it), next(it), next(it), next(it), next(it)
    tri_ref, cnt_in_ref = next(it), next(it)
    hn_ref, fx_ref, ti_ref, tg_ref, rk_ref, cnt_ref = next(it), next(it), next(it), next(it), next(it), next(it)
    cnt_scr = next(it)

    j = pl.program_id(1)
    nb = pl.num_programs(1)
    first_step = (pl.program_id(0) == 0) & (j == 0)

    tb = tq // n_sub
    cols = ATT_GROUP * tb
    col_head = lax.broadcasted_iota(jnp.int32, (1, cols), 1) // tb
    ckv = ckv_ref[...]
    att_subs = []
    for sb in range(n_sub):
        q = q_ref[sb * tb:(sb + 1) * tb, :]
        if local:
            kvl = jnp.concatenate([kv_refs[sb + k][...] for k in range(3)], axis=0)
            bias = bias_refs[sb][...]
        att_t = []
        for g in range(ATT_KV_HEADS):
            qs = jnp.concatenate(
                [q[:, (g * ATT_GROUP + i) * HEAD_DIM:(g * ATT_GROUP + i + 1) * HEAD_DIM] for i in range(ATT_GROUP)],
                axis=0)
            kc = ckv[:, g * HEAD_DIM:(g + 1) * HEAD_DIM]
            vc = ckv[:, ATT_KV + g * HEAD_DIM:ATT_KV + (g + 1) * HEAD_DIM]
            s_ctx = _dot_nt(kc, qs)
            sink = jnp.zeros((1, cols), F32)
            for i in range(ATT_GROUP):
                sink = jnp.where(col_head == i, sink_ref[g * ATT_GROUP + i], sink)
            if local:
                kl = kvl[:, g * HEAD_DIM:(g + 1) * HEAD_DIM]
                vl = kvl[:, ATT_KV + g * HEAD_DIM:ATT_KV + (g + 1) * HEAD_DIM]
                s_loc = _dot_nt(kl, qs) + bias
                m = jnp.maximum(jnp.maximum(jnp.max(s_loc, axis=0, keepdims=True),
                                            jnp.max(s_ctx, axis=0, keepdims=True)), sink)
                p_loc = jnp.exp(s_loc - m)
                p_ctx = jnp.exp(s_ctx - m)
                den = (jnp.sum(p_loc, axis=0, keepdims=True) + jnp.sum(p_ctx, axis=0, keepdims=True)
                       + jnp.exp(sink - m))
                o_t = _dot_tn(vl, p_loc.astype(BF16)) + _dot_tn(vc, p_ctx.astype(BF16))
            else:
                m = jnp.maximum(jnp.max(s_ctx, axis=0, keepdims=True), sink)
                p_ctx = jnp.exp(s_ctx - m)
                den = jnp.sum(p_ctx, axis=0, keepdims=True) + jnp.exp(sink - m)
                o_t = _dot_tn(vc, p_ctx.astype(BF16))
            o_t = o_t / den
            for i in range(ATT_GROUP):
                att_t.append(o_t[:, i * tb:(i + 1) * tb])
        att_subs.append(jnp.concatenate(att_t, axis=0))
    att_xt = jnp.concatenate(att_subs, axis=1).astype(BF16)

    u = uc_ref[...]
    up = jnp.where(j > 0, up_ref[...], 0.0)
    un = jnp.where(j < nb - 1, un_ref[...], 0.0)
    e = jnp.concatenate([up, u, un], axis=0)
    n_e = tq + 2 * POOL_HALO

    def shifted(x, k):
        return pltpu.roll(x, k % n_e, 0)

    a1 = e + shifted(e, 1)
    a2 = shifted(a1, 1) + shifted(a1, -1)
    a3 = shifted(a2, 2) + shifted(a2, -2)
    a4 = shifted(a3, 4) + shifted(a3, -4)
    lane = lax.broadcasted_iota(jnp.int32, (n_e, POOL_WIDTH), 1)
    grp = lane // POOL_GROUP
    win = jnp.where(grp == 0, a1, jnp.where(grp == 1, a2, jnp.where(grp == 2, a3, a4)))
    win = win[POOL_HALO:POOL_HALO + tq]
    t_i = j * tq + lax.broadcasted_iota(jnp.int32, (tq, POOL_WIDTH), 0)
    half = jnp.left_shift(1, lax.broadcasted_iota(jnp.int32, (tq, POOL_WIDTH), 1) // POOL_GROUP)
    cnt = (jnp.minimum(t_i + half, T) - jnp.maximum(t_i - half, 0)).astype(F32)
    pooled = win / cnt - u
    pool_x = _dot(pooled.astype(BF16), pw_ref[...]) * ps_ref[...]

    y = _dot(pool_x.astype(BF16), wout_ref[:POOL_WIDTH, :]) + _dot_tn(att_xt, wout_ref[POOL_WIDTH:, :])
    _residual_and_router(h_ref[...], y, gm_ref[...], gfn_ref[...], shf_ref[...], scf_ref[...],
                         rw_ref[...], rb_ref[...], tri_ref[...], first_step, cnt_in_ref,
                         hn_ref, fx_ref, ti_ref, tg_ref, rk_ref, cnt_ref, cnt_scr)


def _router_out_shapes(B, T, D):
    return (
        jax.ShapeDtypeStruct((B, T, D), F32),
        jax.ShapeDtypeStruct((B, T, D), BF16),
        jax.ShapeDtypeStruct((B, TOP_K, T), jnp.int32),
        jax.ShapeDtypeStruct((B, TOP_K, T), F32),
        jax.ShapeDtypeStruct((B, TOP_K, T), jnp.int32),
        jax.ShapeDtypeStruct((N_EXPERTS, LANES), F32),
    )


def _router_out_specs(tm, D):
    return (
        pl.BlockSpec((None, tm, D), lambda b, s: (b, s, 0)),
        pl.BlockSpec((None, tm, D), lambda b, s: (b, s, 0)),
        pl.BlockSpec((None, TOP_K, tm), lambda b, s: (b, 0, s)),
        pl.BlockSpec((None, TOP_K, tm), lambda b, s: (b, 0, s)),
        pl.BlockSpec((None, TOP_K, tm), lambda b, s: (b, 0, s)),
        pl.BlockSpec((N_EXPERTS, LANES), lambda b, s: (0, 0)),
    )


def _band_bias(tq, nb):
    c = np.arange(3 * ATT_BLOCK)[:, None]
    qq = (np.arange(ATT_GROUP * tq) % tq)[None, :]
    k_rel = c - ATT_BLOCK
    out = []
    for first, last in ((True, nb == 1), (False, False), (nb == 1, True)):
        ok = np.abs(qq - k_rel) <= ATT_WINDOW
        if first:
            ok = ok & (k_rel >= 0)
        if last:
            ok = ok & (k_rel < tq)
        out.append(np.where(ok, 0.0, MASK_NEG))
    return jnp.asarray(np.stack(out), F32)


def _even_mix(sink, q, kv, ckv, pool_u, h, gm, shf, scf, gfn, wout, pw_bd, pscale, rw, rb, cnt_in, *, local):
    B, T, D = h.shape
    n_ctx = ckv.shape[1]
    n_sub = (4 if T % (4 * ATT_BLOCK) == 0 else 1) if local else 1
    tq = n_sub * ATT_BLOCK if local else T
    nb = T // tq
    nkb = T // ATT_BLOCK
    r8 = tq // SUBLANES
    n8 = T // SUBLANES
    bm = _bmap(gm)
    kern = functools.partial(_even_mix_kernel, local=local, tq=tq, T=T)
    vec = lambda: pl.BlockSpec((None, 1, D), lambda b, j: (bm(b), 0, 0))
    full = lambda a: pl.BlockSpec(a.shape, lambda b, j: (0,) * a.ndim)
    in_specs = [pl.BlockSpec(memory_space=pltpu.SMEM),
                pl.BlockSpec((None, tq, ATT_Q), lambda b, j: (b, j, 0))]
    args = [sink, q]
    if local:
        def kv_spec(k):
            return pl.BlockSpec((None, ATT_BLOCK, 2 * ATT_KV),
                                lambda b, j: (b, jnp.clip(j * n_sub - 1 + k, 0, nkb - 1), 0))

        def bias_spec(sb):
            def imap(b, j):
                jb = j * n_sub + sb
                return (jnp.where(jb == 0, 0, jnp.where(jb == nkb - 1, 2, 1)), 0, 0)
            return pl.BlockSpec((None, 3 * ATT_BLOCK, ATT_GROUP * ATT_BLOCK), imap)

        band = _band_bias(ATT_BLOCK, nkb)
        in_specs += [kv_spec(k) for k in range(n_sub + 2)] + [bias_spec(sb) for sb in range(n_sub)]
        args += [kv] * (n_sub + 2) + [band] * n_sub
    in_specs += [
        pl.BlockSpec((None, n_ctx, 2 * ATT_KV), lambda b, j: (b, 0, 0)),
        pl.BlockSpec((None, SUBLANES, POOL_WIDTH), lambda b, j: (b, jnp.maximum(j * r8 - 1, 0), 0)),
        pl.BlockSpec((None, tq, POOL_WIDTH), lambda b, j: (b, j, 0)),
        pl.BlockSpec((None, SUBLANES, POOL_WIDTH), lambda b, j: (b, jnp.minimum((j + 1) * r8, n8 - 1), 0)),
        pl.BlockSpec((None, tq, D), lambda b, j: (b, j, 0)),
        vec(), vec(), vec(), full(gfn), full(wout), full(pw_bd), full(pscale), full(rw), full(rb),
    ]
    tri = _strict_upper(tq)
    in_specs += [full(tri), full(cnt_in)]
    args += [ckv, pool_u, pool_u, pool_u, h, gm, shf, scf, gfn, wout, pw_bd, pscale, rw, rb, tri, cnt_in]
    return pl.pallas_call(
        kern,
        out_shape=_router_out_shapes(B, T, D),
        grid=(B, nb),
        in_specs=in_specs,
        out_specs=_router_out_specs(tq, D),
        scratch_shapes=[pltpu.VMEM((N_EXPERTS, LANES), F32)],
        compiler_params=_cparams(2, 48),
        name="even_mix_local" if local else "even_mix_ctx",
    )(*args)


def _odd_in_kernel(h_ref, hp_ref, hn_ref, sh_ref, sc_ref, g_ref, w_ref, mu_ref, w0_ref, w2_ref,
                   a0_ref, a2_ref, kk_ref, ka_ref, rk_ref,
                   r_out, v_out, kkn_out, bonus_out, lw_out, kd_out, bd_out, gc_out, fn_out, *, tm):
    s = pl.program_id(1)
    ns = pl.num_programs(1)
    g, sh, sc = g_ref[...], sh_ref[...], sc_ref[...]
    w = w_ref[...]
    a = _modnorm(h_ref[...], g, sh, sc).astype(BF16)
    px = _dot(a, w)
    fn_out[...] = px[:, RWKV_IN:]
    main = px[:, :RWKV_IN]
    ap = _modnorm(hp_ref[...], g, sh, sc).astype(BF16)
    an = _modnorm(hn_ref[...], g, sh, sc).astype(BF16)
    halo = _dot(jnp.concatenate([ap, an], axis=0), w[:, :RWKV_IN])
    prev_row = jnp.where(s > 0, halo[SUBLANES - 1:SUBLANES], 0.0)
    next_row = jnp.where(s < ns - 1, halo[SUBLANES:SUBLANES + 1], 0.0)
    row = lax.broadcasted_iota(jnp.int32, (tm, 1), 0)
    prev = jnp.where(row == 0, prev_row, pltpu.roll(main, 1, 0))
    nxt = jnp.where(row == tm - 1, next_row, pltpu.roll(main, tm - 1, 0))
    mu = mu_ref[...]
    fs = main + mu[0:1] * (prev - main) + mu[1:2] * (nxt - main)

    W = RWKV_WIDTH
    r = fs[:, :W]
    k = fs[:, W:2 * W]
    v = fs[:, 2 * W:3 * W]
    lora = fs[:, LORA_LO:STATE_HI]
    gc_out[...] = fs[:, STATE_HI:RWKV_IN]
    r_out[...] = r
    v_out[...] = v

    kx = k * kk_ref[...]
    nrm = jnp.sqrt(_head_sum(kx * kx))
    kkn = kx / jnp.maximum(nrm, 1e-12)
    kkn_out[...] = kkn
    ka = ka_ref[...]
    ksum = None
    for d in range(2):
        wd = lora[:, d * DECAY_LORA:(d + 1) * DECAY_LORA]
        o_a = 2 * DECAY_LORA
        ad = lora[:, o_a + d * ICLR_LORA:o_a + (d + 1) * ICLR_LORA]
        xw = _dot(jnp.tanh(wd).astype(BF16), w2_ref[d]) + w0_ref[d:d + 1]
        z = -xw
        softplus = jnp.maximum(z, 0.0) + jnp.log(1.0 + jnp.exp(-jnp.abs(z)))
        w_log = -softplus - 0.5
        lw_out[d] = -jnp.exp(w_log)
        xa = _dot(ad.astype(BF16), a2_ref[d]) + a0_ref[d:d + 1]
        a_d = _sigmoid(xa)
        k_d = k * (1.0 + (a_d - 1.0) * ka)
        kd_out[d] = k_d
        bd_out[d] = kkn * a_d
        ksum = k_d if ksum is None else ksum + k_d
    coef = _head_sum(r * ksum * rk_ref[...])
    bonus_out[...] = coef * v


def _odd_in(h, sh, sc, g, w_bf, mu, w0, w2_bf, a0, a2_bf, k_k, k_a, r_k, *, tm):
    B, T, D = h.shape
    ns = T // tm
    r8 = tm // SUBLANES
    n8 = T // SUBLANES
    bm = _bmap(sh)
    W = RWKV_WIDTH
    full = lambda a: pl.BlockSpec(a.shape, lambda b, s: (0,) * a.ndim)
    tok = lambda n: pl.BlockSpec((None, tm, n), lambda b, s: (b, s, 0))
    tok2 = lambda n: pl.BlockSpec((2, None, tm, n), lambda b, s: (0, b, s, 0))
    kern = functools.partial(_odd_in_kernel, tm=tm)
    return pl.pallas_call(
        kern,
        out_shape=(
            jax.ShapeDtypeStruct((B, T, W), F32),
            jax.ShapeDtypeStruct((B, T, W), F32),
            jax.ShapeDtypeStruct((B, T, W), F32),
            jax.ShapeDtypeStruct((B, T, W), F32),
            jax.ShapeDtypeStruct((2, B, T, W), F32),
            jax.ShapeDtypeStruct((2, B, T, W), F32),
            jax.ShapeDtypeStruct((2, B, T, W), F32),
            jax.ShapeDtypeStruct((B, T, GATE_LORA), F32),
            jax.ShapeDtypeStruct((B, T, FNET_WIDTH), F32),
        ),
        grid=(B, ns),
        in_specs=[
            pl.BlockSpec((None, tm, D), lambda b, s: (b, s, 0)),
            pl.BlockSpec((None, SUBLANES, D), lambda b, s: (b, jnp.maximum(s * r8 - 1, 0), 0)),
            pl.BlockSpec((None, SUBLANES, D), lambda b, s: (b, jnp.minimum((s + 1) * r8, n8 - 1), 0)),
            pl.BlockSpec((None, 1, D), lambda b, s: (bm(b), 0, 0)),
            pl.BlockSpec((None, 1, D), lambda b, s: (bm(b), 0, 0)),
            full(g), full(w_bf), full(mu), full(w0), full(w2_bf), full(a0), full(a2_bf),
            full(k_k), full(k_a), full(r_k),
        ],
        out_specs=(tok(W), tok(W), tok(W), tok(W), tok2(W), tok2(W), tok2(W), tok(GATE_LORA), tok(FNET_WIDTH)),
        compiler_params=_cparams(2, 60),
        name="odd_in",
    )(h, h, h, sh, sc, g, w_bf, mu, w0, w2_bf, a0, a2_bf, k_k, k_a, r_k)


SCAN_LEVELS = tuple(2 ** k for k in range(int(math.log2(SCAN_CHUNK))))
MASK_BEFORE, MASK_UPTO, MASK_LEVEL0 = 0, 1, 2


def _scan_masks():
    R = SCAN_GROUP * SCAN_CHUNK
    idx = np.arange(R)
    same = (idx[:, None] // SCAN_CHUNK) == (idx[None, :] // SCAN_CHUNK)
    t = (idx % SCAN_CHUNK)[:, None]
    s = (idx % SCAN_CHUNK)[None, :]
    out = []
    for sign in (1, -1):
        order = (t - s) * sign
        ms = [same & (order > 0), same & (order >= 0)]
        for m in SCAN_LEVELS:
            ms.append(same & (t // (2 * m) == s // (2 * m)) & ((((t // m) % 2) - ((s // m) % 2)) * sign == 1))
        out.append(np.stack(ms))
    return jnp.asarray(np.stack(out), F32), jnp.asarray(same, F32)


def _scan_kernel(rf_ref, vf_ref, kkf_ref, rb_ref, vb_ref, kkb_ref, lwf_ref, kdf_ref, bdf_ref,
                 lwb_ref, kdb_ref, bdb_ref, mask_ref, hb_ref, s0_ref, yf_ref, yb_ref, sT_ref, s_scr, *, L):
    c = pl.program_id(1)
    nc = pl.num_programs(1)
    R = SCAN_GROUP * L
    GW = SCAN_GROUP * HEAD_DIM

    @pl.when(c == 0)
    def _():
        s_scr[...] = s0_ref[...]

    head_blk = hb_ref[...]

    def wide(x):
        return (jnp.concatenate([x] * SCAN_GROUP, axis=0) * head_blk).astype(BF16)

    NG = RWKV_HEADS // SCAN_GROUP
    dirs = ((0, (rf_ref, vf_ref, kkf_ref, lwf_ref, kdf_ref, bdf_ref)),
            (1, (rb_ref, vb_ref, kkb_ref, lwb_ref, kdb_ref, bdb_ref)))
    wides = {k: [] for k in ("kt", "rt", "kh", "bh", "v", "ke", "be")}
    g_tots = []
    for d, (r_ref, v_ref, kk_ref, lw_ref, kd_ref, bd_ref) in dirs:
        lw = lw_ref[...]
        c_in = _dot_exact_lhs(mask_ref[d, MASK_UPTO][:L, :L], lw)
        c_ex = c_in - lw
        tot = jnp.sum(lw, axis=0, keepdims=True)
        kd = kd_ref[...]
        bd = bd_ref[...]
        inv = jnp.exp(-c_in)
        end = jnp.exp(tot - c_in)
        g_tot = jnp.exp(tot)
        cols = dict(
            kt=kk_ref[...] * jnp.exp(c_ex), rt=r_ref[...] * jnp.exp(c_in), kh=kd * inv, bh=bd * inv,
            v=v_ref[...], ke=kd * end, be=bd * end)
        for g in range(NG):
            sl = slice(g * GW, (g + 1) * GW)
            for k, a in cols.items():
                wides[k].append(wide(a[:, sl]))
            g_tots.append(jnp.broadcast_to(g_tot[:, sl], (GW, GW)))
    w = {k: jnp.stack(a) for k, a in wides.items()}
    g_tot_all = jnp.stack(g_tots)

    def masked(x, idx):
        return jnp.concatenate([x[:NG] * mask_ref[0, idx][None], x[NG:] * mask_ref[1, idx][None]], axis=0)

    bmm = lambda a, b: jnp.einsum("bij,bjk->bik", a, b, preferred_element_type=F32)
    bmm_nt = lambda a, b: jnp.einsum("bik,bjk->bij", a, b, preferred_element_type=F32)
    bmm_tn = lambda a, b: jnp.einsum("bki,bkj->bij", a, b, preferred_element_type=F32)

    lhs = jnp.concatenate([w["kt"], w["rt"]], axis=1)
    s4 = bmm_nt(lhs, jnp.concatenate([w["kh"], w["bh"]], axis=1))
    a_kk = masked(s4[:, :R, :R], MASK_BEFORE).astype(BF16)
    m = masked(s4[:, :R, R:], MASK_BEFORE)
    a_rk = masked(s4[:, R:, :R], MASK_UPTO).astype(BF16)
    a_rb = masked(s4[:, R:, R:], MASK_UPTO).astype(BF16)
    eye = (mask_ref[0, MASK_UPTO] - mask_ref[0, MASK_BEFORE])[None]
    dinv = eye - masked(m, MASK_LEVEL0)
    for li in range(1, len(SCAN_LEVELS)):
        d16 = dinv.astype(BF16)
        e16 = masked(m, MASK_LEVEL0 + li).astype(BF16)
        dinv = dinv - bmm(bmm(d16, e16).astype(BF16), d16)
    s_all = s_scr[...].reshape(2 * NG, GW, GW)
    ks = bmm_nt(lhs, s_all.astype(BF16))
    u16 = bmm(dinv.astype(BF16), (ks[:, :R] + bmm(a_kk, w["v"])).astype(BF16)).astype(BF16)
    yw = ks[:, R:] + bmm(a_rk, w["v"]) - bmm(a_rb, u16)
    y_all = yw[:, 0:L]
    for hh in range(1, SCAN_GROUP):
        y_all = y_all + yw[:, hh * L:(hh + 1) * L]
    yf_ref[...] = jnp.concatenate([y_all[g] for g in range(NG)], axis=1)
    yb_ref[...] = jnp.concatenate([y_all[NG + g] for g in range(NG)], axis=1)
    vu = jnp.concatenate([w["v"], -u16], axis=1)
    ke = jnp.concatenate([w["ke"], w["be"]], axis=1)
    s_scr[...] = (s_all * g_tot_all + bmm_tn(vu, ke)).reshape(2, NG, GW, GW)

    @pl.when(c == nc - 1)
    def _():
        sT_ref[...] = s_scr[...]


def _scan_state_shape(B):
    gw = SCAN_GROUP * HEAD_DIM
    return (2, B, RWKV_HEADS // SCAN_GROUP, gw, gw)


def _scan(r, v, kk, lw, kd, bd, s0, masks, head_blk):
    B, T, W = r.shape
    L = SCAN_CHUNK
    assert L == HEAD_DIM and T % L == 0
    nc = T // L
    st_shape = _scan_state_shape(B)
    fwd = pl.BlockSpec((None, L, W), lambda b, c: (b, c, 0))
    bwd = pl.BlockSpec((None, L, W), lambda b, c: (b, nc - 1 - c, 0))
    fwd_d = pl.BlockSpec((None, None, L, W), lambda b, c: (0, b, c, 0))
    bwd_d = pl.BlockSpec((None, None, L, W), lambda b, c: (1, b, nc - 1 - c, 0))
    state = pl.BlockSpec((2, None) + st_shape[2:], lambda b, c: (0, b, 0, 0, 0))
    full = lambda a: pl.BlockSpec(a.shape, lambda b, c: (0,) * a.ndim)
    kern = functools.partial(_scan_kernel, L=L)
    return pl.pallas_call(
        kern,
        out_shape=(
            jax.ShapeDtypeStruct((B, T, W), F32),
            jax.ShapeDtypeStruct((B, T, W), F32),
            jax.ShapeDtypeStruct(st_shape, F32),
        ),
        grid=(B, nc),
        in_specs=[fwd, fwd, fwd, bwd, bwd, bwd, fwd_d, fwd_d, fwd_d, bwd_d, bwd_d, bwd_d,
                  full(masks), full(head_blk), state],
        out_specs=(fwd, bwd, state),
        scratch_shapes=[pltpu.VMEM((2,) + st_shape[2:], F32)],
        compiler_params=_cparams(2, 56),
        name="rwkv_scan",
    )(r, v, kk, r, v, kk, lw, kd, bd, lw, kd, bd, masks, head_blk, s0)


def _odd_out_kernel(yf_ref, yb_ref, bonus_ref, gc_ref, u_ref, dft_ref, cs_ref, h_ref, gm_ref, shf_ref, scf_ref,
                    gfn_ref, gng_ref, gnb_ref, g2_ref, wout_ref, rw_ref, rb_ref, tri_ref, cnt_in_ref,
                    hn_ref, fx_ref, ti_ref, tg_ref, rk_ref, cnt_ref, ucs_scr, cnt_scr, *, T):
    s = pl.program_id(1)
    first_step = (pl.program_id(0) == 0) & (s == 0)

    @pl.when(s == 0)
    def _():
        t = _dot(u_ref[...].astype(BF16), cs_ref[...])
        ucs_scr[0:T, :] = t[:, :FNET_WIDTH].astype(BF16)
        ucs_scr[T:2 * T, :] = t[:, FNET_WIDTH:].astype(BF16)

    f = _dot(dft_ref[...], ucs_scr[...])
    y = yf_ref[...] + yb_ref[...]
    mean = _head_sum(y) * (1.0 / HEAD_DIM)
    yc = y - mean
    var = _head_sum(yc * yc) * (1.0 / HEAD_DIM)
    yn = yc * lax.rsqrt(var + GN_EPS) * gng_ref[...] + gnb_ref[...]
    out = yn + bonus_ref[...]
    gate = _dot(_sigmoid(gc_ref[...]).astype(BF16), g2_ref[...])
    o = out * gate
    mix = jnp.concatenate([o, f], axis=1).astype(BF16)
    ymix = _dot(mix, wout_ref[...])
    _residual_and_router(h_ref[...], ymix, gm_ref[...], gfn_ref[...], shf_ref[...], scf_ref[...],
                         rw_ref[...], rb_ref[...], tri_ref[...], first_step, cnt_in_ref,
                         hn_ref, fx_ref, ti_ref, tg_ref, rk_ref, cnt_ref, cnt_scr)


def _odd_out(yf, yb, bonus, gc, u, dft, cs64, h, gm, shf, scf, gfn, gn_g, gn_b, g2_bf, wout, rw, rb, cnt_in, *, tm):
    B, T, D = h.shape
    ns = T // tm
    W = RWKV_WIDTH
    bm = _bmap(gm)
    vec = lambda: pl.BlockSpec((None, 1, D), lambda b, s: (bm(b), 0, 0))
    full = lambda a: pl.BlockSpec(a.shape, lambda b, s: (0,) * a.ndim)
    kern = functools.partial(_odd_out_kernel, T=T)
    tri = _strict_upper(tm)
    return pl.pallas_call(
        kern,
        out_shape=_router_out_shapes(B, T, D),
        grid=(B, ns),
        in_specs=[
            pl.BlockSpec((None, tm, W), lambda b, s: (b, s, 0)),
            pl.BlockSpec((None, tm, W), lambda b, s: (b, s, 0)),
            pl.BlockSpec((None, tm, W), lambda b, s: (b, s, 0)),
            pl.BlockSpec((None, tm, GATE_LORA), lambda b, s: (b, s, 0)),
            pl.BlockSpec((None, T, FNET_WIDTH), lambda b, s: (b, 0, 0)),
            pl.BlockSpec((tm, 2 * T), lambda b, s: (s, 0)),
            full(cs64),
            pl.BlockSpec((None, tm, D), lambda b, s: (b, s, 0)),
            vec(), vec(), vec(), full(gfn), full(gn_g), full(gn_b), full(g2_bf), full(wout), full(rw), full(rb),
            full(tri), full(cnt_in),
        ],
        out_specs=_router_out_specs(tm, D),
        scratch_shapes=[pltpu.VMEM((2 * T, FNET_WIDTH), BF16), pltpu.VMEM((N_EXPERTS, LANES), F32)],
        compiler_params=_cparams(2, 48),
        name="odd_out",
    )(yf, yb, bonus, gc, u, dft, cs64, h, gm, shf, scf, gfn, gn_g, gn_b, g2_bf, wout, rw, rb, tri, cnt_in)


def _dft_tables(T):
    t = jnp.arange(T, dtype=jnp.int32)
    ang = ((t[:, None] * t[None, :]) % T).astype(F32) * (2.0 * math.pi / T)
    scale = 1.0 / math.sqrt(T * FNET_GROUP)
    return (jnp.concatenate([jnp.cos(ang), -jnp.sin(ang)], axis=1) * scale).astype(BF16)


def _channel_dft():
    c = jnp.arange(FNET_GROUP, dtype=jnp.int32)
    ang = ((c[:, None] * c[None, :]) % FNET_GROUP).astype(F32) * (2.0 * math.pi / FNET_GROUP)
    eye = jnp.eye(FNET_GROUPS, dtype=F32)
    return jnp.concatenate([jnp.kron(eye, jnp.cos(ang)), jnp.kron(eye, jnp.sin(ang))], axis=1).astype(BF16)


def _split_gu_kernel(w_ref, wg_ref, wl_ref, t_scr):
    n_slab = w_ref.shape[0] // LANES
    de = w_ref.shape[1] // 2
    for s in range(n_slab):
        t_scr[s] = w_ref[s * LANES:(s + 1) * LANES, :].T
    ev = [t_scr[s, pl.ds(0, de, stride=2), :].T for s in range(n_slab)]
    od = [t_scr[s, pl.ds(1, de, stride=2), :].T for s in range(n_slab)]
    wg_ref[...] = jnp.concatenate(ev, axis=0).astype(BF16)
    wl_ref[...] = jnp.concatenate(od, axis=0).astype(BF16)


def _split_gu(w_gu_all, layer):
    depth, E, D, de2 = w_gu_all.shape
    de = de2 // 2
    rows = E * D
    tr = SPLIT_ROWS if rows % SPLIT_ROWS == 0 else LANES
    nb = rows // tr
    wg, wl = pl.pallas_call(
        _split_gu_kernel,
        out_shape=(jax.ShapeDtypeStruct((rows, de), BF16), jax.ShapeDtypeStruct((rows, de), BF16)),
        grid=(nb,),
        in_specs=[pl.BlockSpec((tr, de2), lambda i: (layer * nb + i, 0))],
        out_specs=(pl.BlockSpec((tr, de), lambda i: (i, 0)), pl.BlockSpec((tr, de), lambda i: (i, 0))),
        scratch_shapes=[pltpu.VMEM((tr // LANES, de2, LANES), F32)],
        compiler_params=_cparams(1, 48),
        name="split_gu",
    )(w_gu_all.reshape(depth * rows, de2))
    return wg.reshape(E, D, de), wl.reshape(E, D, de)


def _moe_kernel(vb_ref, ve_ref, vlo_ref, nv_ref, x_ref, wg_ref, wl_ref, bg_ref, bl_ref, wd_ref, bd_ref, y_ref):
    v = pl.program_id(0)

    @pl.when(v < nv_ref[0])
    def _():
        x = x_ref[...]
        glu = jnp.minimum(_dot(x, wg_ref[...]) + bg_ref[...], SWIGLU_LIMIT)
        lin = jnp.clip(_dot(x, wl_ref[...]) + bl_ref[...], -SWIGLU_LIMIT, SWIGLU_LIMIT)
        act = glu * _sigmoid(SWIGLU_ALPHA * glu) * (lin + 1.0)
        y = _dot(act.astype(BF16), wd_ref[...].astype(BF16)) + bd_ref[...]
        lo = vlo_ref[v]

        @pl.when(lo == 0)
        def _():
            y_ref[...] = y.astype(BF16)

        @pl.when(lo > 0)
        def _():
            row = lax.broadcasted_iota(jnp.int32, y.shape, 0)
            y_ref[...] = jnp.where(row >= lo, y, y_ref[...].astype(F32)).astype(BF16)


def _moe_experts(vis_blk, vis_e, vis_lo, n_vis, xg, wg, wl, bg, bl, wd_all, layer, bd):
    n_rows, D = xg.shape
    E, _, DE = wg.shape
    n_visits = vis_blk.shape[0]
    wmap = lambda v, vb, ve, vlo, nv: (ve[v], 0, 0)
    dmap = lambda v, vb, ve, vlo, nv: (layer, ve[v], 0, 0)
    xmap = lambda v, vb, ve, vlo, nv: (vb[v], 0)
    grid_spec = pltpu.PrefetchScalarGridSpec(
        num_scalar_prefetch=4,
        grid=(n_visits,),
        in_specs=[
            pl.BlockSpec((MOE_ROWS, D), xmap),
            pl.BlockSpec((None, D, DE), wmap),
            pl.BlockSpec((None, D, DE), wmap),
            pl.BlockSpec((None, 1, DE), wmap),
            pl.BlockSpec((None, 1, DE), wmap),
            pl.BlockSpec((None, None, DE, D), dmap),
            pl.BlockSpec((None, 1, D), wmap),
        ],
        out_specs=pl.BlockSpec((MOE_ROWS, D), xmap),
    )
    return pl.pallas_call(
        _moe_kernel,
        out_shape=jax.ShapeDtypeStruct((n_rows, D), BF16),
        grid_spec=grid_spec,
        compiler_params=_cparams(1, 56),
        name="moe_experts",
    )(vis_blk, vis_e, vis_lo, n_vis, xg, wg, wl, bg, bl, wd_all, bd)


def _expert_lookup(table, idx):
    experts = jnp.arange(N_EXPERTS, dtype=jnp.int32)
    return jnp.sum(jnp.where(idx[..., None] == experts, table, 0), axis=-1)


def _route(top_idx, counts):
    n_tok = top_idx.shape[0]
    n_assign = n_tok * TOP_K
    assert n_assign % MOE_ROWS == 0
    n_blocks = n_assign // MOE_ROWS
    flat_e = top_idx.reshape(-1).astype(jnp.int32)
    iota = jnp.arange(n_assign, dtype=jnp.int32)
    _, order = lax.sort((flat_e, iota), num_keys=1)
    sorted_tok = order // TOP_K
    ends = jnp.cumsum(counts)
    starts = ends - counts
    first_blk = starts // MOE_ROWS
    n_vis_e = jnp.where(ends > starts, (ends - 1) // MOE_ROWS - first_blk + 1, 0)
    vis_end = jnp.cumsum(n_vis_e)
    vis_start = vis_end - n_vis_e
    n_vis = vis_end[-1]
    n_visits = n_blocks + N_EXPERTS - 1
    v = jnp.minimum(jnp.arange(n_visits, dtype=jnp.int32), n_vis - 1)
    vis_e = jnp.sum((vis_end[None, :] <= v[:, None]).astype(jnp.int32), axis=1)
    vis_blk = first_blk[vis_e] + (v - vis_start[vis_e])
    vis_lo = jnp.maximum(starts[vis_e] - vis_blk * MOE_ROWS, 0)
    return (starts, sorted_tok, vis_blk.astype(jnp.int32), vis_e.astype(jnp.int32),
            vis_lo.astype(jnp.int32), n_vis.astype(jnp.int32).reshape(1))


def _combine_kernel(yg_ref, gate_ref, h_ref, gf_ref, fg_ref, o_ref, *, final):
    gate = gate_ref[...]
    acc = yg_ref[0].astype(F32) * gate[:, 0:1]
    for k in range(1, TOP_K):
        acc = acc + yg_ref[k].astype(F32) * gate[:, k:k + 1]
    hn = h_ref[...] + gf_ref[...] * acc
    if final:
        ms = jnp.mean(hn * hn, axis=-1, keepdims=True)
        hn = hn * lax.rsqrt(ms + NORM_EPS) * fg_ref[...]
    o_ref[...] = hn


def _combine(yg, gates, h, gf, final_g, *, final, tm):
    B, T, D = h.shape
    bm = _bmap(gf)
    kern = functools.partial(_combine_kernel, final=final)
    return pl.pallas_call(
        kern,
        out_shape=jax.ShapeDtypeStruct((B, T, D), F32),
        grid=(B, T // tm),
        in_specs=[
            pl.BlockSpec((TOP_K, None, tm, D), lambda b, s: (0, b, s, 0)),
            pl.BlockSpec((None, tm, TOP_K), lambda b, s: (b, s, 0)),
            pl.BlockSpec((None, tm, D), lambda b, s: (b, s, 0)),
            pl.BlockSpec((None, 1, D), lambda b, s: (bm(b), 0, 0)),
            pl.BlockSpec((1, D), lambda b, s: (0, 0)),
        ],
        out_specs=pl.BlockSpec((None, tm, D), lambda b, s: (b, s, 0)),
        compiler_params=_cparams(2, 48),
        name="moe_combine",
    )(yg, gates, h, gf, final_g)


def _take_rows(a, idx):
    return a.at[idx].get(mode="promise_in_bounds")


def _block_diag(w):
    G, c, d = w.shape
    eye = jnp.eye(G, dtype=w.dtype)
    return (eye[:, None, :, None] * w[:, :, None, :]).reshape(G * c, G * d)


def kernel(x, c, ctx, c_ctx, ada_w, ada_b, norm_mix_g, norm_ffn_g, ev_w_in, ev_w_out, pool_w, pool_scale,
           att_sink, od_w_in, od_w_out, rw_mu, rw_w0, rw_w2, rw_a0, rw_a2, rw_g2, rw_k_k, rw_k_a, rw_r_k,
           rw_gn_g, rw_gn_b, router_w, router_b, exp_w_gu, exp_b_gu, exp_w_dn, exp_b_dn, final_g):
    B, S, D = x.shape
    C = ctx.shape[1]
    depth = ada_w.shape[0]
    DE = exp_w_dn.shape[2]

    n_mod = -(-(B + 1) // SUBLANES) * SUBLANES
    cvec = jnp.concatenate([c, c_ctx[None, :], jnp.zeros((n_mod - B - 1, D), F32)], axis=0)
    mods = _ada_all(cvec, ada_w, ada_b)

    cos_t, sin_t = _rope_tables(S)
    dft_x = dft_c = cs64 = scan_masks = head_blk = None
    if depth > 1:
        dft_x, dft_c, cs64 = _dft_tables(S), _dft_tables(C), _channel_dft()
        scan_masks, head_blk = _scan_masks()

    cnt0 = jnp.zeros((N_EXPERTS, LANES), F32)
    tm_x = 512 if S % 512 == 0 else S
    tm_o = 256 if S % 256 == 0 else S

    h, hc = x, ctx
    for i in range(depth):
        last = i == depth - 1
        j = i // 2
        m6 = mods[i].reshape(n_mod, 6, D)
        mx = [m6[:B, k][:, None, :] for k in range(6)]
        mc = [m6[B:B + 1, k][:, None, :] for k in range(6)]
        g_mix = norm_mix_g[i][None, :]
        g_ffn = norm_ffn_g[i][None, :]
        rw = router_w[i].T
        rb = router_b[i][:, None]

        if i % 2 == 0:
            assert not last, "an even final layer is not part of this block"
            w_in = ev_w_in[j].astype(BF16)
            w_out = ev_w_out[j].astype(BF16)
            pw_bd = _block_diag(pool_w[j]).astype(BF16)
            ps = pool_scale[j][None, :]
            sink = att_sink[j]
            pool_ux, qx, kvx = _even_in(h, mx[0], mx[1], g_mix, w_in, cos_t, sin_t, rope=True, tm=tm_x)
            pool_uc, qc, kvc = _even_in(hc, mc[0], mc[1], g_mix, w_in, cos_t[:C], sin_t[:C], rope=False, tm=C)
            h, fx_x, ti_x, tg_x, rk_x, cnt = _even_mix(sink, qx, kvx, kvc, pool_ux, h, mx[2], mx[3], mx[4], g_ffn,
                                                       w_out, pw_bd, ps, rw, rb, cnt0, local=True)
            hc, fx_c, ti_c, tg_c, rk_c, cnt = _even_mix(sink, qc, None, kvc, pool_uc, hc, mc[2], mc[3], mc[4], g_ffn,
                                                        w_out, pw_bd, ps, rw, rb, cnt, local=False)
        else:
            w_in = od_w_in[j].astype(BF16)
            w_out = od_w_out[j].astype(BF16)
            od_args = (w_in, rw_mu[j], rw_w0[j], rw_w2[j].astype(BF16), rw_a0[j], rw_a2[j].astype(BF16),
                       rw_k_k[j][None, :], rw_k_a[j][None, :], rw_r_k[j].reshape(1, RWKV_WIDTH))
            rx, vx, kkx, bonus_x, lwx, kdx, bdx, gcx, ux = _odd_in(h, mx[0], mx[1], g_mix, *od_args, tm=tm_x)
            rc, vc, kkc, bonus_c, lwc, kdc, bdc, gcc, uc = _odd_in(hc, mc[0], mc[1], g_mix, *od_args, tm=C)
            s0 = jnp.zeros(_scan_state_shape(B), F32)
            yf_c, yb_c, s_c = _scan(rc, vc, kkc, lwc, kdc, bdc, s0, scan_masks, head_blk)
            yf_x, yb_x, _ = _scan(rx, vx, kkx, lwx, kdx, bdx, s_c, scan_masks, head_blk)
            ro_args = (g_ffn, rw_gn_g[j][None, :], rw_gn_b[j][None, :], rw_g2[j].astype(BF16), w_out, rw, rb)
            h, fx_x, ti_x, tg_x, rk_x, cnt = _odd_out(yf_x, yb_x, bonus_x, gcx, ux, dft_x, cs64, h, mx[2], mx[3], mx[4],
                                                      *ro_args, cnt0, tm=tm_x)
            if not last:
                hc, fx_c, ti_c, tg_c, rk_c, cnt = _odd_out(yf_c, yb_c, bonus_c, gcc, uc, dft_c, cs64, hc, mc[2], mc[3],
                                                           mc[4], *ro_args, cnt, tm=C)

        n_x = B * S
        if last:
            fx_all = fx_x.reshape(n_x, D)
            ti_all = ti_x.transpose(0, 2, 1).reshape(n_x, TOP_K)
        else:
            fx_all = jnp.concatenate([fx_x.reshape(n_x, D), fx_c.reshape(B * C, D)], axis=0)
            ti_all = jnp.concatenate([ti_x.transpose(0, 2, 1).reshape(n_x, TOP_K),
                                      ti_c.transpose(0, 2, 1).reshape(B * C, TOP_K)], axis=0)
        counts = cnt[:, 0].astype(jnp.int32)
        starts, sorted_tok, vis_blk, vis_e, vis_lo, n_vis = _route(ti_all, counts)
        xg = _take_rows(fx_all, sorted_tok)
        wg, wl = _split_gu(exp_w_gu, i)
        bg = exp_b_gu[i][:, None, 0::2]
        bl = exp_b_gu[i][:, None, 1::2]
        bd = exp_b_dn[i][:, None, :]
        y = _moe_experts(vis_blk, vis_e, vis_lo, n_vis, xg, wg, wl, bg, bl, exp_w_dn, i, bd)
        tm_cx = 256 if S % 256 == 0 else S
        yg_x = _take_rows(y, (_expert_lookup(starts, ti_x) + rk_x).transpose(1, 0, 2))
        h = _combine(yg_x, tg_x.transpose(0, 2, 1), h, mx[5], final_g[None, :], final=last, tm=tm_cx)
        if not last:
            yg_c = _take_rows(y, (_expert_lookup(starts, ti_c) + rk_c).transpose(1, 0, 2))
            hc = _combine(yg_c, tg_c.transpose(0, 2, 1), hc, mc[5], final_g[None, :], final=False, tm=C)
    return h
```
